```python
import math
import jax
import jax.numpy as jnp
from jax import lax
import numpy as np

D_MODEL = 4096
BATCH = 2
SEQ = 4096
DEPTH = 4

GRID_W = 64
CTX_LEN = 256
N_MIXERS = 3
HEAD_DIM = 128
NA_HEADS = D_MODEL // HEAD_DIM
NA_WIN_H = 8
NA_WIN_W = 16
DIFF_HEADS = D_MODEL // (2 * HEAD_DIM)
Q_BLOCK = 128
SWA_Q_HEADS = D_MODEL // HEAD_DIM
SWA_KV_HEADS = SWA_Q_HEADS // 4
SWA_WINDOW = 128
SWA_BLOCK = 128
D_FF = 2 * D_MODEL
N_EXPERTS = 8
TOP_K = 2
D_EXPERT = D_MODEL // 2
ADA_RANK = D_MODEL // 8
ROPE_BASE = 10000.0
NORM_EPS = 1e-6
NEG_INF = -1e30

kernel_name = 'hybrid_prefix_diffusion_trunk'


def rms_norm(x, g):
    xf = x.astype(jnp.float32)
    y = xf * lax.rsqrt(jnp.mean(xf * xf, axis=-1, keepdims=True) + NORM_EPS)
    return (y * g.astype(jnp.float32)).astype(x.dtype)


def modulate(x, g, shift, scale):
    return rms_norm(x, g) * (1 + scale) + shift


def ada_modulation(cond, down, up, bias):
    return jnp.split((jax.nn.silu(cond) @ down) @ up + bias, 6, axis=-1)


def axial_rope_tables(n_tokens):
    t = jnp.arange(n_tokens, dtype=jnp.int32)
    row = (t // GRID_W).astype(jnp.float32)
    col = (t % GRID_W).astype(jnp.float32)
    axis_dim = HEAD_DIM // 2
    inv_freq = ROPE_BASE ** (-jnp.arange(0, axis_dim, 2, dtype=jnp.float32) / axis_dim)
    ang_r = row[:, None] * inv_freq[None, :]
    ang_c = col[:, None] * inv_freq[None, :]
    ang = jnp.concatenate([ang_r, ang_r, ang_c, ang_c], axis=-1)
    return jnp.cos(ang), jnp.sin(ang)


def apply_rope(x, cos, sin):
    x1, x2, x3, x4 = jnp.split(x, 4, axis=-1)
    rot = jnp.concatenate([-x2, x1, -x4, x3], axis=-1)
    return (x * cos[:, None, :] + rot * sin[:, None, :]).astype(x.dtype)


def softmax_f32(s):
    return jax.nn.softmax(s.astype(jnp.float32), axis=-1)


def neighborhood_attention(xn, cn, w_qkv, w_o, q_g, k_g, rpb, ctx_out):
    B, S, D = xn.shape
    L = cn.shape[1]
    H, Dh, W = NA_HEADS, HEAD_DIM, GRID_W
    rows = S // W
    kh, kw = min(NA_WIN_H, rows), NA_WIN_W
    scale = Dh ** -0.5
    qkv = (xn @ w_qkv).reshape(B, S, 3, H, Dh)
    q = rms_norm(qkv[:, :, 0], q_g).reshape(B, rows, W, H, Dh)
    k = rms_norm(qkv[:, :, 1], k_g).reshape(B, rows, W, H, Dh)
    v = qkv[:, :, 2].reshape(B, rows, W, H, Dh)
    ckv = (cn @ w_qkv[:, D:]).reshape(B, L, 2, H, Dh)
    ck = rms_norm(ckv[:, :, 0], k_g)
    cv = ckv[:, :, 1]

    row_start = jnp.clip(jnp.arange(rows) - kh // 2, 0, rows - kh)
    col = jnp.arange(W)
    col_keys = jnp.clip(col - kw // 2, 0, W - kw)[:, None] + jnp.arange(kw)[None, :]
    bias_col = rpb[:, :, col_keys - col[:, None] + (NA_WIN_W - 1)]

    def row_block(r):
        r0 = row_start[r]
        k_win = lax.dynamic_slice_in_dim(k, r0, kh, axis=1)[:, :, col_keys]
        v_win = lax.dynamic_slice_in_dim(v, r0, kh, axis=1)[:, :, col_keys]
        q_r = lax.dynamic_index_in_dim(q, r, axis=1, keepdims=False)
        bias = bias_col[:, r0 - r + jnp.arange(kh) + (NA_WIN_H - 1)]
        s_loc = (jnp.einsum('bwhd,biwjhd->bhwij', q_r, k_win).astype(jnp.float32) * scale
                 + jnp.transpose(bias, (0, 2, 1, 3)).astype(jnp.float32))
        s_ctx = jnp.einsum('bwhd,blhd->bhwl', q_r, ck).astype(jnp.float32) * scale
        p = softmax_f32(jnp.concatenate([s_loc.reshape(B, H, W, kh * kw), s_ctx], axis=-1)).astype(v.dtype)
        p_loc = p[..., :kh * kw].reshape(B, H, W, kh, kw)
        return (jnp.einsum('bhwij,biwjhd->bwhd', p_loc, v_win)
                + jnp.einsum('bhwl,blhd->bwhd', p[..., kh * kw:], cv))

    o = lax.map(row_block, jnp.arange(rows))
    y = jnp.swapaxes(o, 0, 1).reshape(B, S, D) @ w_o
    if not ctx_out:
        return y, None
    cq = rms_norm((cn @ w_qkv[:, :D]).reshape(B, L, H, Dh), q_g)
    p = softmax_f32(jnp.einsum('blhd,bmhd->bhlm', cq, ck) * scale).astype(cv.dtype)
    cy = jnp.einsum('bhlm,bmhd->blhd', p, cv).reshape(B, L, D) @ w_o
    return y, cy


def diff_attention(xn, cn, w_qkv, w_o, q_g, k_g, lq1, lk1, lq2, lk2, subln_g, lambda_init, cos, sin, ctx_out):
    B, S, D = xn.shape
    L = cn.shape[1]
    H, Dh = DIFF_HEADS, HEAD_DIM
    scale = Dh ** -0.5
    qkv = xn @ w_qkv
    q = apply_rope(rms_norm(qkv[..., :D].reshape(B, S, 2 * H, Dh), q_g), cos, sin).reshape(B, S, H, 2, Dh)
    k = apply_rope(rms_norm(qkv[..., D:2 * D].reshape(B, S, 2 * H, Dh), k_g), cos, sin).reshape(B, S, H, 2, Dh)
    v = qkv[..., 2 * D:].reshape(B, S, H, 2 * Dh)
    ckv = cn @ w_qkv[:, D:]
    ck = rms_norm(ckv[..., :D].reshape(B, L, 2 * H, Dh), k_g).reshape(B, L, H, 2, Dh)
    cv = ckv[..., D:].reshape(B, L, H, 2 * Dh)
    lam = (jnp.exp(jnp.sum(lq1.astype(jnp.float32) * lk1.astype(jnp.float32)))
           - jnp.exp(jnp.sum(lq2.astype(jnp.float32) * lk2.astype(jnp.float32))) + lambda_init)

    def diff_probs(s):
        p = softmax_f32(s)
        return p[:, :, 0] - lam * p[:, :, 1]

    def head_out(o):
        n = o.shape[1]
        return (rms_norm(o, subln_g) * (1.0 - lambda_init)).reshape(B, n, D) @ w_o

    kk = jnp.concatenate([k, ck], axis=1)
    vv = jnp.concatenate([v, cv], axis=1)
    nb = S // Q_BLOCK
    qb = jnp.swapaxes(q.reshape(B, nb, Q_BLOCK, H, 2, Dh), 0, 1)

    def block(qi):
        s = jnp.einsum('bqhcd,bkhcd->bhcqk', qi, kk).astype(jnp.float32) * scale
        return jnp.einsum('bhqk,bkhe->bqhe', diff_probs(s).astype(vv.dtype), vv)

    o = jnp.swapaxes(lax.map(block, qb), 0, 1).reshape(B, S, H, 2 * Dh)
    y = head_out(o)
    if not ctx_out:
        return y, None
    cq = rms_norm((cn @ w_qkv[:, :D]).reshape(B, L, 2 * H, Dh), q_g).reshape(B, L, H, 2, Dh)
    s = jnp.einsum('blhcd,bmhcd->bhclm', cq, ck).astype(jnp.float32) * scale
    co = jnp.einsum('bhlm,bmhe->blhe', diff_probs(s).astype(cv.dtype), cv)
    return y, head_out(co)


def window_gqa_attention(xn, cn, w_qkv, w_o, q_g, k_g, sink, cos, sin, ctx_out):
    B, S, D = xn.shape
    L = cn.shape[1]
    Hq, Hk, Dh = SWA_Q_HEADS, SWA_KV_HEADS, HEAD_DIM
    G = Hq // Hk
    dq, dk = Hq * Dh, Hk * Dh
    Bk = SWA_BLOCK
    nb = S // Bk
    scale = Dh ** -0.5
    qkv = xn @ w_qkv
    q = apply_rope(rms_norm(qkv[..., :dq].reshape(B, S, Hq, Dh), q_g), cos, sin)
    k = apply_rope(rms_norm(qkv[..., dq:dq + dk].reshape(B, S, Hk, Dh), k_g), cos, sin)
    v = qkv[..., dq + dk:].reshape(B, S, Hk, Dh)
    ckv = cn @ w_qkv[:, dq:]
    ck = rms_norm(ckv[..., :dk].reshape(B, L, Hk, Dh), k_g)
    cv = ckv[..., dk:].reshape(B, L, Hk, Dh)
    sink_logit = sink.astype(jnp.float32).reshape(Hk, G)[:, :, None, None]

    def attend(s_parts, n_q):
        s = jnp.concatenate(s_parts + [jnp.broadcast_to(sink_logit, (B, Hk, G, n_q, 1))], axis=-1)
        return softmax_f32(s)[..., :-1]

    pad = ((0, 0), (Bk, Bk), (0, 0), (0, 0))
    kp = jnp.pad(k, pad)
    vp = jnp.pad(v, pad)
    qb = jnp.swapaxes(q.reshape(B, nb, Bk, Hk, G, Dh), 0, 1)
    key_off = jnp.arange(3 * Bk) - Bk
    q_off = jnp.arange(Bk)

    def block(args):
        i, qi = args
        k_blk = lax.dynamic_slice_in_dim(kp, i * Bk, 3 * Bk, axis=1)
        v_blk = lax.dynamic_slice_in_dim(vp, i * Bk, 3 * Bk, axis=1)
        qpos = i * Bk + q_off
        kpos = i * Bk + key_off
        valid = ((jnp.abs(kpos[None, :] - qpos[:, None]) <= SWA_WINDOW)
                 & (kpos[None, :] >= 0) & (kpos[None, :] < S))
        s_loc = jnp.where(valid, jnp.einsum('bqhgd,bkhd->bhgqk', qi, k_blk).astype(jnp.float32) * scale, NEG_INF)
        s_ctx = jnp.einsum('bqhgd,blhd->bhgql', qi, ck).astype(jnp.float32) * scale
        p = attend([s_loc, s_ctx], Bk).astype(v.dtype)
        return (jnp.einsum('bhgqk,bkhd->bqhgd', p[..., :3 * Bk], v_blk)
                + jnp.einsum('bhgql,blhd->bqhgd', p[..., 3 * Bk:], cv))

    o = jnp.swapaxes(lax.map(block, (jnp.arange(nb), qb)), 0, 1).reshape(B, S, D)
    y = o @ w_o
    if not ctx_out:
        return y, None
    cq = rms_norm((cn @ w_qkv[:, :dq]).reshape(B, L, Hk, G, Dh), q_g)
    p = attend([jnp.einsum('blhgd,bmhd->bhglm', cq, ck).astype(jnp.float32) * scale], L).astype(cv.dtype)
    cy = jnp.einsum('bhglm,bmhd->blhgd', p, cv).reshape(B, L, D) @ w_o
    return y, cy


def swiglu(t, w_gate, w_up, w_down):
    return (jax.nn.silu(t @ w_gate) * (t @ w_up)) @ w_down


def moe_swiglu(t, router_w, router_b, w_gate, w_up, w_down):
    logits = (t @ router_w).astype(jnp.float32) + router_b.astype(jnp.float32)
    top_logit, top_idx = lax.top_k(logits, TOP_K)
    top_w = jax.nn.softmax(top_logit, axis=-1)
    gates = jnp.sum(jax.nn.one_hot(top_idx, N_EXPERTS, dtype=jnp.float32) * top_w[..., None], axis=1).astype(t.dtype)
    out = jnp.zeros_like(t)
    for e in range(N_EXPERTS):
        out = out + gates[:, e:e + 1] * swiglu(t, w_gate[e], w_up[e], w_down[e])
    return out


def setup_inputs(seed: int = 0) -> dict:
    key = jax.random.key(seed)
    keys = jax.random.split(key, 40)
    counter = [0]

    def nrm(shape, std):
        kk = keys[counter[0]]
        counter[0] += 1
        return jax.random.normal(kk, shape, jnp.float32) * std

    def gain(shape):
        return 1.0 + nrm(shape, 0.02)

    D, Dh = D_MODEL, HEAD_DIM
    n_na = len(range(0, DEPTH, N_MIXERS))
    n_diff = len(range(1, DEPTH, N_MIXERS))
    n_swa = len(range(2, DEPTH, N_MIXERS))
    n_dense = len(range(0, DEPTH, 2))
    n_moe = len(range(1, DEPTH, 2))
    swa_cols = SWA_Q_HEADS * Dh + 2 * SWA_KV_HEADS * Dh
    return {
        'x': nrm((BATCH, SEQ, D), 1.0),
        'c': nrm((BATCH, D), 1.0),
        'ctx': nrm((BATCH, CTX_LEN, D), 1.0),
        'c_ctx': nrm((D,), 1.0),
        'ada_down': nrm((DEPTH, D, ADA_RANK), D ** -0.5),
        'ada_up': nrm((DEPTH, ADA_RANK, 6 * D), 0.5 * ADA_RANK ** -0.5),
        'ada_bias': nrm((DEPTH, 6 * D), 0.02),
        'norm_mix': gain((DEPTH, D)),
        'norm_ffn': gain((DEPTH, D)),
        'na_w_qkv': nrm((n_na, D, 3 * D), D ** -0.5),
        'na_w_o': nrm((n_na, D, D), D ** -0.5),
        'na_q_norm': gain((n_na, Dh)),
        'na_k_norm': gain((n_na, Dh)),
        'na_rpb': nrm((n_na, NA_HEADS, 2 * NA_WIN_H - 1, 2 * NA_WIN_W - 1), 0.1),
        'diff_w_qkv': nrm((n_diff, D, 3 * D), D ** -0.5),
        'diff_w_o': nrm((n_diff, D, D), D ** -0.5),
        'diff_q_norm': gain((n_diff, Dh)),
        'diff_k_norm': gain((n_diff, Dh)),
        'diff_lambda_q1': nrm((n_diff, Dh), 0.1),
        'diff_lambda_k1': nrm((n_diff, Dh), 0.1),
        'diff_lambda_q2': nrm((n_diff, Dh), 0.1),
        'diff_lambda_k2': nrm((n_diff, Dh), 0.1),
        'diff_subln': gain((n_diff, 2 * Dh)),
        'swa_w_qkv': nrm((n_swa, D, swa_cols), D ** -0.5),
        'swa_w_o': nrm((n_swa, D, D), D ** -0.5),
        'swa_q_norm': gain((n_swa, Dh)),
        'swa_k_norm': gain((n_swa, Dh)),
        'swa_sink': nrm((n_swa, SWA_Q_HEADS), 0.5),
        'ffn_w_gate': nrm((n_dense, D, D_FF), D ** -0.5),
        'ffn_w_up': nrm((n_dense, D, D_FF), D ** -0.5),
        'ffn_w_down': nrm((n_dense, D_FF, D), D_FF ** -0.5),
        'moe_router': nrm((n_moe, D, N_EXPERTS), D ** -0.5),
        'moe_router_bias': nrm((n_moe, N_EXPERTS), 0.01),
        'moe_w_gate': nrm((n_moe, N_EXPERTS, D, D_EXPERT), D ** -0.5),
        'moe_w_up': nrm((n_moe, N_EXPERTS, D, D_EXPERT), D ** -0.5),
        'moe_w_down': nrm((n_moe, N_EXPERTS, D_EXPERT, D), D_EXPERT ** -0.5),
    }


def reference(x, c, ctx, c_ctx, ada_down, ada_up, ada_bias, norm_mix, norm_ffn,
              na_w_qkv, na_w_o, na_q_norm, na_k_norm, na_rpb,
              diff_w_qkv, diff_w_o, diff_q_norm, diff_k_norm,
              diff_lambda_q1, diff_lambda_k1, diff_lambda_q2, diff_lambda_k2, diff_subln,
              swa_w_qkv, swa_w_o, swa_q_norm, swa_k_norm, swa_sink,
              ffn_w_gate, ffn_w_up, ffn_w_down,
              moe_router, moe_router_bias, moe_w_gate, moe_w_up, moe_w_down):
    B, S, D = x.shape
    L = ctx.shape[1]
    cos, sin = axial_rope_tables(S)
    for i in range(DEPTH):
        ctx_out = i < DEPTH - 1
        sh1, sc1, g1, sh2, sc2, g2 = [m[:, None, :] for m in ada_modulation(c, ada_down[i], ada_up[i], ada_bias[i])]
        csh1, csc1, cg1, csh2, csc2, cg2 = ada_modulation(c_ctx, ada_down[i], ada_up[i], ada_bias[i])
        xn = modulate(x, norm_mix[i], sh1, sc1)
        cn = modulate(ctx, norm_mix[i], csh1, csc1)
        kind, j = i % N_MIXERS, i // N_MIXERS
        if kind == 0:
            y, cy = neighborhood_attention(xn, cn, na_w_qkv[j], na_w_o[j], na_q_norm[j], na_k_norm[j],
                                           na_rpb[j], ctx_out)
        elif kind == 1:
            y, cy = diff_attention(xn, cn, diff_w_qkv[j], diff_w_o[j], diff_q_norm[j], diff_k_norm[j],
                                   diff_lambda_q1[j], diff_lambda_k1[j], diff_lambda_q2[j], diff_lambda_k2[j],
                                   diff_subln[j], 0.8 - 0.6 * math.exp(-0.3 * i), cos, sin, ctx_out)
        else:
            y, cy = window_gqa_attention(xn, cn, swa_w_qkv[j], swa_w_o[j], swa_q_norm[j], swa_k_norm[j],
                                         swa_sink[j], cos, sin, ctx_out)
        x = x + g1 * y
        tokens = modulate(x, norm_ffn[i], sh2, sc2).reshape(B * S, D)
        if ctx_out:
            ctx = ctx + cg1 * cy
            tokens = jnp.concatenate([tokens, modulate(ctx, norm_ffn[i], csh2, csc2).reshape(B * L, D)], axis=0)
        fj = i // 2
        if i % 2 == 0:
            f = swiglu(tokens, ffn_w_gate[fj], ffn_w_up[fj], ffn_w_down[fj])
        else:
            f = moe_swiglu(tokens, moe_router[fj], moe_router_bias[fj], moe_w_gate[fj], moe_w_up[fj], moe_w_down[fj])
        x = x + g2 * f[:B * S].reshape(B, S, D)
        if ctx_out:
            ctx = ctx + cg2 * f[B * S:].reshape(B, L, D)
    return x
```

```python
import functools
import math
from typing import NamedTuple

import jax
import jax.numpy as jnp
from jax import lax
from jax.experimental import pallas as pl
from jax.experimental.pallas import tpu as pltpu

F32 = jnp.float32
BF16 = jnp.bfloat16

LANE = 128
SUBLANE_BF16 = 16
MOD_ROWS = 8
VMEM_LIMIT = 58 * 1024 * 1024
MAX_FULL_K = 4096
K_TILE = 2048

NORM_EPS = 1e-6
NEG_INF = -1e30
ROPE_BASE = 10000.0


class Cfg(NamedTuple):
    B: int
    S: int
    L: int
    D: int
    depth: int
    grid_w: int
    head_dim: int
    na_win_h: int
    na_win_w: int
    swa_window: int
    swa_block: int
    swa_group: int
    n_experts: int
    top_k: int

    @property
    def M(self):
        return self.B * (self.S + self.L)

    @property
    def MX(self):
        return self.B * self.S


def _cparams(n_axes):
    return pltpu.CompilerParams(dimension_semantics=("arbitrary",) * n_axes,
                                vmem_limit_bytes=VMEM_LIMIT)


def _pick_block(n, cap, mult):
    best = None
    for d in range(mult, min(n, cap) + 1, mult):
        if n % d == 0:
            best = d
    assert best is not None, (n, cap, mult)
    return best


def _dot(a, b):
    return jnp.dot(a, b, preferred_element_type=F32)


def _dot_t(a, b):
    return lax.dot_general(a, b, (((1,), (1,)), ((), ())), preferred_element_type=F32)


def _row_select(vecs_ref, row0, bm, cfg):
    rows = row0 + lax.broadcasted_iota(jnp.int32, (bm, 1), 0)
    g = vecs_ref[cfg.B:cfg.B + 1, :]
    for b in reversed(range(cfg.B)):
        g = jnp.where(rows < (b + 1) * cfg.S, vecs_ref[b:b + 1, :], g)
    return g


def _ada_kernel(c_ref, down_ref, up_ref, b_ref, o_ref):
    c = c_ref[...]
    sc = (c * jax.nn.sigmoid(c)).astype(BF16)
    t = _dot(sc, down_ref[...].astype(BF16))
    o_ref[...] = _dot(t.astype(BF16), up_ref[...].astype(BF16)) + b_ref[...]


def _ada_all(cond, ada_down, ada_up, ada_bias):
    depth, D, R = ada_down.shape
    N = ada_up.shape[2]
    bn = _pick_block(N, 3072, LANE)
    return pl.pallas_call(
        _ada_kernel,
        out_shape=jax.ShapeDtypeStruct((depth, MOD_ROWS, N), F32),
        grid=(depth, N // bn),
        in_specs=[
            pl.BlockSpec((MOD_ROWS, D), lambda l, j: (0, 0)),
            pl.BlockSpec((None, D, R), lambda l, j: (l, 0, 0)),
            pl.BlockSpec((None, R, bn), lambda l, j: (l, 0, j)),
            pl.BlockSpec((None, 1, bn), lambda l, j: (l, 0, j)),
        ],
        out_specs=pl.BlockSpec((None, MOD_ROWS, bn), lambda l, j: (l, 0, j)),
        compiler_params=_cparams(2),
        name="ada_modulation",
    )(cond, ada_down, ada_up, ada_bias.reshape(depth, 1, N))


def _modulate_kernel(x_ref, g_ref, sh_ref, sc_ref, o_ref, *, bm, cfg):
    row0 = pl.program_id(0) * bm
    t = jnp.where(row0 < cfg.MX, row0 // cfg.S, cfg.B)
    x = x_ref[...]
    ms = jnp.mean(x * x, axis=-1, keepdims=True)
    y = x * lax.rsqrt(ms + NORM_EPS) * g_ref[...]
    o_ref[...] = (y * (1 + sc_ref[pl.ds(t, 1), :]) + sh_ref[pl.ds(t, 1), :]).astype(BF16)


def _modulate(h, gain, mods, shift_chunk, scale_chunk, n_rows, cfg):
    D = cfg.D
    bm = _pick_block(math.gcd(cfg.S, cfg.L), 256, SUBLANE_BF16)
    return pl.pallas_call(
        functools.partial(_modulate_kernel, bm=bm, cfg=cfg),
        out_shape=jax.ShapeDtypeStruct((n_rows, D), BF16),
        grid=(n_rows // bm,),
        in_specs=[
            pl.BlockSpec((bm, D), lambda i: (i, 0)),
            pl.BlockSpec((1, D), lambda i: (0, 0)),
            pl.BlockSpec((MOD_ROWS, D), lambda i: (0, shift_chunk)),
            pl.BlockSpec((MOD_ROWS, D), lambda i: (0, scale_chunk)),
        ],
        out_specs=pl.BlockSpec((bm, D), lambda i: (i, 0)),
        compiler_params=_cparams(1),
        name="modulate",
    )(h, gain.reshape(1, D), mods, mods)


def _qkv_kernel(x_ref, w_ref, g_ref, cos_ref, sa_ref, sb_ref, o_ref, wb_ref, *, n_norm_blocks, rope, bn):
    j = pl.program_id(0)

    @pl.when(pl.program_id(1) == 0)
    def _():
        wb_ref[...] = w_ref[...].astype(BF16)

    acc = _dot(x_ref[...], wb_ref[...])

    @pl.when(j < n_norm_blocks)
    def _():
        for hh in range(bn // LANE):
            sl = slice(hh * LANE, (hh + 1) * LANE)
            xh = acc[:, sl]
            ms = jnp.mean(xh * xh, axis=-1, keepdims=True)
            y = xh * lax.rsqrt(ms + NORM_EPS) * g_ref[:, sl]
            if rope:
                y = (y * cos_ref[...] + pltpu.roll(y, 3 * LANE // 4, 1) * sa_ref[...]
                     + pltpu.roll(y, LANE // 4, 1) * sb_ref[...])
            o_ref[:, sl] = y.astype(BF16)

    @pl.when(j >= n_norm_blocks)
    def _():
        o_ref[...] = acc.astype(BF16)


def _qkv_proj(xn, w_all, layer, gains, n_norm_cols, rope_tabs, cfg):
    M, D = xn.shape
    N = w_all.shape[2]
    bn = math.gcd(math.gcd(N, n_norm_cols), 512)
    bm = _pick_block(M, 1152, SUBLANE_BF16)
    assert bn % LANE == 0
    rope = rope_tabs is not None
    cos, sa, sb = rope_tabs if rope else (jnp.zeros((M, LANE), F32),) * 3
    tab_spec = pl.BlockSpec((bm, LANE), lambda j, m: (m, 0))
    return pl.pallas_call(
        functools.partial(_qkv_kernel, n_norm_blocks=n_norm_cols // bn, rope=rope, bn=bn),
        out_shape=jax.ShapeDtypeStruct((M, N), BF16),
        grid=(N // bn, M // bm),
        in_specs=[
            pl.BlockSpec((bm, D), lambda j, m: (m, 0)),
            pl.BlockSpec((None, D, bn), lambda j, m: (layer, 0, j)),
            pl.BlockSpec((1, bn), lambda j, m: (0, j)),
            tab_spec, tab_spec, tab_spec,
        ],
        out_specs=pl.BlockSpec((bm, bn), lambda j, m: (m, j)),
        scratch_shapes=[pltpu.VMEM((D, bn), BF16)],
        compiler_params=_cparams(2),
        name="qkv_proj",
    )(xn, w_all, gains, cos, sa, sb)


def _mm_res_kernel(a_ref, w_ref, res_ref, gate_ref, o_ref, wb_ref, *acc, nk, bm, cfg):
    k = pl.program_id(1)
    m = pl.program_id(2)

    @pl.when(m == 0)
    def _():
        wb_ref[...] = w_ref[...].astype(BF16)

    part = _dot(a_ref[...], wb_ref[...])

    def epilogue(total):
        o_ref[...] = res_ref[...] + _row_select(gate_ref, m * bm, bm, cfg) * total

    if nk == 1:
        epilogue(part)
    else:
        acc_ref = acc[0]
        rows = pl.ds(pl.multiple_of(m * bm, 8), bm)

        @pl.when(k == 0)
        def _():
            acc_ref[rows, :] = part

        @pl.when(jnp.logical_and(k > 0, k < nk - 1))
        def _():
            acc_ref[rows, :] += part

        @pl.when(k == nk - 1)
        def _():
            epilogue(acc_ref[rows, :] + part)


def _mm_res(a, w, res, mods, gate_chunk, n_rows, cfg):
    K, N = w.shape
    bn = 512
    bk = K if K <= MAX_FULL_K else _pick_block(K, K_TILE, LANE)
    nk = K // bk
    bm = _pick_block(n_rows, 1152, SUBLANE_BF16)
    last = nk - 1
    scratch = [pltpu.VMEM((bk, bn), BF16)]
    if nk > 1:
        scratch.append(pltpu.VMEM((n_rows, bn), F32))
    return pl.pallas_call(
        functools.partial(_mm_res_kernel, nk=nk, bm=bm, cfg=cfg),
        out_shape=jax.ShapeDtypeStruct((n_rows, N), F32),
        grid=(N // bn, nk, n_rows // bm),
        in_specs=[
            pl.BlockSpec((bm, bk), lambda j, k, m: (m, k)),
            pl.BlockSpec((bk, bn), lambda j, k, m: (k, j)),
            pl.BlockSpec((bm, bn), lambda j, k, m: (jnp.where(k == last, m, 0), j)),
            pl.BlockSpec((MOD_ROWS, bn), lambda j, k, m: (0, gate_chunk * (N // bn) + j)),
        ],
        out_specs=pl.BlockSpec((bm, bn), lambda j, k, m: (jnp.where(k == last, m, 0), j)),
        scratch_shapes=scratch,
        compiler_params=_cparams(3),
        name="matmul_residual",
    )(a, w, res, mods)


def _swiglu_kernel(*refs, gated, nbe):
    if gated:
        x_ref, wg_ref, wu_ref, gates_ref, o_ref, wgb_ref, wub_ref = refs
    else:
        x_ref, wg_ref, wu_ref, o_ref, wgb_ref, wub_ref = refs

    @pl.when(pl.program_id(1) == 0)
    def _():
        wgb_ref[...] = wg_ref[...].astype(BF16)
        wub_ref[...] = wu_ref[...].astype(BF16)

    x = x_ref[...]
    g = _dot(x, wgb_ref[...])
    u = _dot(x, wub_ref[...])
    hid = g * jax.nn.sigmoid(g) * u
    if gated:
        e = pl.program_id(0) // nbe
        gates = gates_ref[...]
        lane = lax.broadcasted_iota(jnp.int32, gates.shape, 1)
        hid = hid * jnp.sum(jnp.where(lane == e, gates, 0.0), axis=-1, keepdims=True)
    o_ref[...] = hid.astype(BF16)


def _swiglu_up(t, w_gate, w_up, gates):
    M, D = t.shape
    E, _, F = w_gate.shape
    bn = 256
    nbe = F // bn
    bm = _pick_block(M, 1152, SUBLANE_BF16)
    gated = gates is not None
    w_spec = pl.BlockSpec((None, D, bn), lambda j, m: (j // nbe, 0, j % nbe))
    in_specs = [pl.BlockSpec((bm, D), lambda j, m: (m, 0)), w_spec, w_spec]
    args = [t, w_gate, w_up]
    if gated:
        in_specs.append(pl.BlockSpec((bm, LANE), lambda j, m: (m, 0)))
        args.append(gates)
    return pl.pallas_call(
        functools.partial(_swiglu_kernel, gated=gated, nbe=nbe),
        out_shape=jax.ShapeDtypeStruct((M, E * F), BF16),
        grid=(E * nbe, M // bm),
        in_specs=in_specs,
        out_specs=pl.BlockSpec((bm, bn), lambda j, m: (m, j)),
        scratch_shapes=[pltpu.VMEM((D, bn), BF16), pltpu.VMEM((D, bn), BF16)],
        compiler_params=_cparams(2),
        name="swiglu_up",
    )(*args)


def _router_kernel(t_ref, w_ref, b_ref, o_ref):
    logits = _dot(t_ref[...], w_ref[...].astype(BF16)) + b_ref[...]
    lane = lax.broadcasted_iota(jnp.int32, logits.shape, 1).astype(F32)
    top1 = jnp.max(logits, axis=-1, keepdims=True)
    i1 = jnp.min(jnp.where(logits == top1, lane, float(LANE)), axis=-1, keepdims=True)
    rest = jnp.where(lane == i1, -jnp.inf, logits)
    top2 = jnp.max(rest, axis=-1, keepdims=True)
    i2 = jnp.min(jnp.where(rest == top2, lane, float(LANE)), axis=-1, keepdims=True)
    e2 = jnp.exp(top2 - top1)
    inv = 1.0 / (1.0 + e2)
    o_ref[...] = jnp.where(lane == i1, inv, 0.0) + jnp.where(lane == i2, e2 * inv, 0.0)


def _router(t, router_w, router_b, cfg):
    M, D = t.shape
    E = cfg.n_experts
    assert cfg.top_k == 2 and E <= LANE
    w = jnp.zeros((D, LANE), F32).at[:, :E].set(router_w)
    b = jnp.full((1, LANE), NEG_INF, F32).at[0, :E].set(router_b)
    bm = _pick_block(M, 1152, SUBLANE_BF16)
    return pl.pallas_call(
        _router_kernel,
        out_shape=jax.ShapeDtypeStruct((M, LANE), F32),
        grid=(M // bm,),
        in_specs=[
            pl.BlockSpec((bm, D), lambda m: (m, 0)),
            pl.BlockSpec((D, LANE), lambda m: (0, 0)),
            pl.BlockSpec((1, LANE), lambda m: (0, 0)),
        ],
        out_specs=pl.BlockSpec((bm, LANE), lambda m: (m, 0)),
        compiler_params=_cparams(1),
        name="moe_router",
    )(t, w, b)


def _na_kernel(q_ref, k_ref, v_ref, ck_ref, cv_ref, bias_ref, o_ref, *, rows, W, kh, scale):
    ck = ck_ref[...]
    cv = cv_ref[...]

    def body(r, carry):
        r0 = jnp.clip(r - kh // 2, 0, rows - kh)
        variant = r0 - r + (kh - 1)
        q = q_ref[pl.ds(pl.multiple_of(r * W, W), W), :]
        kstart = pl.multiple_of(r0 * W, W)
        kw = k_ref[pl.ds(kstart, kh * W), :]
        vw = v_ref[pl.ds(kstart, kh * W), :]
        s = _dot_t(q, kw) * scale + bias_ref[variant]
        sc = _dot_t(q, ck) * scale
        m = jnp.maximum(jnp.max(s, axis=-1, keepdims=True), jnp.max(sc, axis=-1, keepdims=True))
        e = jnp.exp(s - m)
        ec = jnp.exp(sc - m)
        inv = 1.0 / (jnp.sum(e, axis=-1, keepdims=True) + jnp.sum(ec, axis=-1, keepdims=True))
        o = _dot((e * inv).astype(BF16), vw) + _dot((ec * inv).astype(BF16), cv)
        o_ref[pl.ds(pl.multiple_of(r * W, W), W), :] = o.astype(BF16)
        return carry

    lax.fori_loop(0, rows, body, 0)


def _na_bias_table(rpb, cfg):
    W, kw = cfg.grid_w, cfg.na_win_w
    kh = min(cfg.na_win_h, cfg.S // W)
    H = rpb.shape[0]
    wq = jnp.arange(W)[:, None]
    kc = jnp.arange(W)[None, :]
    c0 = jnp.clip(wq - kw // 2, 0, W - kw)
    valid = (kc >= c0) & (kc < c0 + kw)
    cidx = jnp.clip(kc - wq + (cfg.na_win_w - 1), 0, 2 * cfg.na_win_w - 2)
    toe = jnp.where(valid[None, None], rpb[:, :, cidx], NEG_INF)
    ridx = jnp.arange(kh)[:, None] + jnp.arange(kh)[None, :] + (cfg.na_win_h - kh)
    tab = toe[:, ridx]
    return jnp.transpose(tab, (0, 1, 3, 2, 4)).reshape(H, kh, W, kh * W).astype(F32)


def _na_attention(qkv, rpb, n_out_rows, cfg):
    B, S, L, D, W, Dh = cfg.B, cfg.S, cfg.L, cfg.D, cfg.grid_w, cfg.head_dim
    H = D // Dh
    rows = S // W
    kh = min(cfg.na_win_h, rows)
    bias = _na_bias_table(rpb, cfg)
    cblk = cfg.MX // L
    return pl.pallas_call(
        functools.partial(_na_kernel, rows=rows, W=W, kh=kh, scale=Dh ** -0.5),
        out_shape=jax.ShapeDtypeStruct((n_out_rows, D), BF16),
        grid=(H, B),
        in_specs=[
            pl.BlockSpec((S, Dh), lambda h, b: (b, h)),
            pl.BlockSpec((S, Dh), lambda h, b: (b, H + h)),
            pl.BlockSpec((S, Dh), lambda h, b: (b, 2 * H + h)),
            pl.BlockSpec((L, Dh), lambda h, b: (cblk + b, H + h)),
            pl.BlockSpec((L, Dh), lambda h, b: (cblk + b, 2 * H + h)),
            pl.BlockSpec((None, kh, W, kh * W), lambda h, b: (h, 0, 0, 0)),
        ],
        out_specs=pl.BlockSpec((S, Dh), lambda h, b: (b, h)),
        compiler_params=_cparams(2),
        name="neighborhood_attention",
    )(qkv, qkv, qkv, qkv, qkv, bias)


def _gqa_kernel(*refs, G, has_local, has_sink, has_alias, bq, win, S, scale, Dh):
    refs = list(refs)
    sink_ref = refs.pop(0) if has_sink else None
    q_ref = refs.pop(0)
    k_ref, v_ref = (refs.pop(0), refs.pop(0)) if has_local else (None, None)
    ck_ref, cv_ref = refs.pop(0), refs.pop(0)
    if has_alias:
        refs.pop(0)
    o_ref = refs.pop(0)
    hk = pl.program_id(1)
    i = pl.program_id(2)
    ck = ck_ref[...]
    cv = cv_ref[...]
    if has_local:
        start = pl.multiple_of(jnp.clip(i * bq - bq, 0, S - 3 * bq), bq)
        kw = k_ref[pl.ds(start, 3 * bq), :]
        vw = v_ref[pl.ds(start, 3 * bq), :]
        kpos = start + lax.broadcasted_iota(jnp.int32, (1, 3 * bq), 1)
        qpos = i * bq + lax.broadcasted_iota(jnp.int32, (bq, 1), 0)
        valid = jnp.abs(kpos - qpos) <= win
    for g in range(G):
        sl = slice(g * Dh, (g + 1) * Dh)
        q = q_ref[:, sl]
        sc = _dot_t(q, ck) * scale
        m = jnp.max(sc, axis=-1, keepdims=True)
        if has_local:
            s = jnp.where(valid, _dot_t(q, kw) * scale, NEG_INF)
            m = jnp.maximum(m, jnp.max(s, axis=-1, keepdims=True))
        if has_sink:
            sk = sink_ref[hk, g]
            m = jnp.maximum(m, sk)
        ec = jnp.exp(sc - m)
        l = jnp.sum(ec, axis=-1, keepdims=True)
        if has_local:
            e = jnp.exp(s - m)
            l = l + jnp.sum(e, axis=-1, keepdims=True)
        if has_sink:
            l = l + jnp.exp(sk - m)
        inv = 1.0 / l
        o = _dot((ec * inv).astype(BF16), cv)
        if has_local:
            o = o + _dot((e * inv).astype(BF16), vw)
        o_ref[:, sl] = o.astype(BF16)


def _gqa_attention(qkv, sink, q_heads, kv_heads, local, o_prev, n_out_rows, cfg):
    B, S, L, D, Dh = cfg.B, cfg.S, cfg.L, cfg.D, cfg.head_dim
    G = q_heads // kv_heads
    kcol = q_heads
    vcol = q_heads + kv_heads
    cblk = cfg.MX // L
    has_sink = sink is not None
    bq = cfg.swa_block if local else L
    nq = S // bq if local else 1
    qrow = (lambda b, i: b * nq + i) if local else (lambda b, i: cblk + b)
    in_specs, args = [], []
    if has_sink:
        in_specs.append(pl.BlockSpec(memory_space=pltpu.SMEM))
        args.append(sink.reshape(kv_heads, G).astype(F32))
    in_specs.append(pl.BlockSpec((bq, G * Dh), lambda b, h, i: (qrow(b, i), h)))
    args.append(qkv)
    if local:
        in_specs += [pl.BlockSpec((S, Dh), lambda b, h, i: (b, kcol + h)),
                     pl.BlockSpec((S, Dh), lambda b, h, i: (b, vcol + h))]
        args += [qkv, qkv]
    in_specs += [pl.BlockSpec((L, Dh), lambda b, h, i: (cblk + b, kcol + h)),
                 pl.BlockSpec((L, Dh), lambda b, h, i: (cblk + b, vcol + h))]
    args += [qkv, qkv]
    aliases = {}
    if o_prev is not None:
        in_specs.append(pl.BlockSpec(memory_space=pl.ANY))
        aliases = {len(args): 0}
        args.append(o_prev)
    return pl.pallas_call(
        functools.partial(_gqa_kernel, G=G, has_local=local, has_sink=has_sink, has_alias=o_prev is not None,
                          bq=bq, win=cfg.swa_window, S=S, scale=Dh ** -0.5, Dh=Dh),
        out_shape=jax.ShapeDtypeStruct((n_out_rows, D), BF16),
        grid=(B, kv_heads, nq),
        in_specs=in_specs,
        out_specs=pl.BlockSpec((bq, G * Dh), lambda b, h, i: (qrow(b, i), h)),
        input_output_aliases=aliases,
        compiler_params=_cparams(3),
        name="gqa_window_attention" if local else "gqa_context_attention",
    )(*args)


def _diff_kernel(*refs, has_x, has_alias, chunk, S, L, lambda_init, scale, Dh):
    refs = list(refs)
    lq1, lk1, lq2, lk2, sub_ref, q_ref = refs[:6]
    refs = refs[6:]
    kx_ref, vx_ref = (refs.pop(0), refs.pop(0)) if has_x else (None, None)
    kc_ref, vc_ref = refs.pop(0), refs.pop(0)
    if has_alias:
        refs.pop(0)
    o_ref, e_ref = refs
    lam = (jnp.exp(jnp.sum(lq1[...] * lk1[...], axis=-1, keepdims=True))
           - jnp.exp(jnp.sum(lq2[...] * lk2[...], axis=-1, keepdims=True)) + lambda_init)
    chunks = []
    if has_x:
        chunks += [(kx_ref, vx_ref, c * chunk, chunk, c * chunk) for c in range(S // chunk)]
    off = S if has_x else 0
    chunks.append((kc_ref, vc_ref, 0, L, off))
    bq = q_ref.shape[0]
    coef = []
    for mi in range(2):
        sl = slice(mi * Dh, (mi + 1) * Dh)
        q = q_ref[:, sl]
        mx = jnp.full((bq, 1), -jnp.inf, F32)
        l = jnp.zeros((bq, 1), F32)
        maxes = []
        for k_ref, _, r0, n, e0 in chunks:
            s = _dot_t(q, k_ref[r0:r0 + n, sl]) * scale
            mn = jnp.maximum(mx, jnp.max(s, axis=-1, keepdims=True))
            e = jnp.exp(s - mn)
            l = l * jnp.exp(mx - mn) + jnp.sum(e, axis=-1, keepdims=True)
            e_ref[mi, :, e0:e0 + n] = e
            maxes.append(mn)
            mx = mn
        inv = 1.0 / l
        coef.append([jnp.exp(mc - mx) * inv for mc in maxes])
    acc = jnp.zeros((bq, 2 * Dh), F32)
    for ci, (_, v_ref, r0, n, e0) in enumerate(chunks):
        p = e_ref[0, :, e0:e0 + n] * coef[0][ci] - e_ref[1, :, e0:e0 + n] * (lam * coef[1][ci])
        acc = acc + _dot(p.astype(BF16), v_ref[r0:r0 + n, :])
    ms = jnp.mean(acc * acc, axis=-1, keepdims=True)
    y = acc * lax.rsqrt(ms + NORM_EPS) * sub_ref[...] * (1.0 - lambda_init)
    o_ref[...] = y.astype(BF16)


def _diff_attention(qkv, lams, subln, lambda_init, main, o_prev, n_out_rows, cfg):
    B, S, L, D, Dh = cfg.B, cfg.S, cfg.L, cfg.D, cfg.head_dim
    H = D // (2 * Dh)
    HW = 2 * Dh
    cblk = cfg.MX // L
    bq = 256 if main else L
    nq = S // bq if main else 1
    nkeys = S + L if main else L
    qrow = (lambda b, i: b * nq + i) if main else (lambda b, i: cblk + b)
    vec = pl.BlockSpec((1, Dh), lambda b, h, i: (0, 0))
    in_specs = [vec, vec, vec, vec,
                pl.BlockSpec((1, HW), lambda b, h, i: (0, 0)),
                pl.BlockSpec((bq, HW), lambda b, h, i: (qrow(b, i), h))]
    args = [v.reshape(1, Dh).astype(F32) for v in lams] + [subln.reshape(1, HW).astype(F32), qkv]
    if main:
        in_specs += [pl.BlockSpec((S, HW), lambda b, h, i: (b, H + h)),
                     pl.BlockSpec((S, HW), lambda b, h, i: (b, 2 * H + h))]
        args += [qkv, qkv]
    in_specs += [pl.BlockSpec((L, HW), lambda b, h, i: (cblk + b, H + h)),
                 pl.BlockSpec((L, HW), lambda b, h, i: (cblk + b, 2 * H + h))]
    args += [qkv, qkv]
    aliases = {}
    if o_prev is not None:
        in_specs.append(pl.BlockSpec(memory_space=pl.ANY))
        aliases = {len(args): 0}
        args.append(o_prev)
    return pl.pallas_call(
        functools.partial(_diff_kernel, has_x=main, has_alias=o_prev is not None, chunk=512, S=S, L=L,
                          lambda_init=lambda_init, scale=Dh ** -0.5, Dh=Dh),
        out_shape=jax.ShapeDtypeStruct((n_out_rows, D), BF16),
        grid=(B, H, nq),
        in_specs=in_specs,
        out_specs=pl.BlockSpec((bq, HW), lambda b, h, i: (qrow(b, i), h)),
        scratch_shapes=[pltpu.VMEM((2, bq, nkeys), F32)],
        input_output_aliases=aliases,
        compiler_params=_cparams(3),
        name="diff_attention" if main else "diff_context_attention",
    )(*args)


def _rope_tables(cfg):
    Dh = cfg.head_dim
    t = jnp.arange(cfg.S, dtype=jnp.int32)
    row = (t // cfg.grid_w).astype(F32)
    col = (t % cfg.grid_w).astype(F32)
    axis_dim = Dh // 2
    inv_freq = ROPE_BASE ** (-jnp.arange(0, axis_dim, 2, dtype=F32) / axis_dim)
    ang_r = row[:, None] * inv_freq[None, :]
    ang_c = col[:, None] * inv_freq[None, :]
    ang = jnp.concatenate([ang_r, ang_r, ang_c, ang_c], axis=-1)
    cos, sin = jnp.cos(ang), jnp.sin(ang)
    quarter = jnp.arange(Dh) // (Dh // 4)
    sin_a = jnp.where(quarter % 2 == 0, -sin, 0.0)
    sin_b = jnp.where(quarter % 2 == 1, sin, 0.0)
    n_ctx = cfg.B * cfg.L

    def full(tab, fill):
        return jnp.concatenate([jnp.tile(tab, (cfg.B, 1)), jnp.full((n_ctx, Dh), fill, F32)], axis=0)

    return full(cos, 1.0), full(sin_a, 0.0), full(sin_b, 0.0)


def _head_gains(q_gain, k_gain, n_q, n_k, n_v_cols):
    return jnp.concatenate([jnp.tile(q_gain, n_q), jnp.tile(k_gain, n_k),
                            jnp.ones((n_v_cols,), F32)]).reshape(1, -1).astype(F32)


def _forward(cfg, x, c, ctx, c_ctx, ada_down, ada_up, ada_bias, norm_mix, norm_ffn,
             na_w_qkv, na_w_o, na_q_norm, na_k_norm, na_rpb,
             diff_w_qkv, diff_w_o, diff_q_norm, diff_k_norm,
             diff_lambda_q1, diff_lambda_k1, diff_lambda_q2, diff_lambda_k2, diff_subln,
             swa_w_qkv, swa_w_o, swa_q_norm, swa_k_norm, swa_sink,
             ffn_w_gate, ffn_w_up, ffn_w_down,
             moe_router, moe_router_bias, moe_w_gate, moe_w_up, moe_w_down):
    B, S, L, D, Dh = cfg.B, cfg.S, cfg.L, cfg.D, cfg.head_dim
    M, MX = cfg.M, cfg.MX
    heads = D // Dh
    h = jnp.concatenate([x.reshape(MX, D), ctx.reshape(B * L, D)], axis=0).astype(F32)
    cond = jnp.zeros((MOD_ROWS, D), F32).at[:B].set(c).at[B].set(c_ctx)
    mods_all = _ada_all(cond, ada_down, ada_up, ada_bias)
    rope_tabs = _rope_tables(cfg)

    for i in range(cfg.depth):
        ctx_out = i < cfg.depth - 1
        n_rows = M if ctx_out else MX
        mods = mods_all[i]
        xn = _modulate(h, norm_mix[i], mods, 0, 1, M, cfg)
        kind, j = i % 3, i // 3
        if kind == 0:
            gains = _head_gains(na_q_norm[j], na_k_norm[j], heads, heads, D)
            qkv = _qkv_proj(xn, na_w_qkv, j, gains, 2 * D, None, cfg)
            o = _na_attention(qkv, na_rpb[j], n_rows, cfg)
            if ctx_out:
                o = _gqa_attention(qkv, None, heads, heads, False, o, n_rows, cfg)
            w_o = na_w_o[j]
        elif kind == 1:
            gains = _head_gains(diff_q_norm[j], diff_k_norm[j], heads, heads, D)
            qkv = _qkv_proj(xn, diff_w_qkv, j, gains, 2 * D, rope_tabs, cfg)
            lams = (diff_lambda_q1[j], diff_lambda_k1[j], diff_lambda_q2[j], diff_lambda_k2[j])
            lambda_init = 0.8 - 0.6 * math.exp(-0.3 * i)
            o = _diff_attention(qkv, lams, diff_subln[j], lambda_init, True, None, n_rows, cfg)
            if ctx_out:
                o = _diff_attention(qkv, lams, diff_subln[j], lambda_init, False, o, n_rows, cfg)
            w_o = diff_w_o[j]
        else:
            kvh = heads // cfg.swa_group
            gains = _head_gains(swa_q_norm[j], swa_k_norm[j], heads, kvh, kvh * Dh)
            qkv = _qkv_proj(xn, swa_w_qkv, j, gains, (heads + kvh) * Dh, rope_tabs, cfg)
            o = _gqa_attention(qkv, swa_sink[j], heads, kvh, True, None, n_rows, cfg)
            if ctx_out:
                o = _gqa_attention(qkv, swa_sink[j], heads, kvh, False, o, n_rows, cfg)
            w_o = swa_w_o[j]
        h = _mm_res(o, w_o, h, mods, 2, n_rows, cfg)
        t = _modulate(h, norm_ffn[i], mods, 3, 4, n_rows, cfg)
        fj = i // 2
        if i % 2 == 0:
            hid = _swiglu_up(t, ffn_w_gate[fj][None], ffn_w_up[fj][None], None)
            w_down = ffn_w_down[fj]
        else:
            gates = _router(t, moe_router[fj], moe_router_bias[fj], cfg)
            hid = _swiglu_up(t, moe_w_gate[fj], moe_w_up[fj], gates)
            w_down = moe_w_down[fj].reshape(-1, D)
        h = _mm_res(hid, w_down, h, mods, 5, n_rows, cfg)
    return h[:MX].reshape(B, S, D)


_CFG = Cfg(B=2, S=4096, L=256, D=4096, depth=4, grid_w=64, head_dim=128, na_win_h=8, na_win_w=16,
           swa_window=128, swa_block=128, swa_group=4, n_experts=8, top_k=2)


def kernel(x, c, ctx, c_ctx, ada_down, ada_up, ada_bias, norm_mix, norm_ffn, na_w_qkv, na_w_o, na_q_norm, na_k_norm, na_rpb, diff_w_qkv, diff_w_o, diff_q_norm, diff_k_norm, diff_lambda_q1, diff_lambda_k1, diff_lambda_q2, diff_lambda_k2, diff_subln, swa_w_qkv, swa_w_o, swa_q_norm, swa_k_norm, swa_sink, ffn_w_gate, ffn_w_up, ffn_w_down, moe_router, moe_router_bias, moe_w_gate, moe_w_up, moe_w_down):
    return _forward(_CFG, x, c, ctx, c_ctx, ada_down, ada_up, ada_bias, norm_mix, norm_ffn, na_w_qkv, na_w_o, na_q_norm, na_k_norm, na_rpb, diff_w_qkv, diff_w_o, diff_q_norm, diff_k_norm, diff_lambda_q1, diff_lambda_k1, diff_lambda_q2, diff_lambda_k2, diff_subln, swa_w_qkv, swa_w_o, swa_q_norm, swa_k_norm, swa_sink, ffn_w_gate, ffn_w_up, ffn_w_down, moe_router, moe_router_bias, moe_w_gate, moe_w_up, moe_w_down)
```

```python
import functools
import math
from typing import NamedTuple

import jax
import jax.numpy as jnp
from jax import lax
from jax.experimental import pallas as pl
from jax.experimental.pallas import tpu as pltpu

F32 = jnp.float32
BF16 = jnp.bfloat16

LANE = 128
SUBLANE_BF16 = 16
MOD_ROWS = 8
VMEM_LIMIT = 58 * 1024 * 1024
MAX_FULL_K = 4096
K_TILE = 2048

NA_GROUP_UNROLL = 2
GQA_SUBBLOCKS = 4

NORM_EPS = 1e-6
NEG_INF = -1e30
ROPE_BASE = 10000.0
LOG2E = 1.4426950408889634


class Cfg(NamedTuple):
    B: int
    S: int
    L: int
    D: int
    depth: int
    grid_w: int
    head_dim: int
    na_win_h: int
    na_win_w: int
    swa_window: int
    swa_block: int
    swa_group: int
    n_experts: int
    top_k: int

    @property
    def M(self):
        return self.B * (self.S + self.L)

    @property
    def MX(self):
        return self.B * self.S


def _cparams(n_axes):
    return pltpu.CompilerParams(dimension_semantics=("arbitrary",) * n_axes,
                                vmem_limit_bytes=VMEM_LIMIT)


def _pick_block(n, cap, mult):
    best = None
    for d in range(mult, min(n, cap) + 1, mult):
        if n % d == 0:
            best = d
    assert best is not None, (n, cap, mult)
    return best


def _dot(a, b):
    return jnp.dot(a, b, preferred_element_type=F32)


def _dot_t(a, b):
    return lax.dot_general(a, b, (((1,), (1,)), ((), ())), preferred_element_type=F32)


def _row_select(vecs_ref, row0, bm, cfg):
    rows = row0 + lax.broadcasted_iota(jnp.int32, (bm, 1), 0)
    g = vecs_ref[cfg.B:cfg.B + 1, :]
    for b in reversed(range(cfg.B)):
        g = jnp.where(rows < (b + 1) * cfg.S, vecs_ref[b:b + 1, :], g)
    return g


def _ada_kernel(c_ref, down_ref, up_ref, b_ref, o_ref):
    c = c_ref[...]
    sc = (c * jax.nn.sigmoid(c)).astype(BF16)
    t = _dot(sc, down_ref[...].astype(BF16))
    o_ref[...] = _dot(t.astype(BF16), up_ref[...].astype(BF16)) + b_ref[...]


def _ada_all(cond, ada_down, ada_up, ada_bias):
    depth, D, R = ada_down.shape
    N = ada_up.shape[2]
    bn = _pick_block(N, 3072, LANE)
    return pl.pallas_call(
        _ada_kernel,
        out_shape=jax.ShapeDtypeStruct((depth, MOD_ROWS, N), F32),
        grid=(depth, N // bn),
        in_specs=[
            pl.BlockSpec((MOD_ROWS, D), lambda l, j: (0, 0)),
            pl.BlockSpec((None, D, R), lambda l, j: (l, 0, 0)),
            pl.BlockSpec((None, R, bn), lambda l, j: (l, 0, j)),
            pl.BlockSpec((None, 1, bn), lambda l, j: (l, 0, j)),
        ],
        out_specs=pl.BlockSpec((None, MOD_ROWS, bn), lambda l, j: (l, 0, j)),
        compiler_params=_cparams(2),
        name="ada_modulation",
    )(cond, ada_down, ada_up, ada_bias.reshape(depth, 1, N))


def _modulate_kernel(x_ref, g_ref, sh_ref, sc_ref, o_ref, *, bm, cfg):
    row0 = pl.program_id(0) * bm
    t = jnp.where(row0 < cfg.MX, row0 // cfg.S, cfg.B)
    x = x_ref[...]
    ms = jnp.mean(x * x, axis=-1, keepdims=True)
    y = x * lax.rsqrt(ms + NORM_EPS) * g_ref[...]
    o_ref[...] = (y * (1 + sc_ref[pl.ds(t, 1), :]) + sh_ref[pl.ds(t, 1), :]).astype(BF16)


def _modulate(h, gain, mods, shift_chunk, scale_chunk, n_rows, cfg):
    D = cfg.D
    bm = _pick_block(math.gcd(cfg.S, cfg.L), 256, SUBLANE_BF16)
    return pl.pallas_call(
        functools.partial(_modulate_kernel, bm=bm, cfg=cfg),
        out_shape=jax.ShapeDtypeStruct((n_rows, D), BF16),
        grid=(n_rows // bm,),
        in_specs=[
            pl.BlockSpec((bm, D), lambda i: (i, 0)),
            pl.BlockSpec((1, D), lambda i: (0, 0)),
            pl.BlockSpec((MOD_ROWS, D), lambda i: (0, shift_chunk)),
            pl.BlockSpec((MOD_ROWS, D), lambda i: (0, scale_chunk)),
        ],
        out_specs=pl.BlockSpec((bm, D), lambda i: (i, 0)),
        compiler_params=_cparams(1),
        name="modulate",
    )(h, gain.reshape(1, D), mods, mods)


def _qkv_kernel(*refs, n_norm_blocks, rope, bn):
    if rope:
        x_ref, w_ref, g_ref, cos_ref, sa_ref, sb_ref, o_ref, wb_ref = refs
    else:
        x_ref, w_ref, g_ref, o_ref, wb_ref = refs
    j = pl.program_id(0)

    @pl.when(pl.program_id(1) == 0)
    def _():
        wb_ref[...] = w_ref[...].astype(BF16)

    acc = _dot(x_ref[...], wb_ref[...])

    @pl.when(j < n_norm_blocks)
    def _():
        outs = []
        for hh in range(bn // LANE):
            sl = slice(hh * LANE, (hh + 1) * LANE)
            xh = acc[:, sl]
            ms = jnp.mean(xh * xh, axis=-1, keepdims=True)
            y = xh * lax.rsqrt(ms + NORM_EPS) * g_ref[:, sl]
            if rope:
                y = (y * cos_ref[...] + pltpu.roll(y, 3 * LANE // 4, 1) * sa_ref[...]
                     + pltpu.roll(y, LANE // 4, 1) * sb_ref[...])
            outs.append(y.astype(BF16))
        o_ref[...] = jnp.concatenate(outs, axis=1)

    @pl.when(j >= n_norm_blocks)
    def _():
        o_ref[...] = acc.astype(BF16)


def _qkv_proj(xn, w_all, layer, gains, n_norm_cols, rope_tabs, cfg):
    M, D = xn.shape
    N = w_all.shape[2]
    bn = math.gcd(math.gcd(N, n_norm_cols), 512)
    bm = _pick_block(M, 1152, SUBLANE_BF16)
    assert bn % LANE == 0
    rope = rope_tabs is not None
    in_specs = [
        pl.BlockSpec((bm, D), lambda j, m: (m, 0)),
        pl.BlockSpec((None, D, bn), lambda j, m: (layer, 0, j)),
        pl.BlockSpec((1, bn), lambda j, m: (0, j)),
    ]
    args = [xn, w_all, gains]
    if rope:
        in_specs += [pl.BlockSpec((bm, LANE), lambda j, m: (m, 0))] * 3
        args += list(rope_tabs)
    return pl.pallas_call(
        functools.partial(_qkv_kernel, n_norm_blocks=n_norm_cols // bn, rope=rope, bn=bn),
        out_shape=jax.ShapeDtypeStruct((M, N), BF16),
        grid=(N // bn, M // bm),
        in_specs=in_specs,
        out_specs=pl.BlockSpec((bm, bn), lambda j, m: (m, j)),
        scratch_shapes=[pltpu.VMEM((D, bn), BF16)],
        compiler_params=_cparams(2),
        name="qkv_proj",
    )(*args)


def _mm_res_kernel(a_ref, w_ref, res_ref, gate_ref, o_ref, wb_ref, *acc, nk, bm, cfg):
    k = pl.program_id(1)
    m = pl.program_id(2)

    @pl.when(m == 0)
    def _():
        wb_ref[...] = w_ref[...].astype(BF16)

    def epilogue(total):
        o_ref[...] = res_ref[...] + _row_select(gate_ref, m * bm, bm, cfg) * total

    if nk == 1:
        epilogue(_dot(a_ref[...], wb_ref[...]))
    else:
        acc_ref = acc[0]
        rows = pl.ds(pl.multiple_of(m * bm, 8), bm)

        @pl.when(k == 0)
        def _():
            acc_ref[rows, :] = _dot(a_ref[...], wb_ref[...])

        @pl.when(jnp.logical_and(k > 0, k < nk - 1))
        def _():
            acc_ref[rows, :] += _dot(a_ref[...], wb_ref[...])

        @pl.when(k == nk - 1)
        def _():
            epilogue(acc_ref[rows, :] + _dot(a_ref[...], wb_ref[...]))


def _mm_res(a, w_all, layer, res, mods, gate_chunk, n_rows, cfg):
    _, K, N = w_all.shape
    bn = 512
    bk = K if K <= MAX_FULL_K else _pick_block(K, K_TILE, LANE)
    nk = K // bk
    bm = _pick_block(n_rows, 1152, SUBLANE_BF16)
    last = nk - 1
    scratch = [pltpu.VMEM((bk, bn), BF16)]
    if nk > 1:
        scratch.append(pltpu.VMEM((n_rows, bn), F32))
    return pl.pallas_call(
        functools.partial(_mm_res_kernel, nk=nk, bm=bm, cfg=cfg),
        out_shape=jax.ShapeDtypeStruct((n_rows, N), F32),
        grid=(N // bn, nk, n_rows // bm),
        in_specs=[
            pl.BlockSpec((bm, bk), lambda j, k, m: (m, k)),
            pl.BlockSpec((None, bk, bn), lambda j, k, m: (layer, k, j)),
            pl.BlockSpec((bm, bn), lambda j, k, m: (jnp.where(k == last, m, 0), j)),
            pl.BlockSpec((MOD_ROWS, bn), lambda j, k, m: (0, gate_chunk * (N // bn) + j)),
        ],
        out_specs=pl.BlockSpec((bm, bn), lambda j, k, m: (jnp.where(k == last, m, 0), j)),
        scratch_shapes=scratch,
        compiler_params=_cparams(3),
        name="matmul_residual",
    )(a, w_all, res, mods)


def _swiglu_kernel(*refs, gated, nbe):
    if gated:
        x_ref, wg_ref, wu_ref, gates_ref, o_ref, wgb_ref, wub_ref = refs
    else:
        x_ref, wg_ref, wu_ref, o_ref, wgb_ref, wub_ref = refs

    @pl.when(pl.program_id(1) == 0)
    def _():
        wgb_ref[...] = wg_ref[...].astype(BF16)
        wub_ref[...] = wu_ref[...].astype(BF16)

    x = x_ref[...]
    g = _dot(x, wgb_ref[...])
    u = _dot(x, wub_ref[...])
    hid = g * jax.nn.sigmoid(g) * u
    if gated:
        e = pl.program_id(0) // nbe
        gates = gates_ref[...]
        lane = lax.broadcasted_iota(jnp.int32, gates.shape, 1)
        hid = hid * jnp.sum(jnp.where(lane == e, gates, 0.0), axis=-1, keepdims=True)
    o_ref[...] = hid.astype(BF16)


def _swiglu_up(t, w_gate, w_up, layer, gates):
    M, D = t.shape
    _, E, _, F = w_gate.shape
    bn = 256
    nbe = F // bn
    bm = _pick_block(M, 1152, SUBLANE_BF16)
    gated = gates is not None
    w_spec = pl.BlockSpec((None, None, D, bn), lambda j, m: (layer, j // nbe, 0, j % nbe))
    in_specs = [pl.BlockSpec((bm, D), lambda j, m: (m, 0)), w_spec, w_spec]
    args = [t, w_gate, w_up]
    if gated:
        in_specs.append(pl.BlockSpec((bm, LANE), lambda j, m: (m, 0)))
        args.append(gates)
    return pl.pallas_call(
        functools.partial(_swiglu_kernel, gated=gated, nbe=nbe),
        out_shape=jax.ShapeDtypeStruct((M, E * F), BF16),
        grid=(E * nbe, M // bm),
        in_specs=in_specs,
        out_specs=pl.BlockSpec((bm, bn), lambda j, m: (m, j)),
        scratch_shapes=[pltpu.VMEM((D, bn), BF16), pltpu.VMEM((D, bn), BF16)],
        compiler_params=_cparams(2),
        name="swiglu_up",
    )(*args)


def _router_kernel(t_ref, w_ref, b_ref, o_ref):
    logits = _dot(t_ref[...], w_ref[...].astype(BF16)) + b_ref[...]
    lane = lax.broadcasted_iota(jnp.int32, logits.shape, 1).astype(F32)
    top1 = jnp.max(logits, axis=-1, keepdims=True)
    i1 = jnp.min(jnp.where(logits == top1, lane, float(LANE)), axis=-1, keepdims=True)
    rest = jnp.where(lane == i1, -jnp.inf, logits)
    top2 = jnp.max(rest, axis=-1, keepdims=True)
    i2 = jnp.min(jnp.where(rest == top2, lane, float(LANE)), axis=-1, keepdims=True)
    e2 = jnp.exp(top2 - top1)
    inv = 1.0 / (1.0 + e2)
    o_ref[...] = jnp.where(lane == i1, inv, 0.0) + jnp.where(lane == i2, e2 * inv, 0.0)


def _router(t, router_w, router_b, cfg):
    M, D = t.shape
    E = cfg.n_experts
    assert cfg.top_k == 2 and E <= LANE
    w = jnp.zeros((D, LANE), F32).at[:, :E].set(router_w)
    b = jnp.full((1, LANE), NEG_INF, F32).at[0, :E].set(router_b)
    bm = _pick_block(M, 1152, SUBLANE_BF16)
    return pl.pallas_call(
        _router_kernel,
        out_shape=jax.ShapeDtypeStruct((M, LANE), F32),
        grid=(M // bm,),
        in_specs=[
            pl.BlockSpec((bm, D), lambda m: (m, 0)),
            pl.BlockSpec((D, LANE), lambda m: (0, 0)),
            pl.BlockSpec((1, LANE), lambda m: (0, 0)),
        ],
        out_specs=pl.BlockSpec((bm, LANE), lambda m: (m, 0)),
        compiler_params=_cparams(1),
        name="moe_router",
    )(t, w, b)


def _na_group_geometry(g, n_groups, rows, kh):
    variant = jnp.where(g == 0, 0, jnp.where(g == n_groups - 1, 2, 1))
    return variant, jnp.clip(g * kh - kh // 2, 0, rows - 2 * kh)


def _na_kernel(q_ref, k_ref, v_ref, ck_ref, cv_ref, bias_ref, o_ref, *, rows, W, kh, unroll):
    ck = ck_ref[...]
    cv = cv_ref[...]
    n_groups = rows // kh
    gq, gk = kh * W, 2 * kh * W

    def one_group(g):
        variant, key_row = _na_group_geometry(g, n_groups, rows, kh)
        q = q_ref[pl.ds(pl.multiple_of(g * gq, gq), gq), :]
        kstart = pl.multiple_of(key_row * W, W)
        s = _dot_t(q, k_ref[pl.ds(kstart, gk), :]) + bias_ref[variant]
        sc = _dot_t(q, ck)
        m = jnp.maximum(jnp.max(s, axis=-1, keepdims=True), jnp.max(sc, axis=-1, keepdims=True))
        e = jnp.exp2(s - m)
        ec = jnp.exp2(sc - m)
        inv = 1.0 / (jnp.sum(e, axis=-1, keepdims=True) + jnp.sum(ec, axis=-1, keepdims=True))
        o = (_dot(e.astype(BF16), v_ref[pl.ds(kstart, gk), :]) + _dot(ec.astype(BF16), cv)) * inv
        return o.astype(BF16)

    def body(it, carry):
        outs = [one_group(it * unroll + u) for u in range(unroll)]
        start = pl.multiple_of(it * (unroll * gq), unroll * gq)
        o_ref[pl.ds(start, unroll * gq), :] = jnp.concatenate(outs, axis=0)
        return carry

    lax.fori_loop(0, n_groups // unroll, body, 0)


def _na_bias_table(rpb, cfg):
    W, kw = cfg.grid_w, cfg.na_win_w
    rows = cfg.S // W
    kh = min(cfg.na_win_h, rows)
    n_groups = rows // kh
    assert rows % kh == 0 and n_groups >= 2 and kh == cfg.na_win_h
    assert n_groups == 2 or (n_groups - 2) * kh - kh // 2 <= rows - 2 * kh
    H = rpb.shape[0]
    wq = jnp.arange(W)[:, None]
    kc = jnp.arange(W)[None, :]
    c0 = jnp.clip(wq - kw // 2, 0, W - kw)
    col_valid = (kc >= c0) & (kc < c0 + kw)
    cidx = jnp.clip(kc - wq + (cfg.na_win_w - 1), 0, 2 * cfg.na_win_w - 2)
    toe = jnp.where(col_valid[None, None], rpb[:, :, cidx], NEG_INF)
    tabs = []
    for g in (0, 1 if n_groups > 2 else 0, n_groups - 1):
        _, key_row = _na_group_geometry(g, n_groups, rows, kh)
        qi = g * kh + jnp.arange(kh)[:, None]
        ki = key_row + jnp.arange(2 * kh)[None, :]
        r0 = jnp.clip(qi - kh // 2, 0, rows - kh)
        row_valid = (ki >= r0) & (ki < r0 + kh)
        ridx = jnp.clip(ki - qi + (cfg.na_win_h - 1), 0, 2 * cfg.na_win_h - 2)
        tab = jnp.where(row_valid[None, :, :, None, None], toe[:, ridx], NEG_INF)
        tabs.append(jnp.transpose(tab, (0, 1, 3, 2, 4)).reshape(H, kh * W, 2 * kh * W))
    return jnp.stack(tabs, axis=1).astype(F32) * LOG2E


def _na_attention(qkv, rpb, n_out_rows, cfg):
    B, S, L, D, W, Dh = cfg.B, cfg.S, cfg.L, cfg.D, cfg.grid_w, cfg.head_dim
    H = D // Dh
    rows = S // W
    kh = min(cfg.na_win_h, rows)
    bias = _na_bias_table(rpb, cfg)
    cblk = cfg.MX // L
    return pl.pallas_call(
        functools.partial(_na_kernel, rows=rows, W=W, kh=kh, unroll=math.gcd(rows // kh, NA_GROUP_UNROLL)),
        out_shape=jax.ShapeDtypeStruct((n_out_rows, D), BF16),
        grid=(H, B),
        in_specs=[
            pl.BlockSpec((S, Dh), lambda h, b: (b, h)),
            pl.BlockSpec((S, Dh), lambda h, b: (b, H + h)),
            pl.BlockSpec((S, Dh), lambda h, b: (b, 2 * H + h)),
            pl.BlockSpec((L, Dh), lambda h, b: (cblk + b, H + h)),
            pl.BlockSpec((L, Dh), lambda h, b: (cblk + b, 2 * H + h)),
            pl.BlockSpec((None, 3, kh * W, 2 * kh * W), lambda h, b: (h, 0, 0, 0)),
        ],
        out_specs=pl.BlockSpec((S, Dh), lambda h, b: (b, h)),
        compiler_params=_cparams(2),
        name="neighborhood_attention",
    )(qkv, qkv, qkv, qkv, qkv, bias)


def _gqa_kernel(*refs, G, nsub, has_local, has_sink, has_alias, bq, win, S, Dh):
    refs = list(refs)
    sink_ref = refs.pop(0) if has_sink else None
    q_ref = refs.pop(0)
    k_ref, v_ref = (refs.pop(0), refs.pop(0)) if has_local else (None, None)
    ck_ref, cv_ref = refs.pop(0), refs.pop(0)
    if has_alias:
        refs.pop(0)
    o_ref = refs.pop(0)
    hk = pl.program_id(1)
    i = pl.program_id(2)
    ck = ck_ref[...]
    cv = cv_ref[...]
    if has_sink:
        sk = jnp.concatenate([jnp.full((bq, 1), sink_ref[hk, g] * LOG2E, F32) for g in range(G)], axis=0)
    row_in_block = lax.broadcasted_iota(jnp.int32, (G * bq, 1), 0) & (bq - 1)
    outs = []
    for sb in range(nsub):
        q = jnp.concatenate([q_ref[sb * bq:(sb + 1) * bq, g * Dh:(g + 1) * Dh] for g in range(G)], axis=0)
        sc = _dot_t(q, ck)
        m = jnp.max(sc, axis=-1, keepdims=True)
        if has_local:
            blk = i * nsub + sb
            start = pl.multiple_of(jnp.clip(blk * bq - bq, 0, S - 3 * bq), bq)
            kw = k_ref[pl.ds(start, 3 * bq), :]
            vw = v_ref[pl.ds(start, 3 * bq), :]
            kpos = start + lax.broadcasted_iota(jnp.int32, (1, 3 * bq), 1)
            qpos = blk * bq + row_in_block
            s = jnp.where(jnp.abs(kpos - qpos) <= win, _dot_t(q, kw), NEG_INF)
            m = jnp.maximum(m, jnp.max(s, axis=-1, keepdims=True))
        if has_sink:
            m = jnp.maximum(m, sk)
        ec = jnp.exp2(sc - m)
        l = jnp.sum(ec, axis=-1, keepdims=True)
        o = _dot(ec.astype(BF16), cv)
        if has_local:
            e = jnp.exp2(s - m)
            l = l + jnp.sum(e, axis=-1, keepdims=True)
            o = o + _dot(e.astype(BF16), vw)
        if has_sink:
            l = l + jnp.exp2(sk - m)
        o = o * (1.0 / l)
        outs.append(jnp.concatenate([o[g * bq:(g + 1) * bq] for g in range(G)], axis=1).astype(BF16))
    o_ref[...] = jnp.concatenate(outs, axis=0)


def _gqa_attention(qkv, sink, q_heads, kv_heads, local, o_prev, n_out_rows, cfg):
    B, S, L, D, Dh = cfg.B, cfg.S, cfg.L, cfg.D, cfg.head_dim
    G = q_heads // kv_heads
    kcol = q_heads
    vcol = q_heads + kv_heads
    cblk = cfg.MX // L
    has_sink = sink is not None
    bq = cfg.swa_block if local else L
    nsub = math.gcd(S // bq, GQA_SUBBLOCKS) if local else 1
    rows_step = nsub * bq
    assert bq & (bq - 1) == 0
    nq = S // rows_step if local else 1
    qrow = (lambda b, i: b * nq + i) if local else (lambda b, i: cblk + b)
    in_specs, args = [], []
    if has_sink:
        in_specs.append(pl.BlockSpec(memory_space=pltpu.SMEM))
        args.append(sink.reshape(kv_heads, G).astype(F32))
    in_specs.append(pl.BlockSpec((rows_step, G * Dh), lambda b, h, i: (qrow(b, i), h)))
    args.append(qkv)
    if local:
        in_specs += [pl.BlockSpec((S, Dh), lambda b, h, i: (b, kcol + h)),
                     pl.BlockSpec((S, Dh), lambda b, h, i: (b, vcol + h))]
        args += [qkv, qkv]
    in_specs += [pl.BlockSpec((L, Dh), lambda b, h, i: (cblk + b, kcol + h)),
                 pl.BlockSpec((L, Dh), lambda b, h, i: (cblk + b, vcol + h))]
    args += [qkv, qkv]
    aliases = {}
    if o_prev is not None:
        in_specs.append(pl.BlockSpec(memory_space=pl.ANY))
        aliases = {len(args): 0}
        args.append(o_prev)
    return pl.pallas_call(
        functools.partial(_gqa_kernel, G=G, nsub=nsub, has_local=local, has_sink=has_sink,
                          has_alias=o_prev is not None, bq=bq, win=cfg.swa_window, S=S, Dh=Dh),
        out_shape=jax.ShapeDtypeStruct((n_out_rows, D), BF16),
        grid=(B, kv_heads, nq),
        in_specs=in_specs,
        out_specs=pl.BlockSpec((rows_step, G * Dh), lambda b, h, i: (qrow(b, i), h)),
        input_output_aliases=aliases,
        compiler_params=_cparams(3),
        name="gqa_window_attention" if local else "gqa_context_attention",
    )(*args)


def _diff_kernel(*refs, has_x, has_alias, chunk, S, L, lambda_init, Dh):
    refs = list(refs)
    lq1, lk1, lq2, lk2, sub_ref, q_ref = refs[:6]
    refs = refs[6:]
    kx_ref, vx_ref = (refs.pop(0), refs.pop(0)) if has_x else (None, None)
    kc_ref, vc_ref = refs.pop(0), refs.pop(0)
    if has_alias:
        refs.pop(0)
    o_ref, e_ref = refs
    lam = (jnp.exp(jnp.sum(lq1[...] * lk1[...], axis=-1, keepdims=True))
           - jnp.exp(jnp.sum(lq2[...] * lk2[...], axis=-1, keepdims=True)) + lambda_init)
    chunks = []
    if has_x:
        chunks += [(kx_ref, vx_ref, c * chunk, chunk, c * chunk) for c in range(S // chunk)]
    off = S if has_x else 0
    chunks.append((kc_ref, vc_ref, 0, L, off))
    bq = q_ref.shape[0]
    coef = []
    for mi in range(2):
        sl = slice(mi * Dh, (mi + 1) * Dh)
        q = q_ref[:, sl]
        mx = jnp.full((bq, 1), -jnp.inf, F32)
        l = jnp.zeros((bq, 1), F32)
        maxes = []
        for k_ref, _, r0, n, e0 in chunks:
            s = _dot_t(q, k_ref[r0:r0 + n, sl])
            mn = jnp.maximum(mx, jnp.max(s, axis=-1, keepdims=True))
            e = jnp.exp2(s - mn)
            l = l * jnp.exp2(mx - mn) + jnp.sum(e, axis=-1, keepdims=True)
            e_ref[mi, :, e0:e0 + n] = e
            maxes.append(mn)
            mx = mn
        inv = 1.0 / l
        coef.append([jnp.exp2(mc - mx) * inv for mc in maxes])
    acc = jnp.zeros((bq, 2 * Dh), F32)
    for ci, (_, v_ref, r0, n, e0) in enumerate(chunks):
        p = e_ref[0, :, e0:e0 + n] * coef[0][ci] - e_ref[1, :, e0:e0 + n] * (lam * coef[1][ci])
        acc = acc + _dot(p.astype(BF16), v_ref[r0:r0 + n, :])
    ms = jnp.mean(acc * acc, axis=-1, keepdims=True)
    y = acc * lax.rsqrt(ms + NORM_EPS) * sub_ref[...] * (1.0 - lambda_init)
    o_ref[...] = y.astype(BF16)


def _diff_attention(qkv, lams, subln, lambda_init, main, o_prev, n_out_rows, cfg):
    B, S, L, D, Dh = cfg.B, cfg.S, cfg.L, cfg.D, cfg.head_dim
    H = D // (2 * Dh)
    HW = 2 * Dh
    cblk = cfg.MX // L
    bq = 256 if main else L
    nq = S // bq if main else 1
    nkeys = S + L if main else L
    qrow = (lambda b, i: b * nq + i) if main else (lambda b, i: cblk + b)
    vec = pl.BlockSpec((1, Dh), lambda b, h, i: (0, 0))
    in_specs = [vec, vec, vec, vec,
                pl.BlockSpec((1, HW), lambda b, h, i: (0, 0)),
                pl.BlockSpec((bq, HW), lambda b, h, i: (qrow(b, i), h))]
    args = [v.reshape(1, Dh).astype(F32) for v in lams] + [subln.reshape(1, HW).astype(F32), qkv]
    if main:
        in_specs += [pl.BlockSpec((S, HW), lambda b, h, i: (b, H + h)),
                     pl.BlockSpec((S, HW), lambda b, h, i: (b, 2 * H + h))]
        args += [qkv, qkv]
    in_specs += [pl.BlockSpec((L, HW), lambda b, h, i: (cblk + b, H + h)),
                 pl.BlockSpec((L, HW), lambda b, h, i: (cblk + b, 2 * H + h))]
    args += [qkv, qkv]
    aliases = {}
    if o_prev is not None:
        in_specs.append(pl.BlockSpec(memory_space=pl.ANY))
        aliases = {len(args): 0}
        args.append(o_prev)
    return pl.pallas_call(
        functools.partial(_diff_kernel, has_x=main, has_alias=o_prev is not None, chunk=512, S=S, L=L,
                          lambda_init=lambda_init, Dh=Dh),
        out_shape=jax.ShapeDtypeStruct((n_out_rows, D), BF16),
        grid=(B, H, nq),
        in_specs=in_specs,
        out_specs=pl.BlockSpec((bq, HW), lambda b, h, i: (qrow(b, i), h)),
        scratch_shapes=[pltpu.VMEM((2, bq, nkeys), F32)],
        input_output_aliases=aliases,
        compiler_params=_cparams(3),
        name="diff_attention" if main else "diff_context_attention",
    )(*args)


def _rope_tables(cfg):
    Dh = cfg.head_dim
    t = jnp.arange(cfg.S, dtype=jnp.int32)
    row = (t // cfg.grid_w).astype(F32)
    col = (t % cfg.grid_w).astype(F32)
    axis_dim = Dh // 2
    inv_freq = ROPE_BASE ** (-jnp.arange(0, axis_dim, 2, dtype=F32) / axis_dim)
    ang_r = row[:, None] * inv_freq[None, :]
    ang_c = col[:, None] * inv_freq[None, :]
    ang = jnp.concatenate([ang_r, ang_r, ang_c, ang_c], axis=-1)
    cos, sin = jnp.cos(ang), jnp.sin(ang)
    quarter = jnp.arange(Dh) // (Dh // 4)
    sin_a = jnp.where(quarter % 2 == 0, -sin, 0.0)
    sin_b = jnp.where(quarter % 2 == 1, sin, 0.0)
    n_ctx = cfg.B * cfg.L

    def full(tab, fill):
        return jnp.concatenate([jnp.tile(tab, (cfg.B, 1)), jnp.full((n_ctx, Dh), fill, F32)], axis=0)

    return full(cos, 1.0), full(sin_a, 0.0), full(sin_b, 0.0)


def _head_gains(q_gain, k_gain, n_q, n_k, n_v_cols, head_dim):
    q_scale = head_dim ** -0.5 * LOG2E
    return jnp.concatenate([jnp.tile(q_gain.astype(F32) * q_scale, n_q), jnp.tile(k_gain.astype(F32), n_k),
                            jnp.ones((n_v_cols,), F32)]).reshape(1, -1)


def _forward(cfg, x, c, ctx, c_ctx, ada_down, ada_up, ada_bias, norm_mix, norm_ffn,
             na_w_qkv, na_w_o, na_q_norm, na_k_norm, na_rpb,
             diff_w_qkv, diff_w_o, diff_q_norm, diff_k_norm,
             diff_lambda_q1, diff_lambda_k1, diff_lambda_q2, diff_lambda_k2, diff_subln,
             swa_w_qkv, swa_w_o, swa_q_norm, swa_k_norm, swa_sink,
             ffn_w_gate, ffn_w_up, ffn_w_down,
             moe_router, moe_router_bias, moe_w_gate, moe_w_up, moe_w_down):
    B, S, L, D, Dh = cfg.B, cfg.S, cfg.L, cfg.D, cfg.head_dim
    M, MX = cfg.M, cfg.MX
    heads = D // Dh
    h = jnp.concatenate([x.reshape(MX, D), ctx.reshape(B * L, D)], axis=0).astype(F32)
    cond = jnp.zeros((MOD_ROWS, D), F32).at[:B].set(c).at[B].set(c_ctx)
    mods_all = _ada_all(cond, ada_down, ada_up, ada_bias)
    rope_tabs = _rope_tables(cfg)

    for i in range(cfg.depth):
        ctx_out = i < cfg.depth - 1
        n_rows = M if ctx_out else MX
        mods = mods_all[i]
        xn = _modulate(h, norm_mix[i], mods, 0, 1, M, cfg)
        kind, j = i % 3, i // 3
        if kind == 0:
            gains = _head_gains(na_q_norm[j], na_k_norm[j], heads, heads, D, Dh)
            qkv = _qkv_proj(xn, na_w_qkv, j, gains, 2 * D, None, cfg)
            o = _na_attention(qkv, na_rpb[j], n_rows, cfg)
            if ctx_out:
                o = _gqa_attention(qkv, None, heads, heads, False, o, n_rows, cfg)
            w_o = na_w_o
        elif kind == 1:
            gains = _head_gains(diff_q_norm[j], diff_k_norm[j], heads, heads, D, Dh)
            qkv = _qkv_proj(xn, diff_w_qkv, j, gains, 2 * D, rope_tabs, cfg)
            lams = (diff_lambda_q1[j], diff_lambda_k1[j], diff_lambda_q2[j], diff_lambda_k2[j])
            lambda_init = 0.8 - 0.6 * math.exp(-0.3 * i)
            o = _diff_attention(qkv, lams, diff_subln[j], lambda_init, True, None, n_rows, cfg)
            if ctx_out:
                o = _diff_attention(qkv, lams, diff_subln[j], lambda_init, False, o, n_rows, cfg)
            w_o = diff_w_o
        else:
            kvh = heads // cfg.swa_group
            gains = _head_gains(swa_q_norm[j], swa_k_norm[j], heads, kvh, kvh * Dh, Dh)
            qkv = _qkv_proj(xn, swa_w_qkv, j, gains, (heads + kvh) * Dh, rope_tabs, cfg)
            o = _gqa_attention(qkv, swa_sink[j], heads, kvh, True, None, n_rows, cfg)
            if ctx_out:
                o = _gqa_attention(qkv, swa_sink[j], heads, kvh, False, o, n_rows, cfg)
            w_o = swa_w_o
        h = _mm_res(o, w_o, j, h, mods, 2, n_rows, cfg)
        t = _modulate(h, norm_ffn[i], mods, 3, 4, n_rows, cfg)
        fj = i // 2
        if i % 2 == 0:
            n_ffn, _, d_ff = ffn_w_gate.shape
            hid = _swiglu_up(t, ffn_w_gate.reshape(n_ffn, 1, D, d_ff), ffn_w_up.reshape(n_ffn, 1, D, d_ff), fj, None)
            w_down = ffn_w_down
        else:
            gates = _router(t, moe_router[fj], moe_router_bias[fj], cfg)
            hid = _swiglu_up(t, moe_w_gate, moe_w_up, fj, gates)
            w_down = moe_w_down.reshape(moe_w_down.shape[0], -1, D)
        h = _mm_res(hid, w_down, fj, h, mods, 5, n_rows, cfg)
    return h.reshape(B, S, D)


_CFG = Cfg(B=2, S=4096, L=256, D=4096, depth=4, grid_w=64, head_dim=128, na_win_h=8, na_win_w=16,
           swa_window=128, swa_block=128, swa_group=4, n_experts=8, top_k=2)


def kernel(x, c, ctx, c_ctx, ada_down, ada_up, ada_bias, norm_mix, norm_ffn, na_w_qkv, na_w_o, na_q_norm, na_k_norm, na_rpb, diff_w_qkv, diff_w_o, diff_q_norm, diff_k_norm, diff_lambda_q1, diff_lambda_k1, diff_lambda_q2, diff_lambda_k2, diff_subln, swa_w_qkv, swa_w_o, swa_q_norm, swa_k_norm, swa_sink, ffn_w_gate, ffn_w_up, ffn_w_down, moe_router, moe_router_bias, moe_w_gate, moe_w_up, moe_w_down):
    return _forward(_CFG, x, c, ctx, c_ctx, ada_down, ada_up, ada_bias, norm_mix, norm_ffn, na_w_qkv, na_w_o, na_q_norm, na_k_norm, na_rpb, diff_w_qkv, diff_w_o, diff_q_norm, diff_k_norm, diff_lambda_q1, diff_lambda_k1, diff_lambda_q2, diff_lambda_k2, diff_subln, swa_w_qkv, swa_w_o, swa_q_norm, swa_k_norm, swa_sink, ffn_w_gate, ffn_w_up, ffn_w_down, moe_router, moe_router_bias, moe_w_gate, moe_w_up, moe_w_down)
```

```python
import functools
import math
from typing import NamedTuple

import jax
import jax.numpy as jnp
from jax import lax
from jax.experimental import pallas as pl
from jax.experimental.pallas import tpu as pltpu

F32 = jnp.float32
BF16 = jnp.bfloat16

LANE = 128
SUBLANE_BF16 = 16
MOD_ROWS = 8
VMEM_LIMIT = 58 * 1024 * 1024
MAX_FULL_K = 4096
K_TILE = 2048

NA_GROUP_UNROLL = 2
GQA_QUERY_ROWS = 512
DIFF_KEY_CHUNK = 1024

NORM_EPS = 1e-6
NEG_INF = -1e30
ROPE_BASE = 10000.0
LOG2E = 1.4426950408889634


class Cfg(NamedTuple):
    B: int
    S: int
    L: int
    D: int
    depth: int
    grid_w: int
    head_dim: int
    na_win_h: int
    na_win_w: int
    swa_window: int
    swa_block: int
    swa_group: int
    n_experts: int
    top_k: int

    @property
    def M(self):
        return self.B * (self.S + self.L)

    @property
    def MX(self):
        return self.B * self.S


def _cparams(n_axes):
    return pltpu.CompilerParams(dimension_semantics=("arbitrary",) * n_axes,
                                vmem_limit_bytes=VMEM_LIMIT)


def _pick_block(n, cap, mult):
    best = None
    for d in range(mult, min(n, cap) + 1, mult):
        if n % d == 0:
            best = d
    assert best is not None, (n, cap, mult)
    return best


def _dot(a, b):
    return jnp.dot(a, b, preferred_element_type=F32)


def _dot_t(a, b):
    return lax.dot_general(a, b, (((1,), (1,)), ((), ())), preferred_element_type=F32)


def _row_select(vecs_ref, row0, bm, cfg):
    rows = row0 + lax.broadcasted_iota(jnp.int32, (bm, 1), 0)
    g = vecs_ref[cfg.B:cfg.B + 1, :]
    for b in reversed(range(cfg.B)):
        g = jnp.where(rows < (b + 1) * cfg.S, vecs_ref[b:b + 1, :], g)
    return g


def _ada_kernel(c_ref, down_ref, up_ref, b_ref, o_ref):
    c = c_ref[...]
    sc = (c * jax.nn.sigmoid(c)).astype(BF16)
    t = _dot(sc, down_ref[...].astype(BF16))
    o_ref[...] = _dot(t.astype(BF16), up_ref[...].astype(BF16)) + b_ref[...]


def _ada_all(cond, ada_down, ada_up, ada_bias):
    depth, D, R = ada_down.shape
    N = ada_up.shape[2]
    bn = _pick_block(N, 3072, LANE)
    return pl.pallas_call(
        _ada_kernel,
        out_shape=jax.ShapeDtypeStruct((depth, MOD_ROWS, N), F32),
        grid=(depth, N // bn),
        in_specs=[
            pl.BlockSpec((MOD_ROWS, D), lambda l, j: (0, 0)),
            pl.BlockSpec((None, D, R), lambda l, j: (l, 0, 0)),
            pl.BlockSpec((None, R, bn), lambda l, j: (l, 0, j)),
            pl.BlockSpec((None, 1, bn), lambda l, j: (l, 0, j)),
        ],
        out_specs=pl.BlockSpec((None, MOD_ROWS, bn), lambda l, j: (l, 0, j)),
        compiler_params=_cparams(2),
        name="ada_modulation",
    )(cond, ada_down, ada_up, ada_bias.reshape(depth, 1, N))


def _modulate_kernel(x_ref, g_ref, sh_ref, sc_ref, o_ref, *, bm, cfg):
    row0 = pl.program_id(0) * bm
    t = jnp.where(row0 < cfg.MX, row0 // cfg.S, cfg.B)
    x = x_ref[...]
    ms = jnp.mean(x * x, axis=-1, keepdims=True)
    y = x * lax.rsqrt(ms + NORM_EPS) * g_ref[...]
    o_ref[...] = (y * (1 + sc_ref[pl.ds(t, 1), :]) + sh_ref[pl.ds(t, 1), :]).astype(BF16)


def _modulate(h, gain, mods, shift_chunk, scale_chunk, n_rows, cfg):
    D = cfg.D
    bm = _pick_block(math.gcd(cfg.S, cfg.L), 256, SUBLANE_BF16)
    return pl.pallas_call(
        functools.partial(_modulate_kernel, bm=bm, cfg=cfg),
        out_shape=jax.ShapeDtypeStruct((n_rows, D), BF16),
        grid=(n_rows // bm,),
        in_specs=[
            pl.BlockSpec((bm, D), lambda i: (i, 0)),
            pl.BlockSpec((1, D), lambda i: (0, 0)),
            pl.BlockSpec((MOD_ROWS, D), lambda i: (0, shift_chunk)),
            pl.BlockSpec((MOD_ROWS, D), lambda i: (0, scale_chunk)),
        ],
        out_specs=pl.BlockSpec((bm, D), lambda i: (i, 0)),
        compiler_params=_cparams(1),
        name="modulate",
    )(h, gain.reshape(1, D), mods, mods)


def _qkv_kernel(*refs, n_norm_blocks, rope, bn):
    if rope:
        x_ref, w_ref, g_ref, cos_ref, sa_ref, sb_ref, o_ref, wb_ref = refs
    else:
        x_ref, w_ref, g_ref, o_ref, wb_ref = refs
    j = pl.program_id(0)

    @pl.when(pl.program_id(1) == 0)
    def _():
        wb_ref[...] = w_ref[...].astype(BF16)

    acc = _dot(x_ref[...], wb_ref[...])

    @pl.when(j < n_norm_blocks)
    def _():
        outs = []
        for hh in range(bn // LANE):
            sl = slice(hh * LANE, (hh + 1) * LANE)
            xh = acc[:, sl]
            ms = jnp.mean(xh * xh, axis=-1, keepdims=True)
            y = xh * lax.rsqrt(ms + NORM_EPS) * g_ref[:, sl]
            if rope:
                y = (y * cos_ref[...] + pltpu.roll(y, 3 * LANE // 4, 1) * sa_ref[...]
                     + pltpu.roll(y, LANE // 4, 1) * sb_ref[...])
            outs.append(y.astype(BF16))
        o_ref[...] = jnp.concatenate(outs, axis=1)

    @pl.when(j >= n_norm_blocks)
    def _():
        o_ref[...] = acc.astype(BF16)


def _qkv_proj(xn, w_all, layer, gains, n_norm_cols, rope_tabs, cfg):
    M, D = xn.shape
    N = w_all.shape[2]
    bn = math.gcd(math.gcd(N, n_norm_cols), 512)
    bm = _pick_block(M, 1152, SUBLANE_BF16)
    assert bn % LANE == 0
    rope = rope_tabs is not None
    in_specs = [
        pl.BlockSpec((bm, D), lambda j, m: (m, 0)),
        pl.BlockSpec((None, D, bn), lambda j, m: (layer, 0, j)),
        pl.BlockSpec((1, bn), lambda j, m: (0, j)),
    ]
    args = [xn, w_all, gains]
    if rope:
        in_specs += [pl.BlockSpec((bm, LANE), lambda j, m: (m, 0))] * 3
        args += list(rope_tabs)
    return pl.pallas_call(
        functools.partial(_qkv_kernel, n_norm_blocks=n_norm_cols // bn, rope=rope, bn=bn),
        out_shape=jax.ShapeDtypeStruct((M, N), BF16),
        grid=(N // bn, M // bm),
        in_specs=in_specs,
        out_specs=pl.BlockSpec((bm, bn), lambda j, m: (m, j)),
        scratch_shapes=[pltpu.VMEM((D, bn), BF16)],
        compiler_params=_cparams(2),
        name="qkv_proj",
    )(*args)


def _mm_res_kernel(a_ref, w_ref, res_ref, gate_ref, o_ref, wb_ref, *acc, nk, bm, cfg):
    k = pl.program_id(1)
    m = pl.program_id(2)

    @pl.when(m == 0)
    def _():
        wb_ref[...] = w_ref[...].astype(BF16)

    def epilogue(total):
        o_ref[...] = res_ref[...] + _row_select(gate_ref, m * bm, bm, cfg) * total

    if nk == 1:
        epilogue(_dot(a_ref[...], wb_ref[...]))
    else:
        acc_ref = acc[0]
        rows = pl.ds(pl.multiple_of(m * bm, 8), bm)

        @pl.when(k == 0)
        def _():
            acc_ref[rows, :] = _dot(a_ref[...], wb_ref[...])

        @pl.when(jnp.logical_and(k > 0, k < nk - 1))
        def _():
            acc_ref[rows, :] += _dot(a_ref[...], wb_ref[...])

        @pl.when(k == nk - 1)
        def _():
            epilogue(acc_ref[rows, :] + _dot(a_ref[...], wb_ref[...]))


def _mm_res(a, w_all, layer, res, mods, gate_chunk, n_rows, cfg):
    _, K, N = w_all.shape
    bn = 512
    bk = K if K <= MAX_FULL_K else _pick_block(K, K_TILE, LANE)
    nk = K // bk
    bm = _pick_block(n_rows, 1152, SUBLANE_BF16)
    last = nk - 1
    scratch = [pltpu.VMEM((bk, bn), BF16)]
    if nk > 1:
        scratch.append(pltpu.VMEM((n_rows, bn), F32))
    return pl.pallas_call(
        functools.partial(_mm_res_kernel, nk=nk, bm=bm, cfg=cfg),
        out_shape=jax.ShapeDtypeStruct((n_rows, N), F32),
        grid=(N // bn, nk, n_rows // bm),
        in_specs=[
            pl.BlockSpec((bm, bk), lambda j, k, m: (m, k)),
            pl.BlockSpec((None, bk, bn), lambda j, k, m: (layer, k, j)),
            pl.BlockSpec((bm, bn), lambda j, k, m: (jnp.where(k == last, m, 0), j)),
            pl.BlockSpec((MOD_ROWS, bn), lambda j, k, m: (0, gate_chunk * (N // bn) + j)),
        ],
        out_specs=pl.BlockSpec((bm, bn), lambda j, k, m: (jnp.where(k == last, m, 0), j)),
        scratch_shapes=scratch,
        compiler_params=_cparams(3),
        name="matmul_residual",
    )(a, w_all, res, mods)


def _swiglu_kernel(*refs, gated, nbe):
    if gated:
        x_ref, wg_ref, wu_ref, gates_ref, o_ref, wgb_ref, wub_ref = refs
    else:
        x_ref, wg_ref, wu_ref, o_ref, wgb_ref, wub_ref = refs

    @pl.when(pl.program_id(1) == 0)
    def _():
        wgb_ref[...] = wg_ref[...].astype(BF16)
        wub_ref[...] = wu_ref[...].astype(BF16)

    x = x_ref[...]
    g = _dot(x, wgb_ref[...])
    u = _dot(x, wub_ref[...])
    hid = g * jax.nn.sigmoid(g) * u
    if gated:
        e = pl.program_id(0) // nbe
        gates = gates_ref[...]
        lane = lax.broadcasted_iota(jnp.int32, gates.shape, 1)
        hid = hid * jnp.sum(jnp.where(lane == e, gates, 0.0), axis=-1, keepdims=True)
    o_ref[...] = hid.astype(BF16)


def _swiglu_up(t, w_gate, w_up, layer, gates):
    M, D = t.shape
    _, E, _, F = w_gate.shape
    bn = 256
    nbe = F // bn
    bm = _pick_block(M, 1152, SUBLANE_BF16)
    gated = gates is not None
    w_spec = pl.BlockSpec((None, None, D, bn), lambda j, m: (layer, j // nbe, 0, j % nbe))
    in_specs = [pl.BlockSpec((bm, D), lambda j, m: (m, 0)), w_spec, w_spec]
    args = [t, w_gate, w_up]
    if gated:
        in_specs.append(pl.BlockSpec((bm, LANE), lambda j, m: (m, 0)))
        args.append(gates)
    return pl.pallas_call(
        functools.partial(_swiglu_kernel, gated=gated, nbe=nbe),
        out_shape=jax.ShapeDtypeStruct((M, E * F), BF16),
        grid=(E * nbe, M // bm),
        in_specs=in_specs,
        out_specs=pl.BlockSpec((bm, bn), lambda j, m: (m, j)),
        scratch_shapes=[pltpu.VMEM((D, bn), BF16), pltpu.VMEM((D, bn), BF16)],
        compiler_params=_cparams(2),
        name="swiglu_up",
    )(*args)


def _router_kernel(t_ref, w_ref, b_ref, o_ref):
    logits = _dot(t_ref[...], w_ref[...].astype(BF16)) + b_ref[...]
    lane = lax.broadcasted_iota(jnp.int32, logits.shape, 1).astype(F32)
    top1 = jnp.max(logits, axis=-1, keepdims=True)
    i1 = jnp.min(jnp.where(logits == top1, lane, float(LANE)), axis=-1, keepdims=True)
    rest = jnp.where(lane == i1, -jnp.inf, logits)
    top2 = jnp.max(rest, axis=-1, keepdims=True)
    i2 = jnp.min(jnp.where(rest == top2, lane, float(LANE)), axis=-1, keepdims=True)
    e2 = jnp.exp(top2 - top1)
    inv = 1.0 / (1.0 + e2)
    o_ref[...] = jnp.where(lane == i1, inv, 0.0) + jnp.where(lane == i2, e2 * inv, 0.0)


def _router(t, router_w, router_b, cfg):
    M, D = t.shape
    E = cfg.n_experts
    assert cfg.top_k == 2 and E <= LANE
    w = jnp.zeros((D, LANE), F32).at[:, :E].set(router_w)
    b = jnp.full((1, LANE), NEG_INF, F32).at[0, :E].set(router_b)
    bm = _pick_block(M, 1152, SUBLANE_BF16)
    return pl.pallas_call(
        _router_kernel,
        out_shape=jax.ShapeDtypeStruct((M, LANE), F32),
        grid=(M // bm,),
        in_specs=[
            pl.BlockSpec((bm, D), lambda m: (m, 0)),
            pl.BlockSpec((D, LANE), lambda m: (0, 0)),
            pl.BlockSpec((1, LANE), lambda m: (0, 0)),
        ],
        out_specs=pl.BlockSpec((bm, LANE), lambda m: (m, 0)),
        compiler_params=_cparams(1),
        name="moe_router",
    )(t, w, b)


def _na_variant_groups(n_groups):
    return (0, 1 if n_groups > 2 else 0, n_groups - 1)


def _na_build_bias(tp_ref, bias_ref, *, rows, W, kh):
    n_groups = rows // kh
    lane = lax.broadcasted_iota(jnp.int32, (W, 2 * W), 1)
    neg = jnp.full((W, 2 * W), NEG_INF, F32)
    for variant, g in enumerate(_na_variant_groups(n_groups)):
        key_row = min(max(g * kh - kh // 2, 0), rows - 2 * kh)
        for ql in range(kh):
            qi = g * kh + ql
            r0 = min(max(qi - kh // 2, 0), rows - kh)
            for j in range(kh):
                ki = key_row + 2 * j
                first_ok = r0 <= ki < r0 + kh
                second_ok = r0 <= ki + 1 < r0 + kh
                if not (first_ok or second_ok):
                    blk = neg
                else:
                    blk = tp_ref[ki - qi + kh]
                    if not second_ok:
                        blk = jnp.where(lane < W, blk, NEG_INF)
                    elif not first_ok:
                        blk = jnp.where(lane >= W, blk, NEG_INF)
                bias_ref[variant, ql * W:(ql + 1) * W, j * 2 * W:(j + 1) * 2 * W] = blk


def _na_kernel(q_ref, k_ref, v_ref, ck_ref, cv_ref, tp_ref, o_ref, bias_ref, *, rows, W, kh, unroll):
    @pl.when(pl.program_id(1) == 0)
    def _():
        _na_build_bias(tp_ref, bias_ref, rows=rows, W=W, kh=kh)

    ck = ck_ref[...]
    cv = cv_ref[...]
    n_groups = rows // kh
    gq, gk = kh * W, 2 * kh * W

    def one_group(g):
        variant = jnp.where(g == 0, 0, jnp.where(g == n_groups - 1, 2, 1))
        key_row = jnp.clip(g * kh - kh // 2, 0, rows - 2 * kh)
        q = q_ref[pl.ds(pl.multiple_of(g * gq, gq), gq), :]
        kstart = pl.multiple_of(key_row * W, W)
        s = _dot_t(q, k_ref[pl.ds(kstart, gk), :]) + bias_ref[variant]
        sc = _dot_t(q, ck)
        m = jnp.maximum(jnp.max(s, axis=-1, keepdims=True), jnp.max(sc, axis=-1, keepdims=True))
        e = jnp.exp2(s - m)
        ec = jnp.exp2(sc - m)
        inv = 1.0 / (jnp.sum(e, axis=-1, keepdims=True) + jnp.sum(ec, axis=-1, keepdims=True))
        o = (_dot(e.astype(BF16), v_ref[pl.ds(kstart, gk), :]) + _dot(ec.astype(BF16), cv)) * inv
        return o.astype(BF16)

    def body(it, carry):
        outs = [one_group(it * unroll + u) for u in range(unroll)]
        start = pl.multiple_of(it * (unroll * gq), unroll * gq)
        o_ref[pl.ds(start, unroll * gq), :] = jnp.concatenate(outs, axis=0)
        return carry

    lax.fori_loop(0, n_groups // unroll, body, 0)


def _na_bias_blocks(rpb, cfg):
    W, kw, wh = cfg.grid_w, cfg.na_win_w, cfg.na_win_h
    H = rpb.shape[0]
    wq = jnp.arange(W)[:, None]
    kc = jnp.arange(W)[None, :]
    c0 = jnp.clip(wq - kw // 2, 0, W - kw)
    col_valid = (kc >= c0) & (kc < c0 + kw)
    cidx = jnp.clip(kc - wq + (kw - 1), 0, 2 * kw - 2)
    toe = jnp.where(col_valid[None, None], rpb[:, :, cidx], NEG_INF).astype(F32)
    pad = jnp.full((H, 1, W, W), NEG_INF, F32)
    toe = jnp.concatenate([pad, toe, pad], axis=1)
    return jnp.concatenate([toe[:, :-1], toe[:, 1:]], axis=-1) * LOG2E


def _na_attention(qkv, rpb, n_out_rows, cfg):
    B, S, L, D, W, Dh = cfg.B, cfg.S, cfg.L, cfg.D, cfg.grid_w, cfg.head_dim
    H = D // Dh
    rows = S // W
    kh = min(cfg.na_win_h, rows)
    n_groups = rows // kh
    assert rows % kh == 0 and n_groups >= 2 and kh == cfg.na_win_h and 2 * W == LANE
    blocks = _na_bias_blocks(rpb, cfg)
    cblk = cfg.MX // L
    return pl.pallas_call(
        functools.partial(_na_kernel, rows=rows, W=W, kh=kh, unroll=math.gcd(rows // kh, NA_GROUP_UNROLL)),
        out_shape=jax.ShapeDtypeStruct((n_out_rows, D), BF16),
        grid=(H, B),
        in_specs=[
            pl.BlockSpec((S, Dh), lambda h, b: (b, h)),
            pl.BlockSpec((S, Dh), lambda h, b: (b, H + h)),
            pl.BlockSpec((S, Dh), lambda h, b: (b, 2 * H + h)),
            pl.BlockSpec((L, Dh), lambda h, b: (cblk + b, H + h)),
            pl.BlockSpec((L, Dh), lambda h, b: (cblk + b, 2 * H + h)),
            pl.BlockSpec((None, 2 * kh, W, 2 * W), lambda h, b: (h, 0, 0, 0)),
        ],
        out_specs=pl.BlockSpec((S, Dh), lambda h, b: (b, h)),
        scratch_shapes=[pltpu.VMEM((3, kh * W, 2 * kh * W), F32)],
        compiler_params=_cparams(2),
        name="neighborhood_attention",
    )(qkv, qkv, qkv, qkv, qkv, blocks)


def _gqa_kernel(*refs, G, has_local, has_sink, has_alias, win, S, Dh):
    refs = list(refs)
    sink_ref = refs.pop(0) if has_sink else None
    q_ref = refs.pop(0)
    k_ref, v_ref, mask_ref = (refs.pop(0), refs.pop(0), refs.pop(0)) if has_local else (None, None, None)
    ck_ref, cv_ref = refs.pop(0), refs.pop(0)
    if has_alias:
        refs.pop(0)
    o_ref = refs.pop(0)
    hk = pl.program_id(1)
    rows = q_ref.shape[0]
    ck = ck_ref[...]
    cv = cv_ref[...]
    if has_local:
        nkw = rows + 2 * win
        start = pl.multiple_of(jnp.clip(pl.program_id(2) * rows - win, 0, S - nkw), win)
        kw = k_ref[pl.ds(start, nkw), :]
        vw = v_ref[pl.ds(start, nkw), :]
    outs = []
    for g in range(G):
        q = q_ref[:, g * Dh:(g + 1) * Dh]
        sc = _dot_t(q, ck)
        m = jnp.max(sc, axis=-1, keepdims=True)
        if has_local:
            s = _dot_t(q, kw) + mask_ref[...]
            m = jnp.maximum(m, jnp.max(s, axis=-1, keepdims=True))
        if has_sink:
            sk = sink_ref[hk, g] * LOG2E
            m = jnp.maximum(m, sk)
        ec = jnp.exp2(sc - m)
        l = jnp.sum(ec, axis=-1, keepdims=True)
        o = _dot(ec.astype(BF16), cv)
        if has_local:
            e = jnp.exp2(s - m)
            l = l + jnp.sum(e, axis=-1, keepdims=True)
            o = o + _dot(e.astype(BF16), vw)
        if has_sink:
            l = l + jnp.exp2(sk - m)
        outs.append((o * (1.0 / l)).astype(BF16))
    o_ref[...] = jnp.concatenate(outs, axis=1)


def _band_mask(rows, win, S):
    nkw = rows + 2 * win
    r = jnp.arange(rows)[:, None]
    c = jnp.arange(nkw)[None, :]
    offsets = (0, -win, -2 * win)
    return jnp.stack([jnp.where(jnp.abs(off + c - r) <= win, 0.0, NEG_INF) for off in offsets]).astype(F32)


def _gqa_attention(qkv, sink, q_heads, kv_heads, local, o_prev, n_out_rows, cfg):
    B, S, L, D, Dh = cfg.B, cfg.S, cfg.L, cfg.D, cfg.head_dim
    G = q_heads // kv_heads
    kcol = q_heads
    vcol = q_heads + kv_heads
    cblk = cfg.MX // L
    has_sink = sink is not None
    win = cfg.swa_window
    rows_step = GQA_QUERY_ROWS if local else L
    nq = S // rows_step if local else 1
    qrow = (lambda b, i: b * nq + i) if local else (lambda b, i: cblk + b)
    in_specs, args = [], []
    if has_sink:
        in_specs.append(pl.BlockSpec(memory_space=pltpu.SMEM))
        args.append(sink.reshape(kv_heads, G).astype(F32))
    in_specs.append(pl.BlockSpec((rows_step, G * Dh), lambda b, h, i: (qrow(b, i), h)))
    args.append(qkv)
    if local:
        assert cfg.swa_block == win and S % rows_step == 0 and nq >= 2 and win % SUBLANE_BF16 == 0
        nkw = rows_step + 2 * win
        variant = lambda i: jnp.where(i == 0, 0, jnp.where(i == nq - 1, 2, 1))
        in_specs += [pl.BlockSpec((S, Dh), lambda b, h, i: (b, kcol + h)),
                     pl.BlockSpec((S, Dh), lambda b, h, i: (b, vcol + h)),
                     pl.BlockSpec((None, rows_step, nkw), lambda b, h, i: (variant(i), 0, 0))]
        args += [qkv, qkv, _band_mask(rows_step, win, S)]
    in_specs += [pl.BlockSpec((L, Dh), lambda b, h, i: (cblk + b, kcol + h)),
                 pl.BlockSpec((L, Dh), lambda b, h, i: (cblk + b, vcol + h))]
    args += [qkv, qkv]
    aliases = {}
    if o_prev is not None:
        in_specs.append(pl.BlockSpec(memory_space=pl.ANY))
        aliases = {len(args): 0}
        args.append(o_prev)
    return pl.pallas_call(
        functools.partial(_gqa_kernel, G=G, has_local=local, has_sink=has_sink,
                          has_alias=o_prev is not None, win=win, S=S, Dh=Dh),
        out_shape=jax.ShapeDtypeStruct((n_out_rows, D), BF16),
        grid=(B, kv_heads, nq),
        in_specs=in_specs,
        out_specs=pl.BlockSpec((rows_step, G * Dh), lambda b, h, i: (qrow(b, i), h)),
        input_output_aliases=aliases,
        compiler_params=_cparams(3),
        name="gqa_window_attention" if local else "gqa_context_attention",
    )(*args)


def _diff_kernel(*refs, has_x, has_alias, chunk, S, L, lambda_init, Dh):
    refs = list(refs)
    lq1, lk1, lq2, lk2, sub_ref, q_ref = refs[:6]
    refs = refs[6:]
    kx_ref, vx_ref = (refs.pop(0), refs.pop(0)) if has_x else (None, None)
    kc_ref, vc_ref = refs.pop(0), refs.pop(0)
    if has_alias:
        refs.pop(0)
    o_ref, e_ref = refs
    lam = (jnp.exp(jnp.sum(lq1[...] * lk1[...], axis=-1, keepdims=True))
           - jnp.exp(jnp.sum(lq2[...] * lk2[...], axis=-1, keepdims=True)) + lambda_init)
    chunks = []
    if has_x:
        chunks += [(kx_ref, vx_ref, c * chunk, chunk, c * chunk) for c in range(S // chunk)]
    off = S if has_x else 0
    chunks.append((kc_ref, vc_ref, 0, L, off))
    bq = q_ref.shape[0]
    coef = []
    for mi in range(2):
        sl = slice(mi * Dh, (mi + 1) * Dh)
        q = q_ref[:, sl]
        mx = jnp.full((bq, 1), -jnp.inf, F32)
        l = jnp.zeros((bq, 1), F32)
        maxes = []
        for k_ref, _, r0, n, e0 in chunks:
            s = _dot_t(q, k_ref[r0:r0 + n, sl])
            mn = jnp.maximum(mx, jnp.max(s, axis=-1, keepdims=True))
            e = jnp.exp2(s - mn)
            l = l * jnp.exp2(mx - mn) + jnp.sum(e, axis=-1, keepdims=True)
            e_ref[mi, :, e0:e0 + n] = e
            maxes.append(mn)
            mx = mn
        inv = 1.0 / l
        coef.append([jnp.exp2(mc - mx) * inv for mc in maxes])
    acc = jnp.zeros((bq, 2 * Dh), F32)
    for ci, (_, v_ref, r0, n, e0) in enumerate(chunks):
        p = e_ref[0, :, e0:e0 + n] * coef[0][ci] - e_ref[1, :, e0:e0 + n] * (lam * coef[1][ci])
        acc = acc + _dot(p.astype(BF16), v_ref[r0:r0 + n, :])
    ms = jnp.mean(acc * acc, axis=-1, keepdims=True)
    y = acc * lax.rsqrt(ms + NORM_EPS) * sub_ref[...] * (1.0 - lambda_init)
    o_ref[...] = y.astype(BF16)


def _diff_attention(qkv, lams, subln, lambda_init, main, o_prev, n_out_rows, cfg):
    B, S, L, D, Dh = cfg.B, cfg.S, cfg.L, cfg.D, cfg.head_dim
    H = D // (2 * Dh)
    HW = 2 * Dh
    cblk = cfg.MX // L
    bq = 256 if main else L
    nq = S // bq if main else 1
    nkeys = S + L if main else L
    qrow = (lambda b, i: b * nq + i) if main else (lambda b, i: cblk + b)
    vec = pl.BlockSpec((1, Dh), lambda b, h, i: (0, 0))
    in_specs = [vec, vec, vec, vec,
                pl.BlockSpec((1, HW), lambda b, h, i: (0, 0)),
                pl.BlockSpec((bq, HW), lambda b, h, i: (qrow(b, i), h))]
    args = [v.reshape(1, Dh).astype(F32) for v in lams] + [subln.reshape(1, HW).astype(F32), qkv]
    if main:
        in_specs += [pl.BlockSpec((S, HW), lambda b, h, i: (b, H + h)),
                     pl.BlockSpec((S, HW), lambda b, h, i: (b, 2 * H + h))]
        args += [qkv, qkv]
    in_specs += [pl.BlockSpec((L, HW), lambda b, h, i: (cblk + b, H + h)),
                 pl.BlockSpec((L, HW), lambda b, h, i: (cblk + b, 2 * H + h))]
    args += [qkv, qkv]
    aliases = {}
    if o_prev is not None:
        in_specs.append(pl.BlockSpec(memory_space=pl.ANY))
        aliases = {len(args): 0}
        args.append(o_prev)
    return pl.pallas_call(
        functools.partial(_diff_kernel, has_x=main, has_alias=o_prev is not None,
                          chunk=math.gcd(S, DIFF_KEY_CHUNK), S=S, L=L,
                          lambda_init=lambda_init, Dh=Dh),
        out_shape=jax.ShapeDtypeStruct((n_out_rows, D), BF16),
        grid=(B, H, nq),
        in_specs=in_specs,
        out_specs=pl.BlockSpec((bq, HW), lambda b, h, i: (qrow(b, i), h)),
        scratch_shapes=[pltpu.VMEM((2, bq, nkeys), F32)],
        input_output_aliases=aliases,
        compiler_params=_cparams(3),
        name="diff_attention" if main else "diff_context_attention",
    )(*args)


def _rope_tables(cfg):
    Dh = cfg.head_dim
    t = jnp.arange(cfg.S, dtype=jnp.int32)
    row = (t // cfg.grid_w).astype(F32)
    col = (t % cfg.grid_w).astype(F32)
    axis_dim = Dh // 2
    inv_freq = ROPE_BASE ** (-jnp.arange(0, axis_dim, 2, dtype=F32) / axis_dim)
    ang_r = row[:, None] * inv_freq[None, :]
    ang_c = col[:, None] * inv_freq[None, :]
    ang = jnp.concatenate([ang_r, ang_r, ang_c, ang_c], axis=-1)
    cos, sin = jnp.cos(ang), jnp.sin(ang)
    quarter = jnp.arange(Dh) // (Dh // 4)
    sin_a = jnp.where(quarter % 2 == 0, -sin, 0.0)
    sin_b = jnp.where(quarter % 2 == 1, sin, 0.0)
    n_ctx = cfg.B * cfg.L

    def full(tab, fill):
        return jnp.concatenate([jnp.tile(tab, (cfg.B, 1)), jnp.full((n_ctx, Dh), fill, F32)], axis=0)

    return full(cos, 1.0), full(sin_a, 0.0), full(sin_b, 0.0)


def _head_gains(q_gain, k_gain, n_q, n_k, n_v_cols, head_dim):
    q_scale = head_dim ** -0.5 * LOG2E
    return jnp.concatenate([jnp.tile(q_gain.astype(F32) * q_scale, n_q), jnp.tile(k_gain.astype(F32), n_k),
                            jnp.ones((n_v_cols,), F32)]).reshape(1, -1)


def _forward(cfg, x, c, ctx, c_ctx, ada_down, ada_up, ada_bias, norm_mix, norm_ffn,
             na_w_qkv, na_w_o, na_q_norm, na_k_norm, na_rpb,
             diff_w_qkv, diff_w_o, diff_q_norm, diff_k_norm,
             diff_lambda_q1, diff_lambda_k1, diff_lambda_q2, diff_lambda_k2, diff_subln,
             swa_w_qkv, swa_w_o, swa_q_norm, swa_k_norm, swa_sink,
             ffn_w_gate, ffn_w_up, ffn_w_down,
             moe_router, moe_router_bias, moe_w_gate, moe_w_up, moe_w_down):
    B, S, L, D, Dh = cfg.B, cfg.S, cfg.L, cfg.D, cfg.head_dim
    M, MX = cfg.M, cfg.MX
    heads = D // Dh
    h = jnp.concatenate([x.reshape(MX, D), ctx.reshape(B * L, D)], axis=0).astype(F32)
    cond = jnp.zeros((MOD_ROWS, D), F32).at[:B].set(c).at[B].set(c_ctx)
    mods_all = _ada_all(cond, ada_down, ada_up, ada_bias)
    rope_tabs = _rope_tables(cfg)

    for i in range(cfg.depth):
        ctx_out = i < cfg.depth - 1
        n_rows = M if ctx_out else MX
        mods = mods_all[i]
        xn = _modulate(h, norm_mix[i], mods, 0, 1, M, cfg)
        kind, j = i % 3, i // 3
        if kind == 0:
            gains = _head_gains(na_q_norm[j], na_k_norm[j], heads, heads, D, Dh)
            qkv = _qkv_proj(xn, na_w_qkv, j, gains, 2 * D, None, cfg)
            o = _na_attention(qkv, na_rpb[j], n_rows, cfg)
            if ctx_out:
                o = _gqa_attention(qkv, None, heads, heads, False, o, n_rows, cfg)
            w_o = na_w_o
        elif kind == 1:
            gains = _head_gains(diff_q_norm[j], diff_k_norm[j], heads, heads, D, Dh)
            qkv = _qkv_proj(xn, diff_w_qkv, j, gains, 2 * D, rope_tabs, cfg)
            lams = (diff_lambda_q1[j], diff_lambda_k1[j], diff_lambda_q2[j], diff_lambda_k2[j])
            lambda_init = 0.8 - 0.6 * math.exp(-0.3 * i)
            o = _diff_attention(qkv, lams, diff_subln[j], lambda_init, True, None, n_rows, cfg)
            if ctx_out:
                o = _diff_attention(qkv, lams, diff_subln[j], lambda_init, False, o, n_rows, cfg)
            w_o = diff_w_o
        else:
            kvh = heads // cfg.swa_group
            gains = _head_gains(swa_q_norm[j], swa_k_norm[j], heads, kvh, kvh * Dh, Dh)
            qkv = _qkv_proj(xn, swa_w_qkv, j, gains, (heads + kvh) * Dh, rope_tabs, cfg)
            o = _gqa_attention(qkv, swa_sink[j], heads, kvh, True, None, n_rows, cfg)
            if ctx_out:
                o = _gqa_attention(qkv, swa_sink[j], heads, kvh, False, o, n_rows, cfg)
            w_o = swa_w_o
        h = _mm_res(o, w_o, j, h, mods, 2, n_rows, cfg)
        t = _modulate(h, norm_ffn[i], mods, 3, 4, n_rows, cfg)
        fj = i // 2
        if i % 2 == 0:
            n_ffn, _, d_ff = ffn_w_gate.shape
            hid = _swiglu_up(t, ffn_w_gate.reshape(n_ffn, 1, D, d_ff), ffn_w_up.reshape(n_ffn, 1, D, d_ff), fj, None)
            w_down = ffn_w_down
        else:
            gates = _router(t, moe_router[fj], moe_router_bias[fj], cfg)
            hid = _swiglu_up(t, moe_w_gate, moe_w_up, fj, gates)
            w_down = moe_w_down.reshape(moe_w_down.shape[0], -1, D)
        h = _mm_res(hid, w_down, fj, h, mods, 5, n_rows, cfg)
    return h.reshape(B, S, D)


_CFG = Cfg(B=2, S=4096, L=256, D=4096, depth=4, grid_w=64, head_dim=128, na_win_h=8, na_win_w=16,
           swa_window=128, swa_block=128, swa_group=4, n_experts=8, top_k=2)


def kernel(x, c, ctx, c_ctx, ada_down, ada_up, ada_bias, norm_mix, norm_ffn, na_w_qkv, na_w_o, na_q_norm, na_k_norm, na_rpb, diff_w_qkv, diff_w_o, diff_q_norm, diff_k_norm, diff_lambda_q1, diff_lambda_k1, diff_lambda_q2, diff_lambda_k2, diff_subln, swa_w_qkv, swa_w_o, swa_q_norm, swa_k_norm, swa_sink, ffn_w_gate, ffn_w_up, ffn_w_down, moe_router, moe_router_bias, moe_w_gate, moe_w_up, moe_w_down):
    return _forward(_CFG, x, c, ctx, c_ctx, ada_down, ada_up, ada_bias, norm_mix, norm_ffn, na_w_qkv, na_w_o, na_q_norm, na_k_norm, na_rpb, diff_w_qkv, diff_w_o, diff_q_norm, diff_k_norm, diff_lambda_q1, diff_lambda_k1, diff_lambda_q2, diff_lambda_k2, diff_subln, swa_w_qkv, swa_w_o, swa_q_norm, swa_k_norm, swa_sink, ffn_w_gate, ffn_w_up, ffn_w_down, moe_router, moe_router_bias, moe_w_gate, moe_w_up, moe_w_down)
```

```python
import functools
import math
from typing import NamedTuple

import jax
import jax.numpy as jnp
from jax import lax
from jax.experimental import pallas as pl
from jax.experimental.pallas import tpu as pltpu

F32 = jnp.float32
BF16 = jnp.bfloat16

LANE = 128
SUBLANE_BF16 = 16
MOD_ROWS = 8
VMEM_LIMIT = 58 * 1024 * 1024
MAX_FULL_K = 4096
K_TILE = 4096
RES_ROW_PARTS = 2
QKV_BLOCK_N = 1024
SWIGLU_BLOCK_N = 512

NA_GROUP_UNROLL = 2
GQA_QUERY_ROWS = 512
DIFF_KEY_CHUNK = 1024

NORM_EPS = 1e-6
NEG_INF = -1e30
ROPE_BASE = 10000.0
LOG2E = 1.4426950408889634


class Cfg(NamedTuple):
    B: int
    S: int
    L: int
    D: int
    depth: int
    grid_w: int
    head_dim: int
    na_win_h: int
    na_win_w: int
    swa_window: int
    swa_block: int
    swa_group: int
    n_experts: int
    top_k: int

    @property
    def M(self):
        return self.B * (self.S + self.L)

    @property
    def MX(self):
        return self.B * self.S


def _cparams(n_axes):
    return pltpu.CompilerParams(dimension_semantics=("arbitrary",) * n_axes,
                                vmem_limit_bytes=VMEM_LIMIT)


def _pick_block(n, cap, mult):
    best = None
    for d in range(mult, min(n, cap) + 1, mult):
        if n % d == 0:
            best = d
    assert best is not None, (n, cap, mult)
    return best


def _dot(a, b):
    return jnp.dot(a, b, preferred_element_type=F32)


def _dot_t(a, b):
    return lax.dot_general(a, b, (((1,), (1,)), ((), ())), preferred_element_type=F32)


def _row_select(vecs_ref, row0, bm, cfg):
    rows = row0 + lax.broadcasted_iota(jnp.int32, (bm, 1), 0)
    g = vecs_ref[cfg.B:cfg.B + 1, :]
    for b in reversed(range(cfg.B)):
        g = jnp.where(rows < (b + 1) * cfg.S, vecs_ref[b:b + 1, :], g)
    return g


def _ada_kernel(c_ref, down_ref, up_ref, b_ref, o_ref):
    c = c_ref[...]
    sc = (c * jax.nn.sigmoid(c)).astype(BF16)
    t = _dot(sc, down_ref[...].astype(BF16))
    o_ref[...] = _dot(t.astype(BF16), up_ref[...].astype(BF16)) + b_ref[...]


def _ada_all(cond, ada_down, ada_up, ada_bias):
    depth, D, R = ada_down.shape
    N = ada_up.shape[2]
    bn = _pick_block(N, 3072, LANE)
    return pl.pallas_call(
        _ada_kernel,
        out_shape=jax.ShapeDtypeStruct((depth, MOD_ROWS, N), F32),
        grid=(depth, N // bn),
        in_specs=[
            pl.BlockSpec((MOD_ROWS, D), lambda l, j: (0, 0)),
            pl.BlockSpec((None, D, R), lambda l, j: (l, 0, 0)),
            pl.BlockSpec((None, R, bn), lambda l, j: (l, 0, j)),
            pl.BlockSpec((None, 1, bn), lambda l, j: (l, 0, j)),
        ],
        out_specs=pl.BlockSpec((None, MOD_ROWS, bn), lambda l, j: (l, 0, j)),
        compiler_params=_cparams(2),
        name="ada_modulation",
    )(cond, ada_down, ada_up, ada_bias.reshape(depth, 1, N))


def _modulate_kernel(x_ref, g_ref, sh_ref, sc_ref, o_ref, *, bm, cfg):
    row0 = pl.program_id(0) * bm
    t = jnp.where(row0 < cfg.MX, row0 // cfg.S, cfg.B)
    x = x_ref[...]
    ms = jnp.mean(x * x, axis=-1, keepdims=True)
    y = x * lax.rsqrt(ms + NORM_EPS) * g_ref[...]
    o_ref[...] = (y * (1 + sc_ref[pl.ds(t, 1), :]) + sh_ref[pl.ds(t, 1), :]).astype(BF16)


def _modulate(h, gain, mods, shift_chunk, scale_chunk, n_rows, cfg):
    D = cfg.D
    bm = _pick_block(math.gcd(cfg.S, cfg.L), 256, SUBLANE_BF16)
    return pl.pallas_call(
        functools.partial(_modulate_kernel, bm=bm, cfg=cfg),
        out_shape=jax.ShapeDtypeStruct((n_rows, D), BF16),
        grid=(n_rows // bm,),
        in_specs=[
            pl.BlockSpec((bm, D), lambda i: (i, 0)),
            pl.BlockSpec((1, D), lambda i: (0, 0)),
            pl.BlockSpec((MOD_ROWS, D), lambda i: (0, shift_chunk)),
            pl.BlockSpec((MOD_ROWS, D), lambda i: (0, scale_chunk)),
        ],
        out_specs=pl.BlockSpec((bm, D), lambda i: (i, 0)),
        compiler_params=_cparams(1),
        name="modulate",
    )(h, gain.reshape(1, D), mods, mods)


def _stage_weight(w_ref, wb_ref, t, m, n_blocks):
    ck = w_ref.shape[0]

    @pl.when(t < n_blocks)
    def _():
        wb_ref[t % 2, pl.ds(pl.multiple_of(m * ck, ck), ck), :] = w_ref[...].astype(BF16)


def _qkv_kernel(*refs, n_blocks, n_norm_blocks, rope, bn):
    if rope:
        x_ref, w_ref, g_ref, cos_ref, sa_ref, sb_ref, o_ref, wb_ref = refs
    else:
        x_ref, w_ref, g_ref, o_ref, wb_ref = refs
    t = pl.program_id(0)
    _stage_weight(w_ref, wb_ref, t, pl.program_id(1), n_blocks)

    @pl.when(t > 0)
    def _():
        j = t - 1
        acc = _dot(x_ref[...], wb_ref[j % 2])

        @pl.when(j < n_norm_blocks)
        def _():
            outs = []
            for hh in range(bn // LANE):
                sl = slice(hh * LANE, (hh + 1) * LANE)
                xh = acc[:, sl]
                ms = jnp.mean(xh * xh, axis=-1, keepdims=True)
                y = xh * lax.rsqrt(ms + NORM_EPS) * g_ref[:, sl]
                if rope:
                    y = (y * cos_ref[...] + pltpu.roll(y, 3 * LANE // 4, 1) * sa_ref[...]
                         + pltpu.roll(y, LANE // 4, 1) * sb_ref[...])
                outs.append(y.astype(BF16))
            o_ref[...] = jnp.concatenate(outs, axis=1)

        @pl.when(j >= n_norm_blocks)
        def _():
            o_ref[...] = acc.astype(BF16)


def _qkv_proj(xn, w_all, layer, gains, n_norm_cols, rope_tabs, cfg):
    M, D = xn.shape
    N = w_all.shape[2]
    bn = math.gcd(math.gcd(N, n_norm_cols), QKV_BLOCK_N)
    bm = _pick_block(M, 1152, SUBLANE_BF16)
    n_m, n_blocks = M // bm, N // bn
    ck = D // n_m
    assert bn % LANE == 0 and D % n_m == 0 and ck % SUBLANE_BF16 == 0
    rope = rope_tabs is not None
    row = lambda t, m: jnp.where(t == 0, 0, m)
    col = lambda t: jnp.maximum(t - 1, 0)
    in_specs = [
        pl.BlockSpec((bm, D), lambda t, m: (row(t, m), 0)),
        pl.BlockSpec((None, ck, bn), lambda t, m: (layer, m, jnp.minimum(t, n_blocks - 1))),
        pl.BlockSpec((1, bn), lambda t, m: (0, col(t))),
    ]
    args = [xn, w_all, gains]
    if rope:
        in_specs += [pl.BlockSpec((bm, LANE), lambda t, m: (row(t, m), 0))] * 3
        args += list(rope_tabs)
    return pl.pallas_call(
        functools.partial(_qkv_kernel, n_blocks=n_blocks, n_norm_blocks=n_norm_cols // bn, rope=rope, bn=bn),
        out_shape=jax.ShapeDtypeStruct((M, N), BF16),
        grid=(n_blocks + 1, n_m),
        in_specs=in_specs,
        out_specs=pl.BlockSpec((bm, bn), lambda t, m: (row(t, m), col(t))),
        scratch_shapes=[pltpu.VMEM((2, D, bn), BF16)],
        compiler_params=_cparams(2),
        name="qkv_proj",
    )(*args)


def _mm_res_kernel(a_ref, w_ref, res_ref, gate_ref, o_ref, wb_ref, *acc, n_blocks, nk, n_m, bm, cfg):
    part = pl.program_id(0)
    t = pl.program_id(1)
    m = pl.program_id(2)
    _stage_weight(w_ref, wb_ref, t, m, n_blocks)

    def epilogue(total):
        row0 = (part * n_m + m) * bm
        o_ref[...] = res_ref[...] + _row_select(gate_ref, row0, bm, cfg) * total

    @pl.when(t > 0)
    def _():
        w = wb_ref.at[(t - 1) % 2]
        if nk == 1:
            epilogue(_dot(a_ref[...], w[...]))
        else:
            k = (t - 1) % nk
            acc_ref = acc[0]
            rows = pl.ds(pl.multiple_of(m * bm, 8), bm)

            @pl.when(k == 0)
            def _():
                acc_ref[rows, :] = _dot(a_ref[...], w[...])

            @pl.when(jnp.logical_and(k > 0, k < nk - 1))
            def _():
                acc_ref[rows, :] += _dot(a_ref[...], w[...])

            @pl.when(k == nk - 1)
            def _():
                epilogue(acc_ref[rows, :] + _dot(a_ref[...], w[...]))


def _mm_res(a, w_all, layer, res, mods, gate_chunk, n_rows, cfg):
    _, K, N = w_all.shape
    bn = 512
    bk = K if K <= MAX_FULL_K else _pick_block(K, K_TILE, LANE)
    nk = K // bk
    n_parts = RES_ROW_PARTS if nk > 1 else 1
    bm = _pick_block(n_rows // n_parts, 1152, SUBLANE_BF16)
    n_m = n_rows // n_parts // bm
    n_blocks = (N // bn) * nk
    ck = bk // n_m
    assert n_rows % n_parts == 0 and bk % n_m == 0 and ck % SUBLANE_BF16 == 0
    wblk = lambda t: jnp.minimum(t, n_blocks - 1)
    cblk = lambda t: jnp.maximum(t - 1, 0)
    orow = lambda p, t, m: p * n_m + jnp.where(jnp.logical_and(t > 0, cblk(t) % nk == nk - 1), m, 0)
    scratch = [pltpu.VMEM((2, bk, bn), BF16)]
    if nk > 1:
        scratch.append(pltpu.VMEM((n_rows // n_parts, bn), F32))
    return pl.pallas_call(
        functools.partial(_mm_res_kernel, n_blocks=n_blocks, nk=nk, n_m=n_m, bm=bm, cfg=cfg),
        out_shape=jax.ShapeDtypeStruct((n_rows, N), F32),
        grid=(n_parts, n_blocks + 1, n_m),
        in_specs=[
            pl.BlockSpec((bm, bk), lambda p, t, m: (p * n_m + jnp.where(t == 0, 0, m), cblk(t) % nk)),
            pl.BlockSpec((None, ck, bn), lambda p, t, m: (layer, (wblk(t) % nk) * n_m + m, wblk(t) // nk)),
            pl.BlockSpec((bm, bn), lambda p, t, m: (orow(p, t, m), cblk(t) // nk)),
            pl.BlockSpec((MOD_ROWS, bn), lambda p, t, m: (0, gate_chunk * (N // bn) + cblk(t) // nk)),
        ],
        out_specs=pl.BlockSpec((bm, bn), lambda p, t, m: (orow(p, t, m), cblk(t) // nk)),
        scratch_shapes=scratch,
        compiler_params=_cparams(3),
        name="matmul_residual",
    )(a, w_all, res, mods)


def _swiglu_kernel(*refs, gated, n_blocks, nbe):
    if gated:
        x_ref, wg_ref, wu_ref, gates_ref, o_ref, wgb_ref, wub_ref = refs
    else:
        x_ref, wg_ref, wu_ref, o_ref, wgb_ref, wub_ref = refs
    t = pl.program_id(0)
    m = pl.program_id(1)
    _stage_weight(wg_ref, wgb_ref, t, m, n_blocks)
    _stage_weight(wu_ref, wub_ref, t, m, n_blocks)

    @pl.when(t > 0)
    def _():
        j = t - 1
        x = x_ref[...]
        g = _dot(x, wgb_ref[j % 2])
        u = _dot(x, wub_ref[j % 2])
        hid = g * jax.nn.sigmoid(g) * u
        if gated:
            gates = gates_ref[...]
            lane = lax.broadcasted_iota(jnp.int32, gates.shape, 1)
            hid = hid * jnp.sum(jnp.where(lane == j // nbe, gates, 0.0), axis=-1, keepdims=True)
        o_ref[...] = hid.astype(BF16)


def _swiglu_up(t, w_gate, w_up, layer, gates):
    M, D = t.shape
    _, E, _, F = w_gate.shape
    bn = math.gcd(F, SWIGLU_BLOCK_N)
    nbe = F // bn
    bm = _pick_block(M, 1152, SUBLANE_BF16)
    n_m, n_blocks = M // bm, E * nbe
    ck = D // n_m
    assert D % n_m == 0 and ck % SUBLANE_BF16 == 0
    gated = gates is not None
    row = lambda t, m: jnp.where(t == 0, 0, m)
    col = lambda t: jnp.maximum(t - 1, 0)
    wblk = lambda t: jnp.minimum(t, n_blocks - 1)
    w_spec = pl.BlockSpec((None, None, ck, bn), lambda t, m: (layer, wblk(t) // nbe, m, wblk(t) % nbe))
    in_specs = [pl.BlockSpec((bm, D), lambda t, m: (row(t, m), 0)), w_spec, w_spec]
    args = [t, w_gate, w_up]
    if gated:
        in_specs.append(pl.BlockSpec((bm, LANE), lambda t, m: (row(t, m), 0)))
        args.append(gates)
    return pl.pallas_call(
        functools.partial(_swiglu_kernel, gated=gated, n_blocks=n_blocks, nbe=nbe),
        out_shape=jax.ShapeDtypeStruct((M, E * F), BF16),
        grid=(n_blocks + 1, n_m),
        in_specs=in_specs,
        out_specs=pl.BlockSpec((bm, bn), lambda t, m: (row(t, m), col(t))),
        scratch_shapes=[pltpu.VMEM((2, D, bn), BF16), pltpu.VMEM((2, D, bn), BF16)],
        compiler_params=_cparams(2),
        name="swiglu_up",
    )(*args)


def _router_kernel(t_ref, w_ref, b_ref, o_ref):
    logits = _dot(t_ref[...], w_ref[...].astype(BF16)) + b_ref[...]
    lane = lax.broadcasted_iota(jnp.int32, logits.shape, 1).astype(F32)
    top1 = jnp.max(logits, axis=-1, keepdims=True)
    i1 = jnp.min(jnp.where(logits == top1, lane, float(LANE)), axis=-1, keepdims=True)
    rest = jnp.where(lane == i1, -jnp.inf, logits)
    top2 = jnp.max(rest, axis=-1, keepdims=True)
    i2 = jnp.min(jnp.where(rest == top2, lane, float(LANE)), axis=-1, keepdims=True)
    e2 = jnp.exp(top2 - top1)
    inv = 1.0 / (1.0 + e2)
    o_ref[...] = jnp.where(lane == i1, inv, 0.0) + jnp.where(lane == i2, e2 * inv, 0.0)


def _router(t, router_w, router_b, cfg):
    M, D = t.shape
    E = cfg.n_experts
    assert cfg.top_k == 2 and E <= LANE
    w = jnp.zeros((D, LANE), F32).at[:, :E].set(router_w)
    b = jnp.full((1, LANE), NEG_INF, F32).at[0, :E].set(router_b)
    bm = _pick_block(M, 1152, SUBLANE_BF16)
    return pl.pallas_call(
        _router_kernel,
        out_shape=jax.ShapeDtypeStruct((M, LANE), F32),
        grid=(M // bm,),
        in_specs=[
            pl.BlockSpec((bm, D), lambda m: (m, 0)),
            pl.BlockSpec((D, LANE), lambda m: (0, 0)),
            pl.BlockSpec((1, LANE), lambda m: (0, 0)),
        ],
        out_specs=pl.BlockSpec((bm, LANE), lambda m: (m, 0)),
        compiler_params=_cparams(1),
        name="moe_router",
    )(t, w, b)


def _na_variant_groups(n_groups):
    return (0, 1 if n_groups > 2 else 0, n_groups - 1)


def _na_build_bias(tp_ref, bias_ref, *, rows, W, kh):
    n_groups = rows // kh
    lane = lax.broadcasted_iota(jnp.int32, (W, 2 * W), 1)
    neg = jnp.full((W, 2 * W), NEG_INF, F32)
    for variant, g in enumerate(_na_variant_groups(n_groups)):
        key_row = min(max(g * kh - kh // 2, 0), rows - 2 * kh)
        for ql in range(kh):
            qi = g * kh + ql
            r0 = min(max(qi - kh // 2, 0), rows - kh)
            for j in range(kh):
                ki = key_row + 2 * j
                first_ok = r0 <= ki < r0 + kh
                second_ok = r0 <= ki + 1 < r0 + kh
                if not (first_ok or second_ok):
                    blk = neg
                else:
                    blk = tp_ref[ki - qi + kh]
                    if not second_ok:
                        blk = jnp.where(lane < W, blk, NEG_INF)
                    elif not first_ok:
                        blk = jnp.where(lane >= W, blk, NEG_INF)
                bias_ref[variant, ql * W:(ql + 1) * W, j * 2 * W:(j + 1) * 2 * W] = blk


def _na_kernel(q_ref, k_ref, v_ref, ck_ref, cv_ref, tp_ref, o_ref, bias_ref, *, rows, W, kh, unroll):
    @pl.when(pl.program_id(1) == 0)
    def _():
        _na_build_bias(tp_ref, bias_ref, rows=rows, W=W, kh=kh)

    ck = ck_ref[...]
    cv = cv_ref[...]
    n_groups = rows // kh
    gq, gk = kh * W, 2 * kh * W

    def one_group(g):
        variant = jnp.where(g == 0, 0, jnp.where(g == n_groups - 1, 2, 1))
        key_row = jnp.clip(g * kh - kh // 2, 0, rows - 2 * kh)
        q = q_ref[pl.ds(pl.multiple_of(g * gq, gq), gq), :]
        kstart = pl.multiple_of(key_row * W, W)
        s = _dot_t(q, k_ref[pl.ds(kstart, gk), :]) + bias_ref[variant]
        sc = _dot_t(q, ck)
        m = jnp.maximum(jnp.max(s, axis=-1, keepdims=True), jnp.max(sc, axis=-1, keepdims=True))
        e = jnp.exp2(s - m)
        ec = jnp.exp2(sc - m)
        inv = 1.0 / (jnp.sum(e, axis=-1, keepdims=True) + jnp.sum(ec, axis=-1, keepdims=True))
        o = (_dot(e.astype(BF16), v_ref[pl.ds(kstart, gk), :]) + _dot(ec.astype(BF16), cv)) * inv
        return o.astype(BF16)

    def body(it, carry):
        outs = [one_group(it * unroll + u) for u in range(unroll)]
        start = pl.multiple_of(it * (unroll * gq), unroll * gq)
        o_ref[pl.ds(start, unroll * gq), :] = jnp.concatenate(outs, axis=0)
        return carry

    lax.fori_loop(0, n_groups // unroll, body, 0)


def _na_bias_blocks(rpb, cfg):
    W, kw, wh = cfg.grid_w, cfg.na_win_w, cfg.na_win_h
    H = rpb.shape[0]
    wq = jnp.arange(W)[:, None]
    kc = jnp.arange(W)[None, :]
    c0 = jnp.clip(wq - kw // 2, 0, W - kw)
    col_valid = (kc >= c0) & (kc < c0 + kw)
    cidx = jnp.clip(kc - wq + (kw - 1), 0, 2 * kw - 2)
    toe = jnp.where(col_valid[None, None], rpb[:, :, cidx], NEG_INF).astype(F32)
    pad = jnp.full((H, 1, W, W), NEG_INF, F32)
    toe = jnp.concatenate([pad, toe, pad], axis=1)
    return jnp.concatenate([toe[:, :-1], toe[:, 1:]], axis=-1) * LOG2E


def _na_attention(qkv, rpb, n_out_rows, cfg):
    B, S, L, D, W, Dh = cfg.B, cfg.S, cfg.L, cfg.D, cfg.grid_w, cfg.head_dim
    H = D // Dh
    rows = S // W
    kh = min(cfg.na_win_h, rows)
    n_groups = rows // kh
    assert rows % kh == 0 and n_groups >= 2 and kh == cfg.na_win_h and 2 * W == LANE
    blocks = _na_bias_blocks(rpb, cfg)
    cblk = cfg.MX // L
    return pl.pallas_call(
        functools.partial(_na_kernel, rows=rows, W=W, kh=kh, unroll=math.gcd(rows // kh, NA_GROUP_UNROLL)),
        out_shape=jax.ShapeDtypeStruct((n_out_rows, D), BF16),
        grid=(H, B),
        in_specs=[
            pl.BlockSpec((S, Dh), lambda h, b: (b, h)),
            pl.BlockSpec((S, Dh), lambda h, b: (b, H + h)),
            pl.BlockSpec((S, Dh), lambda h, b: (b, 2 * H + h)),
            pl.BlockSpec((L, Dh), lambda h, b: (cblk + b, H + h)),
            pl.BlockSpec((L, Dh), lambda h, b: (cblk + b, 2 * H + h)),
            pl.BlockSpec((None, 2 * kh, W, 2 * W), lambda h, b: (h, 0, 0, 0)),
        ],
        out_specs=pl.BlockSpec((S, Dh), lambda h, b: (b, h)),
        scratch_shapes=[pltpu.VMEM((3, kh * W, 2 * kh * W), F32)],
        compiler_params=_cparams(2),
        name="neighborhood_attention",
    )(qkv, qkv, qkv, qkv, qkv, blocks)


def _gqa_kernel(*refs, G, has_local, has_sink, has_alias, win, S, Dh):
    refs = list(refs)
    sink_ref = refs.pop(0) if has_sink else None
    q_ref = refs.pop(0)
    k_ref, v_ref, mask_ref = (refs.pop(0), refs.pop(0), refs.pop(0)) if has_local else (None, None, None)
    ck_ref, cv_ref = refs.pop(0), refs.pop(0)
    if has_alias:
        refs.pop(0)
    o_ref = refs.pop(0)
    hk = pl.program_id(1)
    rows = q_ref.shape[0]
    ck = ck_ref[...]
    cv = cv_ref[...]
    if has_local:
        nkw = rows + 2 * win
        start = pl.multiple_of(jnp.clip(pl.program_id(2) * rows - win, 0, S - nkw), win)
        kw = k_ref[pl.ds(start, nkw), :]
        vw = v_ref[pl.ds(start, nkw), :]
    outs = []
    for g in range(G):
        q = q_ref[:, g * Dh:(g + 1) * Dh]
        sc = _dot_t(q, ck)
        m = jnp.max(sc, axis=-1, keepdims=True)
        if has_local:
            s = _dot_t(q, kw) + mask_ref[...]
            m = jnp.maximum(m, jnp.max(s, axis=-1, keepdims=True))
        if has_sink:
            sk = sink_ref[hk, g] * LOG2E
            m = jnp.maximum(m, sk)
        ec = jnp.exp2(sc - m)
        l = jnp.sum(ec, axis=-1, keepdims=True)
        o = _dot(ec.astype(BF16), cv)
        if has_local:
            e = jnp.exp2(s - m)
            l = l + jnp.sum(e, axis=-1, keepdims=True)
            o = o + _dot(e.astype(BF16), vw)
        if has_sink:
            l = l + jnp.exp2(sk - m)
        outs.append((o * (1.0 / l)).astype(BF16))
    o_ref[...] = jnp.concatenate(outs, axis=1)


def _band_mask(rows, win, S):
    nkw = rows + 2 * win
    r = jnp.arange(rows)[:, None]
    c = jnp.arange(nkw)[None, :]
    offsets = (0, -win, -2 * win)
    return jnp.stack([jnp.where(jnp.abs(off + c - r) <= win, 0.0, NEG_INF) for off in offsets]).astype(F32)


def _gqa_attention(qkv, sink, q_heads, kv_heads, local, o_prev, n_out_rows, cfg):
    B, S, L, D, Dh = cfg.B, cfg.S, cfg.L, cfg.D, cfg.head_dim
    G = q_heads // kv_heads
    kcol = q_heads
    vcol = q_heads + kv_heads
    cblk = cfg.MX // L
    has_sink = sink is not None
    win = cfg.swa_window
    rows_step = GQA_QUERY_ROWS if local else L
    nq = S // rows_step if local else 1
    qrow = (lambda b, i: b * nq + i) if local else (lambda b, i: cblk + b)
    in_specs, args = [], []
    if has_sink:
        in_specs.append(pl.BlockSpec(memory_space=pltpu.SMEM))
        args.append(sink.reshape(kv_heads, G).astype(F32))
    in_specs.append(pl.BlockSpec((rows_step, G * Dh), lambda b, h, i: (qrow(b, i), h)))
    args.append(qkv)
    if local:
        assert cfg.swa_block == win and S % rows_step == 0 and nq >= 2 and win % SUBLANE_BF16 == 0
        nkw = rows_step + 2 * win
        variant = lambda i: jnp.where(i == 0, 0, jnp.where(i == nq - 1, 2, 1))
        in_specs += [pl.BlockSpec((S, Dh), lambda b, h, i: (b, kcol + h)),
                     pl.BlockSpec((S, Dh), lambda b, h, i: (b, vcol + h)),
                     pl.BlockSpec((None, rows_step, nkw), lambda b, h, i: (variant(i), 0, 0))]
        args += [qkv, qkv, _band_mask(rows_step, win, S)]
    in_specs += [pl.BlockSpec((L, Dh), lambda b, h, i: (cblk + b, kcol + h)),
                 pl.BlockSpec((L, Dh), lambda b, h, i: (cblk + b, vcol + h))]
    args += [qkv, qkv]
    aliases = {}
    if o_prev is not None:
        in_specs.append(pl.BlockSpec(memory_space=pl.ANY))
        aliases = {len(args): 0}
        args.append(o_prev)
    return pl.pallas_call(
        functools.partial(_gqa_kernel, G=G, has_local=local, has_sink=has_sink,
                          has_alias=o_prev is not None, win=win, S=S, Dh=Dh),
        out_shape=jax.ShapeDtypeStruct((n_out_rows, D), BF16),
        grid=(B, kv_heads, nq),
        in_specs=in_specs,
        out_specs=pl.BlockSpec((rows_step, G * Dh), lambda b, h, i: (qrow(b, i), h)),
        input_output_aliases=aliases,
        compiler_params=_cparams(3),
        name="gqa_window_attention" if local else "gqa_context_attention",
    )(*args)


def _diff_kernel(*refs, has_x, has_alias, chunk, S, L, lambda_init, Dh):
    refs = list(refs)
    lq1, lk1, lq2, lk2, sub_ref, q_ref = refs[:6]
    refs = refs[6:]
    kx_ref, vx_ref = (refs.pop(0), refs.pop(0)) if has_x else (None, None)
    kc_ref, vc_ref = refs.pop(0), refs.pop(0)
    if has_alias:
        refs.pop(0)
    o_ref, e_ref = refs
    lam = (jnp.exp(jnp.sum(lq1[...] * lk1[...], axis=-1, keepdims=True))
           - jnp.exp(jnp.sum(lq2[...] * lk2[...], axis=-1, keepdims=True)) + lambda_init)
    chunks = []
    if has_x:
        chunks += [(kx_ref, vx_ref, c * chunk, chunk, c * chunk) for c in range(S // chunk)]
    off = S if has_x else 0
    chunks.append((kc_ref, vc_ref, 0, L, off))
    bq = q_ref.shape[0]
    coef = []
    for mi in range(2):
        sl = slice(mi * Dh, (mi + 1) * Dh)
        q = q_ref[:, sl]
        mx = jnp.full((bq, 1), -jnp.inf, F32)
        l = jnp.zeros((bq, 1), F32)
        maxes = []
        for k_ref, _, r0, n, e0 in chunks:
            s = _dot_t(q, k_ref[r0:r0 + n, sl])
            mn = jnp.maximum(mx, jnp.max(s, axis=-1, keepdims=True))
            e = jnp.exp2(s - mn)
            l = l * jnp.exp2(mx - mn) + jnp.sum(e, axis=-1, keepdims=True)
            e_ref[mi, :, e0:e0 + n] = e
            maxes.append(mn)
            mx = mn
        inv = 1.0 / l
        coef.append([jnp.exp2(mc - mx) * inv for mc in maxes])
    acc = jnp.zeros((bq, 2 * Dh), F32)
    for ci, (_, v_ref, r0, n, e0) in enumerate(chunks):
        p = e_ref[0, :, e0:e0 + n] * coef[0][ci] - e_ref[1, :, e0:e0 + n] * (lam * coef[1][ci])
        acc = acc + _dot(p.astype(BF16), v_ref[r0:r0 + n, :])
    ms = jnp.mean(acc * acc, axis=-1, keepdims=True)
    y = acc * lax.rsqrt(ms + NORM_EPS) * sub_ref[...] * (1.0 - lambda_init)
    o_ref[...] = y.astype(BF16)


def _diff_attention(qkv, lams, subln, lambda_init, main, o_prev, n_out_rows, cfg):
    B, S, L, D, Dh = cfg.B, cfg.S, cfg.L, cfg.D, cfg.head_dim
    H = D // (2 * Dh)
    HW = 2 * Dh
    cblk = cfg.MX // L
    bq = 256 if main else L
    nq = S // bq if main else 1
    nkeys = S + L if main else L
    qrow = (lambda b, i: b * nq + i) if main else (lambda b, i: cblk + b)
    vec = pl.BlockSpec((1, Dh), lambda b, h, i: (0, 0))
    in_specs = [vec, vec, vec, vec,
                pl.BlockSpec((1, HW), lambda b, h, i: (0, 0)),
                pl.BlockSpec((bq, HW), lambda b, h, i: (qrow(b, i), h))]
    args = [v.reshape(1, Dh).astype(F32) for v in lams] + [subln.reshape(1, HW).astype(F32), qkv]
    if main:
        in_specs += [pl.BlockSpec((S, HW), lambda b, h, i: (b, H + h)),
                     pl.BlockSpec((S, HW), lambda b, h, i: (b, 2 * H + h))]
        args += [qkv, qkv]
    in_specs += [pl.BlockSpec((L, HW), lambda b, h, i: (cblk + b, H + h)),
                 pl.BlockSpec((L, HW), lambda b, h, i: (cblk + b, 2 * H + h))]
    args += [qkv, qkv]
    aliases = {}
    if o_prev is not None:
        in_specs.append(pl.BlockSpec(memory_space=pl.ANY))
        aliases = {len(args): 0}
        args.append(o_prev)
    return pl.pallas_call(
        functools.partial(_diff_kernel, has_x=main, has_alias=o_prev is not None,
                          chunk=math.gcd(S, DIFF_KEY_CHUNK), S=S, L=L,
                          lambda_init=lambda_init, Dh=Dh),
        out_shape=jax.ShapeDtypeStruct((n_out_rows, D), BF16),
        grid=(B, H, nq),
        in_specs=in_specs,
        out_specs=pl.BlockSpec((bq, HW), lambda b, h, i: (qrow(b, i), h)),
        scratch_shapes=[pltpu.VMEM((2, bq, nkeys), F32)],
        input_output_aliases=aliases,
        compiler_params=_cparams(3),
        name="diff_attention" if main else "diff_context_attention",
    )(*args)


def _rope_tables(cfg):
    Dh = cfg.head_dim
    t = jnp.arange(cfg.S, dtype=jnp.int32)
    row = (t // cfg.grid_w).astype(F32)
    col = (t % cfg.grid_w).astype(F32)
    axis_dim = Dh // 2
    inv_freq = ROPE_BASE ** (-jnp.arange(0, axis_dim, 2, dtype=F32) / axis_dim)
    ang_r = row[:, None] * inv_freq[None, :]
    ang_c = col[:, None] * inv_freq[None, :]
    ang = jnp.concatenate([ang_r, ang_r, ang_c, ang_c], axis=-1)
    cos, sin = jnp.cos(ang), jnp.sin(ang)
    quarter = jnp.arange(Dh) // (Dh // 4)
    sin_a = jnp.where(quarter % 2 == 0, -sin, 0.0)
    sin_b = jnp.where(quarter % 2 == 1, sin, 0.0)
    n_ctx = cfg.B * cfg.L

    def full(tab, fill):
        return jnp.concatenate([jnp.tile(tab, (cfg.B, 1)), jnp.full((n_ctx, Dh), fill, F32)], axis=0)

    return full(cos, 1.0), full(sin_a, 0.0), full(sin_b, 0.0)


def _head_gains(q_gain, k_gain, n_q, n_k, n_v_cols, head_dim):
    q_scale = head_dim ** -0.5 * LOG2E
    return jnp.concatenate([jnp.tile(q_gain.astype(F32) * q_scale, n_q), jnp.tile(k_gain.astype(F32), n_k),
                            jnp.ones((n_v_cols,), F32)]).reshape(1, -1)


def _forward(cfg, x, c, ctx, c_ctx, ada_down, ada_up, ada_bias, norm_mix, norm_ffn,
             na_w_qkv, na_w_o, na_q_norm, na_k_norm, na_rpb,
             diff_w_qkv, diff_w_o, diff_q_norm, diff_k_norm,
             diff_lambda_q1, diff_lambda_k1, diff_lambda_q2, diff_lambda_k2, diff_subln,
             swa_w_qkv, swa_w_o, swa_q_norm, swa_k_norm, swa_sink,
             ffn_w_gate, ffn_w_up, ffn_w_down,
             moe_router, moe_router_bias, moe_w_gate, moe_w_up, moe_w_down):
    B, S, L, D, Dh = cfg.B, cfg.S, cfg.L, cfg.D, cfg.head_dim
    M, MX = cfg.M, cfg.MX
    heads = D // Dh
    h = jnp.concatenate([x.reshape(MX, D), ctx.reshape(B * L, D)], axis=0).astype(F32)
    cond = jnp.zeros((MOD_ROWS, D), F32).at[:B].set(c).at[B].set(c_ctx)
    mods_all = _ada_all(cond, ada_down, ada_up, ada_bias)
    rope_tabs = _rope_tables(cfg)

    for i in range(cfg.depth):
        ctx_out = i < cfg.depth - 1
        n_rows = M if ctx_out else MX
        mods = mods_all[i]
        xn = _modulate(h, norm_mix[i], mods, 0, 1, M, cfg)
        kind, j = i % 3, i // 3
        if kind == 0:
            gains = _head_gains(na_q_norm[j], na_k_norm[j], heads, heads, D, Dh)
            qkv = _qkv_proj(xn, na_w_qkv, j, gains, 2 * D, None, cfg)
            o = _na_attention(qkv, na_rpb[j], n_rows, cfg)
            if ctx_out:
                o = _gqa_attention(qkv, None, heads, heads, False, o, n_rows, cfg)
            w_o = na_w_o
        elif kind == 1:
            gains = _head_gains(diff_q_norm[j], diff_k_norm[j], heads, heads, D, Dh)
            qkv = _qkv_proj(xn, diff_w_qkv, j, gains, 2 * D, rope_tabs, cfg)
            lams = (diff_lambda_q1[j], diff_lambda_k1[j], diff_lambda_q2[j], diff_lambda_k2[j])
            lambda_init = 0.8 - 0.6 * math.exp(-0.3 * i)
            o = _diff_attention(qkv, lams, diff_subln[j], lambda_init, True, None, n_rows, cfg)
            if ctx_out:
                o = _diff_attention(qkv, lams, diff_subln[j], lambda_init, False, o, n_rows, cfg)
            w_o = diff_w_o
        else:
            kvh = heads // cfg.swa_group
            gains = _head_gains(swa_q_norm[j], swa_k_norm[j], heads, kvh, kvh * Dh, Dh)
            qkv = _qkv_proj(xn, swa_w_qkv, j, gains, (heads + kvh) * Dh, rope_tabs, cfg)
            o = _gqa_attention(qkv, swa_sink[j], heads, kvh, True, None, n_rows, cfg)
            if ctx_out:
                o = _gqa_attention(qkv, swa_sink[j], heads, kvh, False, o, n_rows, cfg)
            w_o = swa_w_o
        h = _mm_res(o, w_o, j, h, mods, 2, n_rows, cfg)
        t = _modulate(h, norm_ffn[i], mods, 3, 4, n_rows, cfg)
        fj = i // 2
        if i % 2 == 0:
            n_ffn, _, d_ff = ffn_w_gate.shape
            hid = _swiglu_up(t, ffn_w_gate.reshape(n_ffn, 1, D, d_ff), ffn_w_up.reshape(n_ffn, 1, D, d_ff), fj, None)
            w_down = ffn_w_down
        else:
            gates = _router(t, moe_router[fj], moe_router_bias[fj], cfg)
            hid = _swiglu_up(t, moe_w_gate, moe_w_up, fj, gates)
            w_down = moe_w_down.reshape(moe_w_down.shape[0], -1, D)
        h = _mm_res(hid, w_down, fj, h, mods, 5, n_rows, cfg)
    return h.reshape(B, S, D)


_CFG = Cfg(B=2, S=4096, L=256, D=4096, depth=4, grid_w=64, head_dim=128, na_win_h=8, na_win_w=16,
           swa_window=128, swa_block=128, swa_group=4, n_experts=8, top_k=2)


def kernel(x, c, ctx, c_ctx, ada_down, ada_up, ada_bias, norm_mix, norm_ffn, na_w_qkv, na_w_o, na_q_norm, na_k_norm, na_rpb, diff_w_qkv, diff_w_o, diff_q_norm, diff_k_norm, diff_lambda_q1, diff_lambda_k1, diff_lambda_q2, diff_lambda_k2, diff_subln, swa_w_qkv, swa_w_o, swa_q_norm, swa_k_norm, swa_sink, ffn_w_gate, ffn_w_up, ffn_w_down, moe_router, moe_router_bias, moe_w_gate, moe_w_up, moe_w_down):
    return _forward(_CFG, x, c, ctx, c_ctx, ada_down, ada_up, ada_bias, norm_mix, norm_ffn, na_w_qkv, na_w_o, na_q_norm, na_k_norm, na_rpb, diff_w_qkv, diff_w_o, diff_q_norm, diff_k_norm, diff_lambda_q1, diff_lambda_k1, diff_lambda_q2, diff_lambda_k2, diff_subln, swa_w_qkv, swa_w_o, swa_q_norm, swa_k_norm, swa_sink, ffn_w_gate, ffn_w_up, ffn_w_down, moe_router, moe_router_bias, moe_w_gate, moe_w_up, moe_w_down)
```

```python
import functools
import math
from typing import NamedTuple

import jax
import jax.numpy as jnp
from jax import lax
from jax.experimental import pallas as pl
from jax.experimental.pallas import tpu as pltpu

F32 = jnp.float32
BF16 = jnp.bfloat16

LANE = 128
SUBLANE_BF16 = 16
MOD_ROWS = 8
VMEM_LIMIT = 58 * 1024 * 1024
MAX_FULL_K = 4096
K_TILE = 4096
RES_ROW_PARTS = 2
QKV_BLOCK_N = 1024
QKV_ROW_SPLIT = 4
SWIGLU_BLOCK_N = 512
SWIGLU_ROW_SPLIT = 2

NA_GROUP_UNROLL = 4
GQA_QUERY_ROWS = 512
DIFF_QUERY_ROWS = 512
DIFF_KEY_CHUNK = 1024
DIFF_ROW_SPLIT = 4

NORM_EPS = 1e-6
NEG_INF = -1e30
ROPE_BASE = 10000.0
LOG2E = 1.4426950408889634


class Cfg(NamedTuple):
    B: int
    S: int
    L: int
    D: int
    depth: int
    grid_w: int
    head_dim: int
    na_win_h: int
    na_win_w: int
    swa_window: int
    swa_block: int
    swa_group: int
    n_experts: int
    top_k: int

    @property
    def M(self):
        return self.B * (self.S + self.L)

    @property
    def MX(self):
        return self.B * self.S


def _cparams(n_axes):
    return pltpu.CompilerParams(dimension_semantics=("arbitrary",) * n_axes,
                                vmem_limit_bytes=VMEM_LIMIT)


def _pick_block(n, cap, mult):
    best = None
    for d in range(mult, min(n, cap) + 1, mult):
        if n % d == 0:
            best = d
    assert best is not None, (n, cap, mult)
    return best


def _dot(a, b):
    return jnp.dot(a, b, preferred_element_type=F32)


def _dot_t(a, b):
    return lax.dot_general(a, b, (((1,), (1,)), ((), ())), preferred_element_type=F32)


def _row_select(vecs_ref, row0, bm, cfg):
    rows = row0 + lax.broadcasted_iota(jnp.int32, (bm, 1), 0)
    g = vecs_ref[cfg.B:cfg.B + 1, :]
    for b in reversed(range(cfg.B)):
        g = jnp.where(rows < (b + 1) * cfg.S, vecs_ref[b:b + 1, :], g)
    return g


def _ada_kernel(c_ref, down_ref, up_ref, b_ref, o_ref):
    c = c_ref[...]
    sc = (c * jax.nn.sigmoid(c)).astype(BF16)
    t = _dot(sc, down_ref[...].astype(BF16))
    o_ref[...] = _dot(t.astype(BF16), up_ref[...].astype(BF16)) + b_ref[...]


def _ada_all(cond, ada_down, ada_up, ada_bias):
    depth, D, R = ada_down.shape
    N = ada_up.shape[2]
    bn = _pick_block(N, 3072, LANE)
    return pl.pallas_call(
        _ada_kernel,
        out_shape=jax.ShapeDtypeStruct((depth, MOD_ROWS, N), F32),
        grid=(depth, N // bn),
        in_specs=[
            pl.BlockSpec((MOD_ROWS, D), lambda l, j: (0, 0)),
            pl.BlockSpec((None, D, R), lambda l, j: (l, 0, 0)),
            pl.BlockSpec((None, R, bn), lambda l, j: (l, 0, j)),
            pl.BlockSpec((None, 1, bn), lambda l, j: (l, 0, j)),
        ],
        out_specs=pl.BlockSpec((None, MOD_ROWS, bn), lambda l, j: (l, 0, j)),
        compiler_params=_cparams(2),
        name="ada_modulation",
    )(cond, ada_down, ada_up, ada_bias.reshape(depth, 1, N))


def _modulate_kernel(x_ref, g_ref, sh_ref, sc_ref, o_ref, *, bm, cfg):
    row0 = pl.program_id(0) * bm
    t = jnp.where(row0 < cfg.MX, row0 // cfg.S, cfg.B)
    x = x_ref[...]
    ms = jnp.mean(x * x, axis=-1, keepdims=True)
    y = x * lax.rsqrt(ms + NORM_EPS) * g_ref[...]
    o_ref[...] = (y * (1 + sc_ref[pl.ds(t, 1), :]) + sh_ref[pl.ds(t, 1), :]).astype(BF16)


def _modulate(h, gain, mods, shift_chunk, scale_chunk, n_rows, cfg):
    D = cfg.D
    bm = _pick_block(math.gcd(cfg.S, cfg.L), 256, SUBLANE_BF16)
    return pl.pallas_call(
        functools.partial(_modulate_kernel, bm=bm, cfg=cfg),
        out_shape=jax.ShapeDtypeStruct((n_rows, D), BF16),
        grid=(n_rows // bm,),
        in_specs=[
            pl.BlockSpec((bm, D), lambda i: (i, 0)),
            pl.BlockSpec((1, D), lambda i: (0, 0)),
            pl.BlockSpec((MOD_ROWS, D), lambda i: (0, shift_chunk)),
            pl.BlockSpec((MOD_ROWS, D), lambda i: (0, scale_chunk)),
        ],
        out_specs=pl.BlockSpec((bm, D), lambda i: (i, 0)),
        compiler_params=_cparams(1),
        name="modulate",
    )(h, gain.reshape(1, D), mods, mods)


def _stage_weight(w_ref, wb_ref, t, m, n_blocks):
    ck = w_ref.shape[0]

    @pl.when(t < n_blocks)
    def _():
        wb_ref[t % 2, pl.ds(pl.multiple_of(m * ck, ck), ck), :] = w_ref[...].astype(BF16)


def _qkv_kernel(*refs, n_blocks, n_norm_blocks, rope, bn):
    if rope:
        x_ref, w_ref, g_ref, cos_ref, sa_ref, sb_ref, o_ref, wb_ref = refs
    else:
        x_ref, w_ref, g_ref, o_ref, wb_ref = refs
    t = pl.program_id(0)
    _stage_weight(w_ref, wb_ref, t, pl.program_id(1), n_blocks)

    bm = x_ref.shape[0]
    rs = bm // QKV_ROW_SPLIT

    @pl.when(jnp.logical_and(t > 0, t - 1 < n_norm_blocks))
    def _():
        w = wb_ref.at[(t - 1) % 2]
        for r in range(QKV_ROW_SPLIT):
            rows = slice(r * rs, (r + 1) * rs)
            acc = _dot(x_ref[rows, :], w[...])
            outs = []
            for hh in range(bn // LANE):
                sl = slice(hh * LANE, (hh + 1) * LANE)
                xh = acc[:, sl]
                ms = jnp.mean(xh * xh, axis=-1, keepdims=True)
                y = xh * lax.rsqrt(ms + NORM_EPS) * g_ref[:, sl]
                if rope:
                    y = (y * cos_ref[rows, :] + pltpu.roll(y, 3 * LANE // 4, 1) * sa_ref[rows, :]
                         + pltpu.roll(y, LANE // 4, 1) * sb_ref[rows, :])
                outs.append(y.astype(BF16))
            o_ref[rows, :] = jnp.concatenate(outs, axis=1)

    @pl.when(jnp.logical_and(t > 0, t - 1 >= n_norm_blocks))
    def _():
        o_ref[...] = _dot(x_ref[...], wb_ref[(t - 1) % 2]).astype(BF16)


def _qkv_proj(xn, w_all, layer, gains, n_norm_cols, rope_tabs, cfg):
    M, D = xn.shape
    N = w_all.shape[2]
    bn = math.gcd(math.gcd(N, n_norm_cols), QKV_BLOCK_N)
    bm = _pick_block(M, 1152, SUBLANE_BF16)
    n_m, n_blocks = M // bm, N // bn
    ck = D // n_m
    assert bn % LANE == 0 and D % n_m == 0 and ck % SUBLANE_BF16 == 0
    rope = rope_tabs is not None
    row = lambda t, m: jnp.where(t == 0, 0, m)
    col = lambda t: jnp.maximum(t - 1, 0)
    in_specs = [
        pl.BlockSpec((bm, D), lambda t, m: (row(t, m), 0)),
        pl.BlockSpec((None, ck, bn), lambda t, m: (layer, m, jnp.minimum(t, n_blocks - 1))),
        pl.BlockSpec((1, bn), lambda t, m: (0, col(t))),
    ]
    args = [xn, w_all, gains]
    if rope:
        in_specs += [pl.BlockSpec((bm, LANE), lambda t, m: (row(t, m), 0))] * 3
        args += list(rope_tabs)
    return pl.pallas_call(
        functools.partial(_qkv_kernel, n_blocks=n_blocks, n_norm_blocks=n_norm_cols // bn, rope=rope, bn=bn),
        out_shape=jax.ShapeDtypeStruct((M, N), BF16),
        grid=(n_blocks + 1, n_m),
        in_specs=in_specs,
        out_specs=pl.BlockSpec((bm, bn), lambda t, m: (row(t, m), col(t))),
        scratch_shapes=[pltpu.VMEM((2, D, bn), BF16)],
        compiler_params=_cparams(2),
        name="qkv_proj",
    )(*args)


def _mm_res_kernel(a_ref, w_ref, res_ref, gate_ref, o_ref, wb_ref, *acc, n_blocks, nk, n_m, bm, cfg):
    part = pl.program_id(0)
    t = pl.program_id(1)
    m = pl.program_id(2)
    _stage_weight(w_ref, wb_ref, t, m, n_blocks)

    def epilogue(total):
        row0 = (part * n_m + m) * bm
        o_ref[...] = res_ref[...] + _row_select(gate_ref, row0, bm, cfg) * total

    @pl.when(t > 0)
    def _():
        w = wb_ref.at[(t - 1) % 2]
        if nk == 1:
            epilogue(_dot(a_ref[...], w[...]))
        else:
            k = (t - 1) % nk
            acc_ref = acc[0]
            rows = pl.ds(pl.multiple_of(m * bm, 8), bm)

            @pl.when(k == 0)
            def _():
                acc_ref[rows, :] = _dot(a_ref[...], w[...])

            @pl.when(jnp.logical_and(k > 0, k < nk - 1))
            def _():
                acc_ref[rows, :] += _dot(a_ref[...], w[...])

            @pl.when(k == nk - 1)
            def _():
                epilogue(acc_ref[rows, :] + _dot(a_ref[...], w[...]))


def _mm_res(a, w_all, layer, res, mods, gate_chunk, n_rows, cfg):
    _, K, N = w_all.shape
    bn = 512
    bk = K if K <= MAX_FULL_K else _pick_block(K, K_TILE, LANE)
    nk = K // bk
    n_parts = RES_ROW_PARTS if nk > 1 else 1
    bm = _pick_block(n_rows // n_parts, 1152, SUBLANE_BF16)
    n_m = n_rows // n_parts // bm
    n_blocks = (N // bn) * nk
    ck = bk // n_m
    assert n_rows % n_parts == 0 and bk % n_m == 0 and ck % SUBLANE_BF16 == 0
    wblk = lambda t: jnp.minimum(t, n_blocks - 1)
    cblk = lambda t: jnp.maximum(t - 1, 0)
    orow = lambda p, t, m: p * n_m + jnp.where(jnp.logical_and(t > 0, cblk(t) % nk == nk - 1), m, 0)
    scratch = [pltpu.VMEM((2, bk, bn), BF16)]
    if nk > 1:
        scratch.append(pltpu.VMEM((n_rows // n_parts, bn), F32))
    return pl.pallas_call(
        functools.partial(_mm_res_kernel, n_blocks=n_blocks, nk=nk, n_m=n_m, bm=bm, cfg=cfg),
        out_shape=jax.ShapeDtypeStruct((n_rows, N), F32),
        grid=(n_parts, n_blocks + 1, n_m),
        in_specs=[
            pl.BlockSpec((bm, bk), lambda p, t, m: (p * n_m + jnp.where(t == 0, 0, m), cblk(t) % nk)),
            pl.BlockSpec((None, ck, bn), lambda p, t, m: (layer, (wblk(t) % nk) * n_m + m, wblk(t) // nk)),
            pl.BlockSpec((bm, bn), lambda p, t, m: (orow(p, t, m), cblk(t) // nk)),
            pl.BlockSpec((MOD_ROWS, bn), lambda p, t, m: (0, gate_chunk * (N // bn) + cblk(t) // nk)),
        ],
        out_specs=pl.BlockSpec((bm, bn), lambda p, t, m: (orow(p, t, m), cblk(t) // nk)),
        scratch_shapes=scratch,
        compiler_params=_cparams(3),
        name="matmul_residual",
    )(a, w_all, res, mods)


def _swiglu_kernel(*refs, gated, n_blocks, nbe):
    if gated:
        x_ref, wg_ref, wu_ref, gates_ref, o_ref, wgb_ref, wub_ref = refs
    else:
        x_ref, wg_ref, wu_ref, o_ref, wgb_ref, wub_ref = refs
    t = pl.program_id(0)
    m = pl.program_id(1)
    _stage_weight(wg_ref, wgb_ref, t, m, n_blocks)
    _stage_weight(wu_ref, wub_ref, t, m, n_blocks)

    rs = x_ref.shape[0] // SWIGLU_ROW_SPLIT

    @pl.when(t > 0)
    def _():
        j = t - 1
        wg = wgb_ref.at[j % 2]
        wu = wub_ref.at[j % 2]
        for r in range(SWIGLU_ROW_SPLIT):
            rows = slice(r * rs, (r + 1) * rs)
            x = x_ref[rows, :]
            g = _dot(x, wg[...])
            u = _dot(x, wu[...])
            hid = g * jax.nn.sigmoid(g) * u
            if gated:
                gates = gates_ref[rows, :]
                lane = lax.broadcasted_iota(jnp.int32, gates.shape, 1)
                hid = hid * jnp.sum(jnp.where(lane == j // nbe, gates, 0.0), axis=-1, keepdims=True)
            o_ref[rows, :] = hid.astype(BF16)


def _swiglu_up(t, w_gate, w_up, layer, gates):
    M, D = t.shape
    _, E, _, F = w_gate.shape
    bn = math.gcd(F, SWIGLU_BLOCK_N)
    nbe = F // bn
    bm = _pick_block(M, 1152, SUBLANE_BF16)
    n_m, n_blocks = M // bm, E * nbe
    ck = D // n_m
    assert D % n_m == 0 and ck % SUBLANE_BF16 == 0
    gated = gates is not None
    row = lambda t, m: jnp.where(t == 0, 0, m)
    col = lambda t: jnp.maximum(t - 1, 0)
    wblk = lambda t: jnp.minimum(t, n_blocks - 1)
    w_spec = pl.BlockSpec((None, None, ck, bn), lambda t, m: (layer, wblk(t) // nbe, m, wblk(t) % nbe))
    in_specs = [pl.BlockSpec((bm, D), lambda t, m: (row(t, m), 0)), w_spec, w_spec]
    args = [t, w_gate, w_up]
    if gated:
        in_specs.append(pl.BlockSpec((bm, LANE), lambda t, m: (row(t, m), 0)))
        args.append(gates)
    return pl.pallas_call(
        functools.partial(_swiglu_kernel, gated=gated, n_blocks=n_blocks, nbe=nbe),
        out_shape=jax.ShapeDtypeStruct((M, E * F), BF16),
        grid=(n_blocks + 1, n_m),
        in_specs=in_specs,
        out_specs=pl.BlockSpec((bm, bn), lambda t, m: (row(t, m), col(t))),
        scratch_shapes=[pltpu.VMEM((2, D, bn), BF16), pltpu.VMEM((2, D, bn), BF16)],
        compiler_params=_cparams(2),
        name="swiglu_up",
    )(*args)


def _router_kernel(t_ref, w_ref, b_ref, o_ref):
    logits = _dot(t_ref[...], w_ref[...].astype(BF16)) + b_ref[...]
    lane = lax.broadcasted_iota(jnp.int32, logits.shape, 1).astype(F32)
    top1 = jnp.max(logits, axis=-1, keepdims=True)
    i1 = jnp.min(jnp.where(logits == top1, lane, float(LANE)), axis=-1, keepdims=True)
    rest = jnp.where(lane == i1, -jnp.inf, logits)
    top2 = jnp.max(rest, axis=-1, keepdims=True)
    i2 = jnp.min(jnp.where(rest == top2, lane, float(LANE)), axis=-1, keepdims=True)
    e2 = jnp.exp(top2 - top1)
    inv = 1.0 / (1.0 + e2)
    o_ref[...] = jnp.where(lane == i1, inv, 0.0) + jnp.where(lane == i2, e2 * inv, 0.0)


def _router(t, router_w, router_b, cfg):
    M, D = t.shape
    E = cfg.n_experts
    assert cfg.top_k == 2 and E <= LANE
    w = jnp.zeros((D, LANE), F32).at[:, :E].set(router_w)
    b = jnp.full((1, LANE), NEG_INF, F32).at[0, :E].set(router_b)
    bm = _pick_block(M, 1152, SUBLANE_BF16)
    return pl.pallas_call(
        _router_kernel,
        out_shape=jax.ShapeDtypeStruct((M, LANE), F32),
        grid=(M // bm,),
        in_specs=[
            pl.BlockSpec((bm, D), lambda m: (m, 0)),
            pl.BlockSpec((D, LANE), lambda m: (0, 0)),
            pl.BlockSpec((1, LANE), lambda m: (0, 0)),
        ],
        out_specs=pl.BlockSpec((bm, LANE), lambda m: (m, 0)),
        compiler_params=_cparams(1),
        name="moe_router",
    )(t, w, b)


def _na_variant_groups(n_groups):
    return (0, 1 if n_groups > 2 else 0, n_groups - 1)


def _na_build_bias(tp_ref, bias_ref, *, rows, W, kh):
    n_groups = rows // kh
    lane = lax.broadcasted_iota(jnp.int32, (W, 2 * W), 1)
    neg = jnp.full((W, 2 * W), NEG_INF, F32)
    for variant, g in enumerate(_na_variant_groups(n_groups)):
        key_row = min(max(g * kh - kh // 2, 0), rows - 2 * kh)
        for ql in range(kh):
            qi = g * kh + ql
            r0 = min(max(qi - kh // 2, 0), rows - kh)
            for j in range(kh):
                ki = key_row + 2 * j
                first_ok = r0 <= ki < r0 + kh
                second_ok = r0 <= ki + 1 < r0 + kh
                if not (first_ok or second_ok):
                    blk = neg
                else:
                    blk = tp_ref[ki - qi + kh]
                    if not second_ok:
                        blk = jnp.where(lane < W, blk, NEG_INF)
                    elif not first_ok:
                        blk = jnp.where(lane >= W, blk, NEG_INF)
                bias_ref[variant, ql * W:(ql + 1) * W, j * 2 * W:(j + 1) * 2 * W] = blk


def _na_kernel(q_ref, k_ref, v_ref, ck_ref, cv_ref, tp_ref, o_ref, bias_ref, *, rows, W, kh, unroll):
    @pl.when(pl.program_id(1) == 0)
    def _():
        _na_build_bias(tp_ref, bias_ref, rows=rows, W=W, kh=kh)

    ck = ck_ref[...]
    cv = cv_ref[...]
    n_groups = rows // kh
    gq, gk = kh * W, 2 * kh * W

    def one_group(g):
        variant = jnp.where(g == 0, 0, jnp.where(g == n_groups - 1, 2, 1))
        key_row = jnp.clip(g * kh - kh // 2, 0, rows - 2 * kh)
        q = q_ref[pl.ds(pl.multiple_of(g * gq, gq), gq), :]
        kstart = pl.multiple_of(key_row * W, W)
        s = _dot_t(q, k_ref[pl.ds(kstart, gk), :]) + bias_ref[variant]
        sc = _dot_t(q, ck)
        m = jnp.maximum(jnp.max(s, axis=-1, keepdims=True), jnp.max(sc, axis=-1, keepdims=True))
        e = jnp.exp2(s - m)
        ec = jnp.exp2(sc - m)
        inv = 1.0 / (jnp.sum(e, axis=-1, keepdims=True) + jnp.sum(ec, axis=-1, keepdims=True))
        o = (_dot(e.astype(BF16), v_ref[pl.ds(kstart, gk), :]) + _dot(ec.astype(BF16), cv)) * inv
        return o.astype(BF16)

    def body(it, carry):
        outs = [one_group(it * unroll + u) for u in range(unroll)]
        start = pl.multiple_of(it * (unroll * gq), unroll * gq)
        o_ref[pl.ds(start, unroll * gq), :] = jnp.concatenate(outs, axis=0)
        return carry

    lax.fori_loop(0, n_groups // unroll, body, 0)


def _na_bias_blocks(rpb, cfg):
    W, kw, wh = cfg.grid_w, cfg.na_win_w, cfg.na_win_h
    H = rpb.shape[0]
    wq = jnp.arange(W)[:, None]
    kc = jnp.arange(W)[None, :]
    c0 = jnp.clip(wq - kw // 2, 0, W - kw)
    col_valid = (kc >= c0) & (kc < c0 + kw)
    cidx = jnp.clip(kc - wq + (kw - 1), 0, 2 * kw - 2)
    toe = jnp.where(col_valid[None, None], rpb[:, :, cidx], NEG_INF).astype(F32)
    pad = jnp.full((H, 1, W, W), NEG_INF, F32)
    toe = jnp.concatenate([pad, toe, pad], axis=1)
    return jnp.concatenate([toe[:, :-1], toe[:, 1:]], axis=-1) * LOG2E


def _na_attention(qkv, rpb, n_out_rows, cfg):
    B, S, L, D, W, Dh = cfg.B, cfg.S, cfg.L, cfg.D, cfg.grid_w, cfg.head_dim
    H = D // Dh
    rows = S // W
    kh = min(cfg.na_win_h, rows)
    n_groups = rows // kh
    assert rows % kh == 0 and n_groups >= 2 and kh == cfg.na_win_h and 2 * W == LANE
    blocks = _na_bias_blocks(rpb, cfg)
    cblk = cfg.MX // L
    return pl.pallas_call(
        functools.partial(_na_kernel, rows=rows, W=W, kh=kh, unroll=math.gcd(rows // kh, NA_GROUP_UNROLL)),
        out_shape=jax.ShapeDtypeStruct((n_out_rows, D), BF16),
        grid=(H, B),
        in_specs=[
            pl.BlockSpec((S, Dh), lambda h, b: (b, h)),
            pl.BlockSpec((S, Dh), lambda h, b: (b, H + h)),
            pl.BlockSpec((S, Dh), lambda h, b: (b, 2 * H + h)),
            pl.BlockSpec((L, Dh), lambda h, b: (cblk + b, H + h)),
            pl.BlockSpec((L, Dh), lambda h, b: (cblk + b, 2 * H + h)),
            pl.BlockSpec((None, 2 * kh, W, 2 * W), lambda h, b: (h, 0, 0, 0)),
        ],
        out_specs=pl.BlockSpec((S, Dh), lambda h, b: (b, h)),
        scratch_shapes=[pltpu.VMEM((3, kh * W, 2 * kh * W), F32)],
        compiler_params=_cparams(2),
        name="neighborhood_attention",
    )(qkv, qkv, qkv, qkv, qkv, blocks)


def _gqa_kernel(*refs, G, has_local, has_sink, has_alias, win, S, Dh):
    refs = list(refs)
    sink_ref = refs.pop(0) if has_sink else None
    q_ref = refs.pop(0)
    k_ref, v_ref, mask_ref = (refs.pop(0), refs.pop(0), refs.pop(0)) if has_local else (None, None, None)
    ck_ref, cv_ref = refs.pop(0), refs.pop(0)
    if has_alias:
        refs.pop(0)
    o_ref = refs.pop(0)
    hk = pl.program_id(1)
    rows = q_ref.shape[0]
    ck = ck_ref[...]
    cv = cv_ref[...]
    if has_local:
        nkw = rows + 2 * win
        start = pl.multiple_of(jnp.clip(pl.program_id(2) * rows - win, 0, S - nkw), win)
        kw = k_ref[pl.ds(start, nkw), :]
        vw = v_ref[pl.ds(start, nkw), :]
    outs = []
    for g in range(G):
        q = q_ref[:, g * Dh:(g + 1) * Dh]
        sc = _dot_t(q, ck)
        m = jnp.max(sc, axis=-1, keepdims=True)
        if has_local:
            s = _dot_t(q, kw) + mask_ref[...]
            m = jnp.maximum(m, jnp.max(s, axis=-1, keepdims=True))
        if has_sink:
            sk = sink_ref[hk, g] * LOG2E
            m = jnp.maximum(m, sk)
        ec = jnp.exp2(sc - m)
        l = jnp.sum(ec, axis=-1, keepdims=True)
        o = _dot(ec.astype(BF16), cv)
        if has_local:
            e = jnp.exp2(s - m)
            l = l + jnp.sum(e, axis=-1, keepdims=True)
            o = o + _dot(e.astype(BF16), vw)
        if has_sink:
            l = l + jnp.exp2(sk - m)
        outs.append((o * (1.0 / l)).astype(BF16))
    o_ref[...] = jnp.concatenate(outs, axis=1)


def _band_mask(rows, win, S):
    nkw = rows + 2 * win
    r = jnp.arange(rows)[:, None]
    c = jnp.arange(nkw)[None, :]
    offsets = (0, -win, -2 * win)
    return jnp.stack([jnp.where(jnp.abs(off + c - r) <= win, 0.0, NEG_INF) for off in offsets]).astype(F32)


def _gqa_attention(qkv, sink, q_heads, kv_heads, local, o_prev, n_out_rows, cfg):
    B, S, L, D, Dh = cfg.B, cfg.S, cfg.L, cfg.D, cfg.head_dim
    G = q_heads // kv_heads
    kcol = q_heads
    vcol = q_heads + kv_heads
    cblk = cfg.MX // L
    has_sink = sink is not None
    win = cfg.swa_window
    rows_step = GQA_QUERY_ROWS if local else L
    nq = S // rows_step if local else 1
    qrow = (lambda b, i: b * nq + i) if local else (lambda b, i: cblk + b)
    in_specs, args = [], []
    if has_sink:
        in_specs.append(pl.BlockSpec(memory_space=pltpu.SMEM))
        args.append(sink.reshape(kv_heads, G).astype(F32))
    in_specs.append(pl.BlockSpec((rows_step, G * Dh), lambda b, h, i: (qrow(b, i), h)))
    args.append(qkv)
    if local:
        assert cfg.swa_block == win and S % rows_step == 0 and nq >= 2 and win % SUBLANE_BF16 == 0
        nkw = rows_step + 2 * win
        variant = lambda i: jnp.where(i == 0, 0, jnp.where(i == nq - 1, 2, 1))
        in_specs += [pl.BlockSpec((S, Dh), lambda b, h, i: (b, kcol + h)),
                     pl.BlockSpec((S, Dh), lambda b, h, i: (b, vcol + h)),
                     pl.BlockSpec((None, rows_step, nkw), lambda b, h, i: (variant(i), 0, 0))]
        args += [qkv, qkv, _band_mask(rows_step, win, S)]
    in_specs += [pl.BlockSpec((L, Dh), lambda b, h, i: (cblk + b, kcol + h)),
                 pl.BlockSpec((L, Dh), lambda b, h, i: (cblk + b, vcol + h))]
    args += [qkv, qkv]
    aliases = {}
    if o_prev is not None:
        in_specs.append(pl.BlockSpec(memory_space=pl.ANY))
        aliases = {len(args): 0}
        args.append(o_prev)
    return pl.pallas_call(
        functools.partial(_gqa_kernel, G=G, has_local=local, has_sink=has_sink,
                          has_alias=o_prev is not None, win=win, S=S, Dh=Dh),
        out_shape=jax.ShapeDtypeStruct((n_out_rows, D), BF16),
        grid=(B, kv_heads, nq),
        in_specs=in_specs,
        out_specs=pl.BlockSpec((rows_step, G * Dh), lambda b, h, i: (qrow(b, i), h)),
        input_output_aliases=aliases,
        compiler_params=_cparams(3),
        name="gqa_window_attention" if local else "gqa_context_attention",
    )(*args)


def _diff_kernel(*refs, has_x, has_alias, chunk, row_split, S, L, lambda_init, Dh):
    refs = list(refs)
    lq1, lk1, lq2, lk2, sub_ref, q_ref = refs[:6]
    refs = refs[6:]
    kx_ref, vx_ref = (refs.pop(0), refs.pop(0)) if has_x else (None, None)
    kc_ref, vc_ref = refs.pop(0), refs.pop(0)
    if has_alias:
        refs.pop(0)
    o_ref, e_ref = refs
    lam = (jnp.exp(jnp.sum(lq1[...] * lk1[...], axis=-1, keepdims=True))
           - jnp.exp(jnp.sum(lq2[...] * lk2[...], axis=-1, keepdims=True)) + lambda_init)
    chunks = []
    if has_x:
        chunks += [(kx_ref, vx_ref, c * chunk, chunk, c * chunk) for c in range(S // chunk)]
    off = S if has_x else 0
    chunks.append((kc_ref, vc_ref, 0, L, off))
    rs = q_ref.shape[0] // row_split
    for r in range(row_split):
        rows = slice(r * rs, (r + 1) * rs)
        coef = []
        for mi in range(2):
            sl = slice(mi * Dh, (mi + 1) * Dh)
            q = q_ref[rows, sl]
            mx = jnp.full((rs, 1), -jnp.inf, F32)
            l = jnp.zeros((rs, 1), F32)
            maxes = []
            for k_ref, _, r0, n, e0 in chunks:
                s = _dot_t(q, k_ref[r0:r0 + n, sl])
                mn = jnp.maximum(mx, jnp.max(s, axis=-1, keepdims=True))
                e = jnp.exp2(s - mn)
                l = l * jnp.exp2(mx - mn) + jnp.sum(e, axis=-1, keepdims=True)
                e_ref[mi, rows, e0:e0 + n] = e
                maxes.append(mn)
                mx = mn
            inv = 1.0 / l
            coef.append([jnp.exp2(mc - mx) * inv for mc in maxes])
        acc = jnp.zeros((rs, 2 * Dh), F32)
        for ci, (_, v_ref, r0, n, e0) in enumerate(chunks):
            p = e_ref[0, rows, e0:e0 + n] * coef[0][ci] - e_ref[1, rows, e0:e0 + n] * (lam * coef[1][ci])
            acc = acc + _dot(p.astype(BF16), v_ref[r0:r0 + n, :])
        ms = jnp.mean(acc * acc, axis=-1, keepdims=True)
        y = acc * lax.rsqrt(ms + NORM_EPS) * sub_ref[...] * (1.0 - lambda_init)
        o_ref[rows, :] = y.astype(BF16)


def _diff_attention(qkv, lams, subln, lambda_init, main, o_prev, n_out_rows, cfg):
    B, S, L, D, Dh = cfg.B, cfg.S, cfg.L, cfg.D, cfg.head_dim
    H = D // (2 * Dh)
    HW = 2 * Dh
    cblk = cfg.MX // L
    bq = DIFF_QUERY_ROWS if main else L
    nq = S // bq if main else 1
    nkeys = S + L if main else L
    qrow = (lambda b, i: b * nq + i) if main else (lambda b, i: cblk + b)
    vec = pl.BlockSpec((1, Dh), lambda b, h, i: (0, 0))
    in_specs = [vec, vec, vec, vec,
                pl.BlockSpec((1, HW), lambda b, h, i: (0, 0)),
                pl.BlockSpec((bq, HW), lambda b, h, i: (qrow(b, i), h))]
    args = [v.reshape(1, Dh).astype(F32) for v in lams] + [subln.reshape(1, HW).astype(F32), qkv]
    if main:
        in_specs += [pl.BlockSpec((S, HW), lambda b, h, i: (b, H + h)),
                     pl.BlockSpec((S, HW), lambda b, h, i: (b, 2 * H + h))]
        args += [qkv, qkv]
    in_specs += [pl.BlockSpec((L, HW), lambda b, h, i: (cblk + b, H + h)),
                 pl.BlockSpec((L, HW), lambda b, h, i: (cblk + b, 2 * H + h))]
    args += [qkv, qkv]
    aliases = {}
    if o_prev is not None:
        in_specs.append(pl.BlockSpec(memory_space=pl.ANY))
        aliases = {len(args): 0}
        args.append(o_prev)
    return pl.pallas_call(
        functools.partial(_diff_kernel, has_x=main, has_alias=o_prev is not None,
                          chunk=math.gcd(S, DIFF_KEY_CHUNK), row_split=DIFF_ROW_SPLIT, S=S, L=L,
                          lambda_init=lambda_init, Dh=Dh),
        out_shape=jax.ShapeDtypeStruct((n_out_rows, D), BF16),
        grid=(B, H, nq),
        in_specs=in_specs,
        out_specs=pl.BlockSpec((bq, HW), lambda b, h, i: (qrow(b, i), h)),
        scratch_shapes=[pltpu.VMEM((2, bq, nkeys), F32)],
        input_output_aliases=aliases,
        compiler_params=_cparams(3),
        name="diff_attention" if main else "diff_context_attention",
    )(*args)


def _rope_tables(cfg):
    Dh = cfg.head_dim
    t = jnp.arange(cfg.S, dtype=jnp.int32)
    row = (t // cfg.grid_w).astype(F32)
    col = (t % cfg.grid_w).astype(F32)
    axis_dim = Dh // 2
    inv_freq = ROPE_BASE ** (-jnp.arange(0, axis_dim, 2, dtype=F32) / axis_dim)
    ang_r = row[:, None] * inv_freq[None, :]
    ang_c = col[:, None] * inv_freq[None, :]
    ang = jnp.concatenate([ang_r, ang_r, ang_c, ang_c], axis=-1)
    cos, sin = jnp.cos(ang), jnp.sin(ang)
    quarter = jnp.arange(Dh) // (Dh // 4)
    sin_a = jnp.where(quarter % 2 == 0, -sin, 0.0)
    sin_b = jnp.where(quarter % 2 == 1, sin, 0.0)
    n_ctx = cfg.B * cfg.L

    def full(tab, fill):
        return jnp.concatenate([jnp.tile(tab, (cfg.B, 1)), jnp.full((n_ctx, Dh), fill, F32)], axis=0)

    return full(cos, 1.0), full(sin_a, 0.0), full(sin_b, 0.0)


def _head_gains(q_gain, k_gain, n_q, n_k, n_v_cols, head_dim):
    q_scale = head_dim ** -0.5 * LOG2E
    return jnp.concatenate([jnp.tile(q_gain.astype(F32) * q_scale, n_q), jnp.tile(k_gain.astype(F32), n_k),
                            jnp.ones((n_v_cols,), F32)]).reshape(1, -1)


def _forward(cfg, x, c, ctx, c_ctx, ada_down, ada_up, ada_bias, norm_mix, norm_ffn,
             na_w_qkv, na_w_o, na_q_norm, na_k_norm, na_rpb,
             diff_w_qkv, diff_w_o, diff_q_norm, diff_k_norm,
             diff_lambda_q1, diff_lambda_k1, diff_lambda_q2, diff_lambda_k2, diff_subln,
             swa_w_qkv, swa_w_o, swa_q_norm, swa_k_norm, swa_sink,
             ffn_w_gate, ffn_w_up, ffn_w_down,
             moe_router, moe_router_bias, moe_w_gate, moe_w_up, moe_w_down):
    B, S, L, D, Dh = cfg.B, cfg.S, cfg.L, cfg.D, cfg.head_dim
    M, MX = cfg.M, cfg.MX
    heads = D // Dh
    h = jnp.concatenate([x.reshape(MX, D), ctx.reshape(B * L, D)], axis=0).astype(F32)
    cond = jnp.zeros((MOD_ROWS, D), F32).at[:B].set(c).at[B].set(c_ctx)
    mods_all = _ada_all(cond, ada_down, ada_up, ada_bias)
    rope_tabs = _rope_tables(cfg)

    for i in range(cfg.depth):
        ctx_out = i < cfg.depth - 1
        n_rows = M if ctx_out else MX
        mods = mods_all[i]
        xn = _modulate(h, norm_mix[i], mods, 0, 1, M, cfg)
        kind, j = i % 3, i // 3
        if kind == 0:
            gains = _head_gains(na_q_norm[j], na_k_norm[j], heads, heads, D, Dh)
            qkv = _qkv_proj(xn, na_w_qkv, j, gains, 2 * D, None, cfg)
            o = _na_attention(qkv, na_rpb[j], n_rows, cfg)
            if ctx_out:
                o = _gqa_attention(qkv, None, heads, heads, False, o, n_rows, cfg)
            w_o = na_w_o
        elif kind == 1:
            gains = _head_gains(diff_q_norm[j], diff_k_norm[j], heads, heads, D, Dh)
            qkv = _qkv_proj(xn, diff_w_qkv, j, gains, 2 * D, rope_tabs, cfg)
            lams = (diff_lambda_q1[j], diff_lambda_k1[j], diff_lambda_q2[j], diff_lambda_k2[j])
            lambda_init = 0.8 - 0.6 * math.exp(-0.3 * i)
            o = _diff_attention(qkv, lams, diff_subln[j], lambda_init, True, None, n_rows, cfg)
            if ctx_out:
                o = _diff_attention(qkv, lams, diff_subln[j], lambda_init, False, o, n_rows, cfg)
            w_o = diff_w_o
        else:
            kvh = heads // cfg.swa_group
            gains = _head_gains(swa_q_norm[j], swa_k_norm[j], heads, kvh, kvh * Dh, Dh)
            qkv = _qkv_proj(xn, swa_w_qkv, j, gains, (heads + kvh) * Dh, rope_tabs, cfg)
            o = _gqa_attention(qkv, swa_sink[j], heads, kvh, True, None, n_rows, cfg)
            if ctx_out:
                o = _gqa_attention(qkv, swa_sink[j], heads, kvh, False, o, n_rows, cfg)
            w_o = swa_w_o
        h = _mm_res(o, w_o, j, h, mods, 2, n_rows, cfg)
        t = _modulate(h, norm_ffn[i], mods, 3, 4, n_rows, cfg)
        fj = i // 2
        if i % 2 == 0:
            n_ffn, _, d_ff = ffn_w_gate.shape
            hid = _swiglu_up(t, ffn_w_gate.reshape(n_ffn, 1, D, d_ff), ffn_w_up.reshape(n_ffn, 1, D, d_ff), fj, None)
            w_down = ffn_w_down
        else:
            gates = _router(t, moe_router[fj], moe_router_bias[fj], cfg)
            hid = _swiglu_up(t, moe_w_gate, moe_w_up, fj, gates)
            w_down = moe_w_down.reshape(moe_w_down.shape[0], -1, D)
        h = _mm_res(hid, w_down, fj, h, mods, 5, n_rows, cfg)
    return h.reshape(B, S, D)


_CFG = Cfg(B=2, S=4096, L=256, D=4096, depth=4, grid_w=64, head_dim=128, na_win_h=8, na_win_w=16,
           swa_window=128, swa_block=128, swa_group=4, n_experts=8, top_k=2)


def kernel(x, c, ctx, c_ctx, ada_down, ada_up, ada_bias, norm_mix, norm_ffn, na_w_qkv, na_w_o, na_q_norm, na_k_norm, na_rpb, diff_w_qkv, diff_w_o, diff_q_norm, diff_k_norm, diff_lambda_q1, diff_lambda_k1, diff_lambda_q2, diff_lambda_k2, diff_subln, swa_w_qkv, swa_w_o, swa_q_norm, swa_k_norm, swa_sink, ffn_w_gate, ffn_w_up, ffn_w_down, moe_router, moe_router_bias, moe_w_gate, moe_w_up, moe_w_down):
    return _forward(_CFG, x, c, ctx, c_ctx, ada_down, ada_up, ada_bias, norm_mix, norm_ffn, na_w_qkv, na_w_o, na_q_norm, na_k_norm, na_rpb, diff_w_qkv, diff_w_o, diff_q_norm, diff_k_norm, diff_lambda_q1, diff_lambda_k1, diff_lambda_q2, diff_lambda_k2, diff_subln, swa_w_qkv, swa_w_o, swa_q_norm, swa_k_norm, swa_sink, ffn_w_gate, ffn_w_up, ffn_w_down, moe_router, moe_router_bias, moe_w_gate, moe_w_up, moe_w_down)
```

```python
import functools
import math
from typing import NamedTuple

import jax
import jax.numpy as jnp
from jax import lax
from jax.experimental import pallas as pl
from jax.experimental.pallas import tpu as pltpu

F32 = jnp.float32
BF16 = jnp.bfloat16

LANE = 128
SUBLANE_BF16 = 16
MOD_ROWS = 8
VMEM_LIMIT = 58 * 1024 * 1024
MAX_FULL_K = 4096
K_TILE = 4096
RES_ROW_PARTS = 4
RES_BLOCK_N = 1024
RES_BLOCK_M = 576
QKV_BLOCK_N = 1024
QKV_ROW_SPLIT = 4
SWIGLU_BLOCK_N = 512
SWIGLU_ROW_SPLIT = 2

NA_GROUP_UNROLL = 8
GQA_QUERY_ROWS = 512
DIFF_QUERY_ROWS = 512
DIFF_KEY_CHUNK = 1024
DIFF_ROW_SPLIT = 4

NORM_EPS = 1e-6
NEG_INF = -1e30
ROPE_BASE = 10000.0
LOG2E = 1.4426950408889634


class Cfg(NamedTuple):
    B: int
    S: int
    L: int
    D: int
    depth: int
    grid_w: int
    head_dim: int
    na_win_h: int
    na_win_w: int
    swa_window: int
    swa_block: int
    swa_group: int
    n_experts: int
    top_k: int

    @property
    def M(self):
        return self.B * (self.S + self.L)

    @property
    def MX(self):
        return self.B * self.S


def _cparams(n_axes):
    return pltpu.CompilerParams(dimension_semantics=("arbitrary",) * n_axes,
                                vmem_limit_bytes=VMEM_LIMIT)


def _pick_block(n, cap, mult):
    best = None
    for d in range(mult, min(n, cap) + 1, mult):
        if n % d == 0:
            best = d
    assert best is not None, (n, cap, mult)
    return best


def _pick_row_block(rows, cap, bk):
    best = None
    for d in range(SUBLANE_BF16, min(rows, cap) + 1, SUBLANE_BF16):
        if rows % d == 0 and bk % (rows // d) == 0 and (bk // (rows // d)) % SUBLANE_BF16 == 0:
            best = d
    assert best is not None, (rows, cap, bk)
    return best


def _dot(a, b):
    return jnp.dot(a, b, preferred_element_type=F32)


def _dot_t(a, b):
    return lax.dot_general(a, b, (((1,), (1,)), ((), ())), preferred_element_type=F32)


def _row_select(vecs_ref, row0, bm, cfg):
    rows = row0 + lax.broadcasted_iota(jnp.int32, (bm, 1), 0)
    g = vecs_ref[cfg.B:cfg.B + 1, :]
    for b in reversed(range(cfg.B)):
        g = jnp.where(rows < (b + 1) * cfg.S, vecs_ref[b:b + 1, :], g)
    return g


def _ada_kernel(c_ref, down_ref, up_ref, b_ref, o_ref):
    c = c_ref[...]
    sc = (c * jax.nn.sigmoid(c)).astype(BF16)
    t = _dot(sc, down_ref[...].astype(BF16))
    o_ref[...] = _dot(t.astype(BF16), up_ref[...].astype(BF16)) + b_ref[...]


def _ada_all(cond, ada_down, ada_up, ada_bias):
    depth, D, R = ada_down.shape
    N = ada_up.shape[2]
    bn = _pick_block(N, 3072, LANE)
    return pl.pallas_call(
        _ada_kernel,
        out_shape=jax.ShapeDtypeStruct((depth, MOD_ROWS, N), F32),
        grid=(depth, N // bn),
        in_specs=[
            pl.BlockSpec((MOD_ROWS, D), lambda l, j: (0, 0)),
            pl.BlockSpec((None, D, R), lambda l, j: (l, 0, 0)),
            pl.BlockSpec((None, R, bn), lambda l, j: (l, 0, j)),
            pl.BlockSpec((None, 1, bn), lambda l, j: (l, 0, j)),
        ],
        out_specs=pl.BlockSpec((None, MOD_ROWS, bn), lambda l, j: (l, 0, j)),
        compiler_params=_cparams(2),
        name="ada_modulation",
    )(cond, ada_down, ada_up, ada_bias.reshape(depth, 1, N))


def _modulate_kernel(x_ref, g_ref, sh_ref, sc_ref, o_ref, *, bm, cfg):
    row0 = pl.program_id(0) * bm
    t = jnp.where(row0 < cfg.MX, row0 // cfg.S, cfg.B)
    x = x_ref[...]
    ms = jnp.mean(x * x, axis=-1, keepdims=True)
    y = x * lax.rsqrt(ms + NORM_EPS) * g_ref[...]
    o_ref[...] = (y * (1 + sc_ref[pl.ds(t, 1), :]) + sh_ref[pl.ds(t, 1), :]).astype(BF16)


def _modulate(h, gain, mods, shift_chunk, scale_chunk, n_rows, cfg):
    D = cfg.D
    bm = _pick_block(math.gcd(cfg.S, cfg.L), 256, SUBLANE_BF16)
    return pl.pallas_call(
        functools.partial(_modulate_kernel, bm=bm, cfg=cfg),
        out_shape=jax.ShapeDtypeStruct((n_rows, D), BF16),
        grid=(n_rows // bm,),
        in_specs=[
            pl.BlockSpec((bm, D), lambda i: (i, 0)),
            pl.BlockSpec((1, D), lambda i: (0, 0)),
            pl.BlockSpec((MOD_ROWS, D), lambda i: (0, shift_chunk)),
            pl.BlockSpec((MOD_ROWS, D), lambda i: (0, scale_chunk)),
        ],
        out_specs=pl.BlockSpec((bm, D), lambda i: (i, 0)),
        compiler_params=_cparams(1),
        name="modulate",
    )(h, gain.reshape(1, D), mods, mods)


def _stage_weight(w_ref, wb_ref, t, m, n_blocks):
    ck = w_ref.shape[0]

    @pl.when(t < n_blocks)
    def _():
        wb_ref[t % 2, pl.ds(pl.multiple_of(m * ck, ck), ck), :] = w_ref[...].astype(BF16)


def _qkv_kernel(*refs, n_blocks, n_norm_blocks, rope, bn):
    if rope:
        x_ref, w_ref, g_ref, cos_ref, sa_ref, sb_ref, o_ref, wb_ref = refs
    else:
        x_ref, w_ref, g_ref, o_ref, wb_ref = refs
    t = pl.program_id(0)
    _stage_weight(w_ref, wb_ref, t, pl.program_id(1), n_blocks)

    bm = x_ref.shape[0]
    rs = bm // QKV_ROW_SPLIT

    @pl.when(jnp.logical_and(t > 0, t - 1 < n_norm_blocks))
    def _():
        w = wb_ref.at[(t - 1) % 2]
        for r in range(QKV_ROW_SPLIT):
            rows = slice(r * rs, (r + 1) * rs)
            acc = _dot(x_ref[rows, :], w[...])
            outs = []
            for hh in range(bn // LANE):
                sl = slice(hh * LANE, (hh + 1) * LANE)
                xh = acc[:, sl]
                ms = jnp.mean(xh * xh, axis=-1, keepdims=True)
                y = xh * lax.rsqrt(ms + NORM_EPS) * g_ref[:, sl]
                if rope:
                    y = (y * cos_ref[rows, :] + pltpu.roll(y, 3 * LANE // 4, 1) * sa_ref[rows, :]
                         + pltpu.roll(y, LANE // 4, 1) * sb_ref[rows, :])
                outs.append(y.astype(BF16))
            o_ref[rows, :] = jnp.concatenate(outs, axis=1)

    @pl.when(jnp.logical_and(t > 0, t - 1 >= n_norm_blocks))
    def _():
        o_ref[...] = _dot(x_ref[...], wb_ref[(t - 1) % 2]).astype(BF16)


def _qkv_proj(xn, w_all, layer, gains, n_norm_cols, rope_tabs, cfg):
    M, D = xn.shape
    N = w_all.shape[2]
    bn = math.gcd(math.gcd(N, n_norm_cols), QKV_BLOCK_N)
    bm = _pick_block(M, 1152, SUBLANE_BF16)
    n_m, n_blocks = M // bm, N // bn
    ck = D // n_m
    assert bn % LANE == 0 and D % n_m == 0 and ck % SUBLANE_BF16 == 0
    rope = rope_tabs is not None
    row = lambda t, m: jnp.where(t == 0, 0, m)
    col = lambda t: jnp.maximum(t - 1, 0)
    in_specs = [
        pl.BlockSpec((bm, D), lambda t, m: (row(t, m), 0)),
        pl.BlockSpec((None, ck, bn), lambda t, m: (layer, m, jnp.minimum(t, n_blocks - 1))),
        pl.BlockSpec((1, bn), lambda t, m: (0, col(t))),
    ]
    args = [xn, w_all, gains]
    if rope:
        in_specs += [pl.BlockSpec((bm, LANE), lambda t, m: (row(t, m), 0))] * 3
        args += list(rope_tabs)
    return pl.pallas_call(
        functools.partial(_qkv_kernel, n_blocks=n_blocks, n_norm_blocks=n_norm_cols // bn, rope=rope, bn=bn),
        out_shape=jax.ShapeDtypeStruct((M, N), BF16),
        grid=(n_blocks + 1, n_m),
        in_specs=in_specs,
        out_specs=pl.BlockSpec((bm, bn), lambda t, m: (row(t, m), col(t))),
        scratch_shapes=[pltpu.VMEM((2, D, bn), BF16)],
        compiler_params=_cparams(2),
        name="qkv_proj",
    )(*args)


def _mm_res_kernel(a_ref, w_ref, res_ref, gate_ref, o_ref, wb_ref, *acc, n_blocks, nk, n_m, bm, cfg):
    part = pl.program_id(0)
    t = pl.program_id(1)
    m = pl.program_id(2)
    _stage_weight(w_ref, wb_ref, t, m, n_blocks)

    def epilogue(total):
        row0 = (part * n_m + m) * bm
        o_ref[...] = res_ref[...] + _row_select(gate_ref, row0, bm, cfg) * total

    @pl.when(t > 0)
    def _():
        w = wb_ref.at[(t - 1) % 2]
        if nk == 1:
            epilogue(_dot(a_ref[...], w[...]))
        else:
            k = (t - 1) % nk
            acc_ref = acc[0]
            rows = pl.ds(pl.multiple_of(m * bm, 8), bm)

            @pl.when(k == 0)
            def _():
                acc_ref[rows, :] = _dot(a_ref[...], w[...])

            @pl.when(jnp.logical_and(k > 0, k < nk - 1))
            def _():
                acc_ref[rows, :] += _dot(a_ref[...], w[...])

            @pl.when(k == nk - 1)
            def _():
                epilogue(acc_ref[rows, :] + _dot(a_ref[...], w[...]))


def _mm_res(a, w_all, layer, res, mods, gate_chunk, n_rows, cfg):
    _, K, N = w_all.shape
    bn = math.gcd(N, RES_BLOCK_N)
    bk = K if K <= MAX_FULL_K else _pick_block(K, K_TILE, LANE)
    nk = K // bk
    n_parts = RES_ROW_PARTS if nk > 1 else 1
    bm = _pick_row_block(n_rows // n_parts, RES_BLOCK_M, bk)
    n_m = n_rows // n_parts // bm
    n_blocks = (N // bn) * nk
    ck = bk // n_m
    assert n_rows % n_parts == 0 and bk % n_m == 0 and ck % SUBLANE_BF16 == 0
    wblk = lambda t: jnp.minimum(t, n_blocks - 1)
    cblk = lambda t: jnp.maximum(t - 1, 0)
    orow = lambda p, t, m: p * n_m + jnp.where(jnp.logical_and(t > 0, cblk(t) % nk == nk - 1), m, 0)
    scratch = [pltpu.VMEM((2, bk, bn), BF16)]
    if nk > 1:
        scratch.append(pltpu.VMEM((n_rows // n_parts, bn), F32))
    return pl.pallas_call(
        functools.partial(_mm_res_kernel, n_blocks=n_blocks, nk=nk, n_m=n_m, bm=bm, cfg=cfg),
        out_shape=jax.ShapeDtypeStruct((n_rows, N), F32),
        grid=(n_parts, n_blocks + 1, n_m),
        in_specs=[
            pl.BlockSpec((bm, bk), lambda p, t, m: (p * n_m + jnp.where(t == 0, 0, m), cblk(t) % nk)),
            pl.BlockSpec((None, ck, bn), lambda p, t, m: (layer, (wblk(t) % nk) * n_m + m, wblk(t) // nk)),
            pl.BlockSpec((bm, bn), lambda p, t, m: (orow(p, t, m), cblk(t) // nk)),
            pl.BlockSpec((MOD_ROWS, bn), lambda p, t, m: (0, gate_chunk * (N // bn) + cblk(t) // nk)),
        ],
        out_specs=pl.BlockSpec((bm, bn), lambda p, t, m: (orow(p, t, m), cblk(t) // nk)),
        scratch_shapes=scratch,
        compiler_params=_cparams(3),
        name="matmul_residual",
    )(a, w_all, res, mods)


def _swiglu_kernel(*refs, gated, n_blocks, nbe):
    if gated:
        x_ref, wg_ref, wu_ref, gates_ref, o_ref, wgb_ref, wub_ref = refs
    else:
        x_ref, wg_ref, wu_ref, o_ref, wgb_ref, wub_ref = refs
    t = pl.program_id(0)
    m = pl.program_id(1)
    _stage_weight(wg_ref, wgb_ref, t, m, n_blocks)
    _stage_weight(wu_ref, wub_ref, t, m, n_blocks)

    rs = x_ref.shape[0] // SWIGLU_ROW_SPLIT

    @pl.when(t > 0)
    def _():
        j = t - 1
        wg = wgb_ref.at[j % 2]
        wu = wub_ref.at[j % 2]
        for r in range(SWIGLU_ROW_SPLIT):
            rows = slice(r * rs, (r + 1) * rs)
            x = x_ref[rows, :]
            g = _dot(x, wg[...])
            u = _dot(x, wu[...])
            hid = g * jax.nn.sigmoid(g) * u
            if gated:
                gates = gates_ref[rows, :]
                lane = lax.broadcasted_iota(jnp.int32, gates.shape, 1)
                hid = hid * jnp.sum(jnp.where(lane == j // nbe, gates, 0.0), axis=-1, keepdims=True)
            o_ref[rows, :] = hid.astype(BF16)


def _swiglu_up(t, w_gate, w_up, layer, gates):
    M, D = t.shape
    _, E, _, F = w_gate.shape
    bn = math.gcd(F, SWIGLU_BLOCK_N)
    nbe = F // bn
    bm = _pick_block(M, 1152, SUBLANE_BF16)
    n_m, n_blocks = M // bm, E * nbe
    ck = D // n_m
    assert D % n_m == 0 and ck % SUBLANE_BF16 == 0
    gated = gates is not None
    row = lambda t, m: jnp.where(t == 0, 0, m)
    col = lambda t: jnp.maximum(t - 1, 0)
    wblk = lambda t: jnp.minimum(t, n_blocks - 1)
    w_spec = pl.BlockSpec((None, None, ck, bn), lambda t, m: (layer, wblk(t) // nbe, m, wblk(t) % nbe))
    in_specs = [pl.BlockSpec((bm, D), lambda t, m: (row(t, m), 0)), w_spec, w_spec]
    args = [t, w_gate, w_up]
    if gated:
        in_specs.append(pl.BlockSpec((bm, LANE), lambda t, m: (row(t, m), 0)))
        args.append(gates)
    return pl.pallas_call(
        functools.partial(_swiglu_kernel, gated=gated, n_blocks=n_blocks, nbe=nbe),
        out_shape=jax.ShapeDtypeStruct((M, E * F), BF16),
        grid=(n_blocks + 1, n_m),
        in_specs=in_specs,
        out_specs=pl.BlockSpec((bm, bn), lambda t, m: (row(t, m), col(t))),
        scratch_shapes=[pltpu.VMEM((2, D, bn), BF16), pltpu.VMEM((2, D, bn), BF16)],
        compiler_params=_cparams(2),
        name="swiglu_up",
    )(*args)


def _router_kernel(t_ref, w_ref, b_ref, o_ref):
    logits = _dot(t_ref[...], w_ref[...].astype(BF16)) + b_ref[...]
    lane = lax.broadcasted_iota(jnp.int32, logits.shape, 1).astype(F32)
    top1 = jnp.max(logits, axis=-1, keepdims=True)
    i1 = jnp.min(jnp.where(logits == top1, lane, float(LANE)), axis=-1, keepdims=True)
    rest = jnp.where(lane == i1, -jnp.inf, logits)
    top2 = jnp.max(rest, axis=-1, keepdims=True)
    i2 = jnp.min(jnp.where(rest == top2, lane, float(LANE)), axis=-1, keepdims=True)
    e2 = jnp.exp(top2 - top1)
    inv = 1.0 / (1.0 + e2)
    o_ref[...] = jnp.where(lane == i1, inv, 0.0) + jnp.where(lane == i2, e2 * inv, 0.0)


def _router(t, router_w, router_b, cfg):
    M, D = t.shape
    E = cfg.n_experts
    assert cfg.top_k == 2 and E <= LANE
    w = jnp.zeros((D, LANE), F32).at[:, :E].set(router_w)
    b = jnp.full((1, LANE), NEG_INF, F32).at[0, :E].set(router_b)
    bm = _pick_block(M, 1152, SUBLANE_BF16)
    return pl.pallas_call(
        _router_kernel,
        out_shape=jax.ShapeDtypeStruct((M, LANE), F32),
        grid=(M // bm,),
        in_specs=[
            pl.BlockSpec((bm, D), lambda m: (m, 0)),
            pl.BlockSpec((D, LANE), lambda m: (0, 0)),
            pl.BlockSpec((1, LANE), lambda m: (0, 0)),
        ],
        out_specs=pl.BlockSpec((bm, LANE), lambda m: (m, 0)),
        compiler_params=_cparams(1),
        name="moe_router",
    )(t, w, b)


def _na_variant_groups(n_groups):
    return (0, 1 if n_groups > 2 else 0, n_groups - 1)


def _na_build_bias(tp_ref, bias_ref, *, rows, W, kh):
    n_groups = rows // kh
    lane = lax.broadcasted_iota(jnp.int32, (W, 2 * W), 1)
    neg = jnp.full((W, 2 * W), NEG_INF, F32)
    for variant, g in enumerate(_na_variant_groups(n_groups)):
        key_row = min(max(g * kh - kh // 2, 0), rows - 2 * kh)
        for ql in range(kh):
            qi = g * kh + ql
            r0 = min(max(qi - kh // 2, 0), rows - kh)
            for j in range(kh):
                ki = key_row + 2 * j
                first_ok = r0 <= ki < r0 + kh
                second_ok = r0 <= ki + 1 < r0 + kh
                if not (first_ok or second_ok):
                    blk = neg
                else:
                    blk = tp_ref[ki - qi + kh]
                    if not second_ok:
                        blk = jnp.where(lane < W, blk, NEG_INF)
                    elif not first_ok:
                        blk = jnp.where(lane >= W, blk, NEG_INF)
                bias_ref[variant, ql * W:(ql + 1) * W, j * 2 * W:(j + 1) * 2 * W] = blk


def _na_kernel(q_ref, k_ref, v_ref, ck_ref, cv_ref, tp_ref, o_ref, bias_ref, *, rows, W, kh, unroll):
    @pl.when(pl.program_id(1) == 0)
    def _():
        _na_build_bias(tp_ref, bias_ref, rows=rows, W=W, kh=kh)

    ck = ck_ref[...]
    cv = cv_ref[...]
    n_groups = rows // kh
    gq, gk = kh * W, 2 * kh * W

    def one_group(g):
        variant = jnp.where(g == 0, 0, jnp.where(g == n_groups - 1, 2, 1))
        key_row = jnp.clip(g * kh - kh // 2, 0, rows - 2 * kh)
        q = q_ref[pl.ds(pl.multiple_of(g * gq, gq), gq), :]
        kstart = pl.multiple_of(key_row * W, W)
        s = _dot_t(q, k_ref[pl.ds(kstart, gk), :]) + bias_ref[variant]
        sc = _dot_t(q, ck)
        m = jnp.maximum(jnp.max(s, axis=-1, keepdims=True), jnp.max(sc, axis=-1, keepdims=True))
        e = jnp.exp2(s - m)
        ec = jnp.exp2(sc - m)
        inv = 1.0 / (jnp.sum(e, axis=-1, keepdims=True) + jnp.sum(ec, axis=-1, keepdims=True))
        o = (_dot(e.astype(BF16), v_ref[pl.ds(kstart, gk), :]) + _dot(ec.astype(BF16), cv)) * inv
        return o.astype(BF16)

    def body(it, carry):
        outs = [one_group(it * unroll + u) for u in range(unroll)]
        start = pl.multiple_of(it * (unroll * gq), unroll * gq)
        o_ref[pl.ds(start, unroll * gq), :] = jnp.concatenate(outs, axis=0)
        return carry

    lax.fori_loop(0, n_groups // unroll, body, 0)


def _na_bias_blocks(rpb, cfg):
    W, kw, wh = cfg.grid_w, cfg.na_win_w, cfg.na_win_h
    H = rpb.shape[0]
    wq = jnp.arange(W)[:, None]
    kc = jnp.arange(W)[None, :]
    c0 = jnp.clip(wq - kw // 2, 0, W - kw)
    col_valid = (kc >= c0) & (kc < c0 + kw)
    cidx = jnp.clip(kc - wq + (kw - 1), 0, 2 * kw - 2)
    toe = jnp.where(col_valid[None, None], rpb[:, :, cidx], NEG_INF).astype(F32)
    pad = jnp.full((H, 1, W, W), NEG_INF, F32)
    toe = jnp.concatenate([pad, toe, pad], axis=1)
    return jnp.concatenate([toe[:, :-1], toe[:, 1:]], axis=-1) * LOG2E


def _na_attention(qkv, rpb, n_out_rows, cfg):
    B, S, L, D, W, Dh = cfg.B, cfg.S, cfg.L, cfg.D, cfg.grid_w, cfg.head_dim
    H = D // Dh
    rows = S // W
    kh = min(cfg.na_win_h, rows)
    n_groups = rows // kh
    assert rows % kh == 0 and n_groups >= 2 and kh == cfg.na_win_h and 2 * W == LANE
    blocks = _na_bias_blocks(rpb, cfg)
    cblk = cfg.MX // L
    return pl.pallas_call(
        functools.partial(_na_kernel, rows=rows, W=W, kh=kh, unroll=math.gcd(rows // kh, NA_GROUP_UNROLL)),
        out_shape=jax.ShapeDtypeStruct((n_out_rows, D), BF16),
        grid=(H, B),
        in_specs=[
            pl.BlockSpec((S, Dh), lambda h, b: (b, h)),
            pl.BlockSpec((S, Dh), lambda h, b: (b, H + h)),
            pl.BlockSpec((S, Dh), lambda h, b: (b, 2 * H + h)),
            pl.BlockSpec((L, Dh), lambda h, b: (cblk + b, H + h)),
            pl.BlockSpec((L, Dh), lambda h, b: (cblk + b, 2 * H + h)),
            pl.BlockSpec((None, 2 * kh, W, 2 * W), lambda h, b: (h, 0, 0, 0)),
        ],
        out_specs=pl.BlockSpec((S, Dh), lambda h, b: (b, h)),
        scratch_shapes=[pltpu.VMEM((3, kh * W, 2 * kh * W), F32)],
        compiler_params=_cparams(2),
        name="neighborhood_attention",
    )(qkv, qkv, qkv, qkv, qkv, blocks)


def _gqa_kernel(*refs, G, has_local, has_sink, has_alias, win, S, Dh):
    refs = list(refs)
    sink_ref = refs.pop(0) if has_sink else None
    q_ref = refs.pop(0)
    k_ref, v_ref, mask_ref = (refs.pop(0), refs.pop(0), refs.pop(0)) if has_local else (None, None, None)
    ck_ref, cv_ref = refs.pop(0), refs.pop(0)
    if has_alias:
        refs.pop(0)
    o_ref = refs.pop(0)
    hk = pl.program_id(1)
    rows = q_ref.shape[0]
    ck = ck_ref[...]
    cv = cv_ref[...]
    if has_local:
        nkw = rows + 2 * win
        start = pl.multiple_of(jnp.clip(pl.program_id(2) * rows - win, 0, S - nkw), win)
        kw = k_ref[pl.ds(start, nkw), :]
        vw = v_ref[pl.ds(start, nkw), :]
    outs = []
    for g in range(G):
        q = q_ref[:, g * Dh:(g + 1) * Dh]
        sc = _dot_t(q, ck)
        m = jnp.max(sc, axis=-1, keepdims=True)
        if has_local:
            s = _dot_t(q, kw) + mask_ref[...]
            m = jnp.maximum(m, jnp.max(s, axis=-1, keepdims=True))
        if has_sink:
            sk = sink_ref[hk, g] * LOG2E
            m = jnp.maximum(m, sk)
        ec = jnp.exp2(sc - m)
        l = jnp.sum(ec, axis=-1, keepdims=True)
        o = _dot(ec.astype(BF16), cv)
        if has_local:
            e = jnp.exp2(s - m)
            l = l + jnp.sum(e, axis=-1, keepdims=True)
            o = o + _dot(e.astype(BF16), vw)
        if has_sink:
            l = l + jnp.exp2(sk - m)
        outs.append((o * (1.0 / l)).astype(BF16))
    o_ref[...] = jnp.concatenate(outs, axis=1)


def _band_mask(rows, win, S):
    nkw = rows + 2 * win
    r = jnp.arange(rows)[:, None]
    c = jnp.arange(nkw)[None, :]
    offsets = (0, -win, -2 * win)
    return jnp.stack([jnp.where(jnp.abs(off + c - r) <= win, 0.0, NEG_INF) for off in offsets]).astype(F32)


def _gqa_attention(qkv, sink, q_heads, kv_heads, local, o_prev, n_out_rows, cfg):
    B, S, L, D, Dh = cfg.B, cfg.S, cfg.L, cfg.D, cfg.head_dim
    G = q_heads // kv_heads
    kcol = q_heads
    vcol = q_heads + kv_heads
    cblk = cfg.MX // L
    has_sink = sink is not None
    win = cfg.swa_window
    rows_step = GQA_QUERY_ROWS if local else L
    nq = S // rows_step if local else 1
    qrow = (lambda b, i: b * nq + i) if local else (lambda b, i: cblk + b)
    in_specs, args = [], []
    if has_sink:
        in_specs.append(pl.BlockSpec(memory_space=pltpu.SMEM))
        args.append(sink.reshape(kv_heads, G).astype(F32))
    in_specs.append(pl.BlockSpec((rows_step, G * Dh), lambda b, h, i: (qrow(b, i), h)))
    args.append(qkv)
    if local:
        assert cfg.swa_block == win and S % rows_step == 0 and nq >= 2 and win % SUBLANE_BF16 == 0
        nkw = rows_step + 2 * win
        variant = lambda i: jnp.where(i == 0, 0, jnp.where(i == nq - 1, 2, 1))
        in_specs += [pl.BlockSpec((S, Dh), lambda b, h, i: (b, kcol + h)),
                     pl.BlockSpec((S, Dh), lambda b, h, i: (b, vcol + h)),
                     pl.BlockSpec((None, rows_step, nkw), lambda b, h, i: (variant(i), 0, 0))]
        args += [qkv, qkv, _band_mask(rows_step, win, S)]
    in_specs += [pl.BlockSpec((L, Dh), lambda b, h, i: (cblk + b, kcol + h)),
                 pl.BlockSpec((L, Dh), lambda b, h, i: (cblk + b, vcol + h))]
    args += [qkv, qkv]
    aliases = {}
    if o_prev is not None:
        in_specs.append(pl.BlockSpec(memory_space=pl.ANY))
        aliases = {len(args): 0}
        args.append(o_prev)
    return pl.pallas_call(
        functools.partial(_gqa_kernel, G=G, has_local=local, has_sink=has_sink,
                          has_alias=o_prev is not None, win=win, S=S, Dh=Dh),
        out_shape=jax.ShapeDtypeStruct((n_out_rows, D), BF16),
        grid=(B, kv_heads, nq),
        in_specs=in_specs,
        out_specs=pl.BlockSpec((rows_step, G * Dh), lambda b, h, i: (qrow(b, i), h)),
        input_output_aliases=aliases,
        compiler_params=_cparams(3),
        name="gqa_window_attention" if local else "gqa_context_attention",
    )(*args)


def _diff_kernel(*refs, has_x, has_alias, chunk, row_split, S, L, lambda_init, Dh):
    refs = list(refs)
    lq1, lk1, lq2, lk2, sub_ref, q_ref = refs[:6]
    refs = refs[6:]
    kx_ref, vx_ref = (refs.pop(0), refs.pop(0)) if has_x else (None, None)
    kc_ref, vc_ref = refs.pop(0), refs.pop(0)
    if has_alias:
        refs.pop(0)
    o_ref, e_ref = refs
    lam = (jnp.exp(jnp.sum(lq1[...] * lk1[...], axis=-1, keepdims=True))
           - jnp.exp(jnp.sum(lq2[...] * lk2[...], axis=-1, keepdims=True)) + lambda_init)
    chunks = []
    if has_x:
        chunks += [(kx_ref, vx_ref, c * chunk, chunk, c * chunk) for c in range(S // chunk)]
    off = S if has_x else 0
    chunks.append((kc_ref, vc_ref, 0, L, off))
    rs = q_ref.shape[0] // row_split
    for r in range(row_split):
        rows = slice(r * rs, (r + 1) * rs)
        coef = []
        for mi in range(2):
            sl = slice(mi * Dh, (mi + 1) * Dh)
            q = q_ref[rows, sl]
            mx = jnp.full((rs, 1), -jnp.inf, F32)
            l = jnp.zeros((rs, 1), F32)
            maxes = []
            for k_ref, _, r0, n, e0 in chunks:
                s = _dot_t(q, k_ref[r0:r0 + n, sl])
                mn = jnp.maximum(mx, jnp.max(s, axis=-1, keepdims=True))
                e = jnp.exp2(s - mn)
                l = l * jnp.exp2(mx - mn) + jnp.sum(e, axis=-1, keepdims=True)
                e_ref[mi, rows, e0:e0 + n] = e
                maxes.append(mn)
                mx = mn
            inv = 1.0 / l
            coef.append([jnp.exp2(mc - mx) * inv for mc in maxes])
        acc = jnp.zeros((rs, 2 * Dh), F32)
        for ci, (_, v_ref, r0, n, e0) in enumerate(chunks):
            p = e_ref[0, rows, e0:e0 + n] * coef[0][ci] - e_ref[1, rows, e0:e0 + n] * (lam * coef[1][ci])
            acc = acc + _dot(p.astype(BF16), v_ref[r0:r0 + n, :])
        ms = jnp.mean(acc * acc, axis=-1, keepdims=True)
        y = acc * lax.rsqrt(ms + NORM_EPS) * sub_ref[...] * (1.0 - lambda_init)
        o_ref[rows, :] = y.astype(BF16)


def _diff_attention(qkv, lams, subln, lambda_init, main, o_prev, n_out_rows, cfg):
    B, S, L, D, Dh = cfg.B, cfg.S, cfg.L, cfg.D, cfg.head_dim
    H = D // (2 * Dh)
    HW = 2 * Dh
    cblk = cfg.MX // L
    bq = DIFF_QUERY_ROWS if main else L
    nq = S // bq if main else 1
    nkeys = S + L if main else L
    qrow = (lambda b, i: b * nq + i) if main else (lambda b, i: cblk + b)
    vec = pl.BlockSpec((1, Dh), lambda b, h, i: (0, 0))
    in_specs = [vec, vec, vec, vec,
                pl.BlockSpec((1, HW), lambda b, h, i: (0, 0)),
                pl.BlockSpec((bq, HW), lambda b, h, i: (qrow(b, i), h))]
    args = [v.reshape(1, Dh).astype(F32) for v in lams] + [subln.reshape(1, HW).astype(F32), qkv]
    if main:
        in_specs += [pl.BlockSpec((S, HW), lambda b, h, i: (b, H + h)),
                     pl.BlockSpec((S, HW), lambda b, h, i: (b, 2 * H + h))]
        args += [qkv, qkv]
    in_specs += [pl.BlockSpec((L, HW), lambda b, h, i: (cblk + b, H + h)),
                 pl.BlockSpec((L, HW), lambda b, h, i: (cblk + b, 2 * H + h))]
    args += [qkv, qkv]
    aliases = {}
    if o_prev is not None:
        in_specs.append(pl.BlockSpec(memory_space=pl.ANY))
        aliases = {len(args): 0}
        args.append(o_prev)
    return pl.pallas_call(
        functools.partial(_diff_kernel, has_x=main, has_alias=o_prev is not None,
                          chunk=math.gcd(S, DIFF_KEY_CHUNK), row_split=DIFF_ROW_SPLIT, S=S, L=L,
                          lambda_init=lambda_init, Dh=Dh),
        out_shape=jax.ShapeDtypeStruct((n_out_rows, D), BF16),
        grid=(B, H, nq),
        in_specs=in_specs,
        out_specs=pl.BlockSpec((bq, HW), lambda b, h, i: (qrow(b, i), h)),
        scratch_shapes=[pltpu.VMEM((2, bq, nkeys), F32)],
        input_output_aliases=aliases,
        compiler_params=_cparams(3),
        name="diff_attention" if main else "diff_context_attention",
    )(*args)


def _rope_tables(cfg):
    Dh = cfg.head_dim
    t = jnp.arange(cfg.S, dtype=jnp.int32)
    row = (t // cfg.grid_w).astype(F32)
    col = (t % cfg.grid_w).astype(F32)
    axis_dim = Dh // 2
    inv_freq = ROPE_BASE ** (-jnp.arange(0, axis_dim, 2, dtype=F32) / axis_dim)
    ang_r = row[:, None] * inv_freq[None, :]
    ang_c = col[:, None] * inv_freq[None, :]
    ang = jnp.concatenate([ang_r, ang_r, ang_c, ang_c], axis=-1)
    cos, sin = jnp.cos(ang), jnp.sin(ang)
    quarter = jnp.arange(Dh) // (Dh // 4)
    sin_a = jnp.where(quarter % 2 == 0, -sin, 0.0)
    sin_b = jnp.where(quarter % 2 == 1, sin, 0.0)
    n_ctx = cfg.B * cfg.L

    def full(tab, fill):
        return jnp.concatenate([jnp.tile(tab, (cfg.B, 1)), jnp.full((n_ctx, Dh), fill, F32)], axis=0)

    return full(cos, 1.0), full(sin_a, 0.0), full(sin_b, 0.0)


def _head_gains(q_gain, k_gain, n_q, n_k, n_v_cols, head_dim):
    q_scale = head_dim ** -0.5 * LOG2E
    return jnp.concatenate([jnp.tile(q_gain.astype(F32) * q_scale, n_q), jnp.tile(k_gain.astype(F32), n_k),
                            jnp.ones((n_v_cols,), F32)]).reshape(1, -1)


def _forward(cfg, x, c, ctx, c_ctx, ada_down, ada_up, ada_bias, norm_mix, norm_ffn,
             na_w_qkv, na_w_o, na_q_norm, na_k_norm, na_rpb,
             diff_w_qkv, diff_w_o, diff_q_norm, diff_k_norm,
             diff_lambda_q1, diff_lambda_k1, diff_lambda_q2, diff_lambda_k2, diff_subln,
             swa_w_qkv, swa_w_o, swa_q_norm, swa_k_norm, swa_sink,
             ffn_w_gate, ffn_w_up, ffn_w_down,
             moe_router, moe_router_bias, moe_w_gate, moe_w_up, moe_w_down):
    B, S, L, D, Dh = cfg.B, cfg.S, cfg.L, cfg.D, cfg.head_dim
    M, MX = cfg.M, cfg.MX
    heads = D // Dh
    h = jnp.concatenate([x.reshape(MX, D), ctx.reshape(B * L, D)], axis=0).astype(F32)
    cond = jnp.zeros((MOD_ROWS, D), F32).at[:B].set(c).at[B].set(c_ctx)
    mods_all = _ada_all(cond, ada_down, ada_up, ada_bias)
    rope_tabs = _rope_tables(cfg)

    for i in range(cfg.depth):
        ctx_out = i < cfg.depth - 1
        n_rows = M if ctx_out else MX
        mods = mods_all[i]
        xn = _modulate(h, norm_mix[i], mods, 0, 1, M, cfg)
        kind, j = i % 3, i // 3
        if kind == 0:
            gains = _head_gains(na_q_norm[j], na_k_norm[j], heads, heads, D, Dh)
            qkv = _qkv_proj(xn, na_w_qkv, j, gains, 2 * D, None, cfg)
            o = _na_attention(qkv, na_rpb[j], n_rows, cfg)
            if ctx_out:
                o = _gqa_attention(qkv, None, heads, heads, False, o, n_rows, cfg)
            w_o = na_w_o
        elif kind == 1:
            gains = _head_gains(diff_q_norm[j], diff_k_norm[j], heads, heads, D, Dh)
            qkv = _qkv_proj(xn, diff_w_qkv, j, gains, 2 * D, rope_tabs, cfg)
            lams = (diff_lambda_q1[j], diff_lambda_k1[j], diff_lambda_q2[j], diff_lambda_k2[j])
            lambda_init = 0.8 - 0.6 * math.exp(-0.3 * i)
            o = _diff_attention(qkv, lams, diff_subln[j], lambda_init, True, None, n_rows, cfg)
            if ctx_out:
                o = _diff_attention(qkv, lams, diff_subln[j], lambda_init, False, o, n_rows, cfg)
            w_o = diff_w_o
        else:
            kvh = heads // cfg.swa_group
            gains = _head_gains(swa_q_norm[j], swa_k_norm[j], heads, kvh, kvh * Dh, Dh)
            qkv = _qkv_proj(xn, swa_w_qkv, j, gains, (heads + kvh) * Dh, rope_tabs, cfg)
            o = _gqa_attention(qkv, swa_sink[j], heads, kvh, True, None, n_rows, cfg)
            if ctx_out:
                o = _gqa_attention(qkv, swa_sink[j], heads, kvh, False, o, n_rows, cfg)
            w_o = swa_w_o
        h = _mm_res(o, w_o, j, h, mods, 2, n_rows, cfg)
        t = _modulate(h, norm_ffn[i], mods, 3, 4, n_rows, cfg)
        fj = i // 2
        if i % 2 == 0:
            n_ffn, _, d_ff = ffn_w_gate.shape
            hid = _swiglu_up(t, ffn_w_gate.reshape(n_ffn, 1, D, d_ff), ffn_w_up.reshape(n_ffn, 1, D, d_ff), fj, None)
            w_down = ffn_w_down
        else:
            gates = _router(t, moe_router[fj], moe_router_bias[fj], cfg)
            hid = _swiglu_up(t, moe_w_gate, moe_w_up, fj, gates)
            w_down = moe_w_down.reshape(moe_w_down.shape[0], -1, D)
        h = _mm_res(hid, w_down, fj, h, mods, 5, n_rows, cfg)
    return h.reshape(B, S, D)


_CFG = Cfg(B=2, S=4096, L=256, D=4096, depth=4, grid_w=64, head_dim=128, na_win_h=8, na_win_w=16,
           swa_window=128, swa_block=128, swa_group=4, n_experts=8, top_k=2)


def kernel(x, c, ctx, c_ctx, ada_down, ada_up, ada_bias, norm_mix, norm_ffn, na_w_qkv, na_w_o, na_q_norm, na_k_norm, na_rpb, diff_w_qkv, diff_w_o, diff_q_norm, diff_k_norm, diff_lambda_q1, diff_lambda_k1, diff_lambda_q2, diff_lambda_k2, diff_subln, swa_w_qkv, swa_w_o, swa_q_norm, swa_k_norm, swa_sink, ffn_w_gate, ffn_w_up, ffn_w_down, moe_router, moe_router_bias, moe_w_gate, moe_w_up, moe_w_down):
    return _forward(_CFG, x, c, ctx, c_ctx, ada_down, ada_up, ada_bias, norm_mix, norm_ffn, na_w_qkv, na_w_o, na_q_norm, na_k_norm, na_rpb, diff_w_qkv, diff_w_o, diff_q_norm, diff_k_norm, diff_lambda_q1, diff_lambda_k1, diff_lambda_q2, diff_lambda_k2, diff_subln, swa_w_qkv, swa_w_o, swa_q_norm, swa_k_norm, swa_sink, ffn_w_gate, ffn_w_up, ffn_w_down, moe_router, moe_router_bias, moe_w_gate, moe_w_up, moe_w_down)
```

```python
import functools
import math
from typing import NamedTuple

import jax
import jax.numpy as jnp
from jax import lax
from jax.experimental import pallas as pl
from jax.experimental.pallas import tpu as pltpu

F32 = jnp.float32
BF16 = jnp.bfloat16

LANE = 128
SUBLANE_BF16 = 16
MOD_ROWS = 8
VMEM_LIMIT = 58 * 1024 * 1024
MAX_FULL_K = 4096
K_TILE = 4096
RES_ROW_PARTS = 2
RES_BLOCK_N = 512
RES_BLOCK_M = 1152
QKV_BLOCK_N = 1024
QKV_ROW_SPLIT = 4
SWIGLU_BLOCK_N = 512
SWIGLU_ROW_SPLIT = 2

NA_GROUP_UNROLL = 8
MODULATE_ROWS = 512
GQA_QUERY_ROWS = 512
DIFF_QUERY_ROWS = 1024
DIFF_KEY_CHUNK = 1024
DIFF_ROW_SPLIT = 8

NORM_EPS = 1e-6
NEG_INF = -1e30
ROPE_BASE = 10000.0
LOG2E = 1.4426950408889634


class Cfg(NamedTuple):
    B: int
    S: int
    L: int
    D: int
    depth: int
    grid_w: int
    head_dim: int
    na_win_h: int
    na_win_w: int
    swa_window: int
    swa_block: int
    swa_group: int
    n_experts: int
    top_k: int

    @property
    def M(self):
        return self.B * (self.S + self.L)

    @property
    def MX(self):
        return self.B * self.S


def _cparams(n_axes):
    return pltpu.CompilerParams(dimension_semantics=("arbitrary",) * n_axes,
                                vmem_limit_bytes=VMEM_LIMIT)


def _pick_block(n, cap, mult):
    best = None
    for d in range(mult, min(n, cap) + 1, mult):
        if n % d == 0:
            best = d
    assert best is not None, (n, cap, mult)
    return best


def _pick_row_block(rows, cap, bk):
    best = None
    for d in range(SUBLANE_BF16, min(rows, cap) + 1, SUBLANE_BF16):
        if rows % d == 0 and bk % (rows // d) == 0 and (bk // (rows // d)) % SUBLANE_BF16 == 0:
            best = d
    assert best is not None, (rows, cap, bk)
    return best


def _dot(a, b):
    return jnp.dot(a, b, preferred_element_type=F32)


def _dot_t(a, b):
    return lax.dot_general(a, b, (((1,), (1,)), ((), ())), preferred_element_type=F32)


def _row_select(vecs_ref, row0, bm, cfg):
    rows = row0 + lax.broadcasted_iota(jnp.int32, (bm, 1), 0)
    g = vecs_ref[cfg.B:cfg.B + 1, :]
    for b in reversed(range(cfg.B)):
        g = jnp.where(rows < (b + 1) * cfg.S, vecs_ref[b:b + 1, :], g)
    return g


def _ada_kernel(c_ref, down_ref, up_ref, b_ref, o_ref):
    c = c_ref[...]
    sc = (c * jax.nn.sigmoid(c)).astype(BF16)
    t = _dot(sc, down_ref[...].astype(BF16))
    o_ref[...] = _dot(t.astype(BF16), up_ref[...].astype(BF16)) + b_ref[...]


def _ada_all(cond, ada_down, ada_up, ada_bias):
    depth, D, R = ada_down.shape
    N = ada_up.shape[2]
    bn = _pick_block(N, 3072, LANE)
    return pl.pallas_call(
        _ada_kernel,
        out_shape=jax.ShapeDtypeStruct((depth, MOD_ROWS, N), F32),
        grid=(depth, N // bn),
        in_specs=[
            pl.BlockSpec((MOD_ROWS, D), lambda l, j: (0, 0)),
            pl.BlockSpec((None, D, R), lambda l, j: (l, 0, 0)),
            pl.BlockSpec((None, R, bn), lambda l, j: (l, 0, j)),
            pl.BlockSpec((None, 1, bn), lambda l, j: (l, 0, j)),
        ],
        out_specs=pl.BlockSpec((None, MOD_ROWS, bn), lambda l, j: (l, 0, j)),
        compiler_params=_cparams(2),
        name="ada_modulation",
    )(cond, ada_down, ada_up, ada_bias.reshape(depth, 1, N))


def _modulate_kernel(x_ref, g_ref, sh_ref, sc_ref, o_ref, *, bm, cfg):
    row0 = pl.program_id(0) * bm
    t = jnp.where(row0 < cfg.MX, row0 // cfg.S, cfg.B)
    x = x_ref[...]
    ms = jnp.mean(x * x, axis=-1, keepdims=True)
    y = x * lax.rsqrt(ms + NORM_EPS) * g_ref[...]
    o_ref[...] = (y * (1 + sc_ref[pl.ds(t, 1), :]) + sh_ref[pl.ds(t, 1), :]).astype(BF16)


def _modulate(h, gain, mods, shift_chunk, scale_chunk, n_rows, cfg):
    D = cfg.D
    bm = _pick_block(math.gcd(cfg.S, cfg.B * cfg.L), MODULATE_ROWS, SUBLANE_BF16)
    return pl.pallas_call(
        functools.partial(_modulate_kernel, bm=bm, cfg=cfg),
        out_shape=jax.ShapeDtypeStruct((n_rows, D), BF16),
        grid=(n_rows // bm,),
        in_specs=[
            pl.BlockSpec((bm, D), lambda i: (i, 0)),
            pl.BlockSpec((1, D), lambda i: (0, 0)),
            pl.BlockSpec((MOD_ROWS, D), lambda i: (0, shift_chunk)),
            pl.BlockSpec((MOD_ROWS, D), lambda i: (0, scale_chunk)),
        ],
        out_specs=pl.BlockSpec((bm, D), lambda i: (i, 0)),
        compiler_params=_cparams(1),
        name="modulate",
    )(h, gain.reshape(1, D), mods, mods)


def _stage_weight(w_ref, wb_ref, t, m, n_blocks):
    ck = w_ref.shape[0]

    @pl.when(t < n_blocks)
    def _():
        wb_ref[t % 2, pl.ds(pl.multiple_of(m * ck, ck), ck), :] = w_ref[...].astype(BF16)


def _qkv_kernel(*refs, n_blocks, n_norm_blocks, rope, bn):
    if rope:
        x_ref, w_ref, g_ref, cos_ref, sa_ref, sb_ref, o_ref, wb_ref = refs
    else:
        x_ref, w_ref, g_ref, o_ref, wb_ref = refs
    t = pl.program_id(0)
    _stage_weight(w_ref, wb_ref, t, pl.program_id(1), n_blocks)

    bm = x_ref.shape[0]
    rs = bm // QKV_ROW_SPLIT

    @pl.when(jnp.logical_and(t > 0, t - 1 < n_norm_blocks))
    def _():
        w = wb_ref.at[(t - 1) % 2]
        for r in range(QKV_ROW_SPLIT):
            rows = slice(r * rs, (r + 1) * rs)
            acc = _dot(x_ref[rows, :], w[...])
            outs = []
            for hh in range(bn // LANE):
                sl = slice(hh * LANE, (hh + 1) * LANE)
                xh = acc[:, sl]
                ms = jnp.mean(xh * xh, axis=-1, keepdims=True)
                y = xh * lax.rsqrt(ms + NORM_EPS) * g_ref[:, sl]
                if rope:
                    y = (y * cos_ref[rows, :] + pltpu.roll(y, 3 * LANE // 4, 1) * sa_ref[rows, :]
                         + pltpu.roll(y, LANE // 4, 1) * sb_ref[rows, :])
                outs.append(y.astype(BF16))
            o_ref[rows, :] = jnp.concatenate(outs, axis=1)

    @pl.when(jnp.logical_and(t > 0, t - 1 >= n_norm_blocks))
    def _():
        o_ref[...] = _dot(x_ref[...], wb_ref[(t - 1) % 2]).astype(BF16)


def _qkv_proj(xn, w_all, layer, gains, n_norm_cols, rope_tabs, cfg):
    M, D = xn.shape
    N = w_all.shape[2]
    bn = math.gcd(math.gcd(N, n_norm_cols), QKV_BLOCK_N)
    bm = _pick_block(M, 1152, SUBLANE_BF16)
    n_m, n_blocks = M // bm, N // bn
    ck = D // n_m
    assert bn % LANE == 0 and D % n_m == 0 and ck % SUBLANE_BF16 == 0
    rope = rope_tabs is not None
    row = lambda t, m: jnp.where(t == 0, 0, m)
    col = lambda t: jnp.maximum(t - 1, 0)
    in_specs = [
        pl.BlockSpec((bm, D), lambda t, m: (row(t, m), 0)),
        pl.BlockSpec((None, ck, bn), lambda t, m: (layer, m, jnp.minimum(t, n_blocks - 1))),
        pl.BlockSpec((1, bn), lambda t, m: (0, col(t))),
    ]
    args = [xn, w_all, gains]
    if rope:
        in_specs += [pl.BlockSpec((bm, LANE), lambda t, m: (row(t, m), 0))] * 3
        args += list(rope_tabs)
    return pl.pallas_call(
        functools.partial(_qkv_kernel, n_blocks=n_blocks, n_norm_blocks=n_norm_cols // bn, rope=rope, bn=bn),
        out_shape=jax.ShapeDtypeStruct((M, N), BF16),
        grid=(n_blocks + 1, n_m),
        in_specs=in_specs,
        out_specs=pl.BlockSpec((bm, bn), lambda t, m: (row(t, m), col(t))),
        scratch_shapes=[pltpu.VMEM((2, D, bn), BF16)],
        compiler_params=_cparams(2),
        name="qkv_proj",
    )(*args)


def _mm_res_kernel(a_ref, w_ref, res_ref, gate_ref, o_ref, wb_ref, *acc, n_blocks, nk, n_m, bm, cfg):
    part = pl.program_id(0)
    t = pl.program_id(1)
    m = pl.program_id(2)
    _stage_weight(w_ref, wb_ref, t, m, n_blocks)

    def epilogue(total):
        row0 = (part * n_m + m) * bm
        o_ref[...] = res_ref[...] + _row_select(gate_ref, row0, bm, cfg) * total

    @pl.when(t > 0)
    def _():
        w = wb_ref.at[(t - 1) % 2]
        if nk == 1:
            epilogue(_dot(a_ref[...], w[...]))
        else:
            k = (t - 1) % nk
            acc_ref = acc[0]
            rows = pl.ds(pl.multiple_of(m * bm, 8), bm)

            @pl.when(k == 0)
            def _():
                acc_ref[rows, :] = _dot(a_ref[...], w[...])

            @pl.when(jnp.logical_and(k > 0, k < nk - 1))
            def _():
                acc_ref[rows, :] += _dot(a_ref[...], w[...])

            @pl.when(k == nk - 1)
            def _():
                epilogue(acc_ref[rows, :] + _dot(a_ref[...], w[...]))


def _mm_res(a, w_all, layer, res, mods, gate_chunk, n_rows, cfg):
    _, K, N = w_all.shape
    bn = math.gcd(N, RES_BLOCK_N)
    bk = K if K <= MAX_FULL_K else _pick_block(K, K_TILE, LANE)
    nk = K // bk
    n_parts = RES_ROW_PARTS if nk > 1 else 1
    bm = _pick_row_block(n_rows // n_parts, RES_BLOCK_M, bk)
    n_m = n_rows // n_parts // bm
    n_blocks = (N // bn) * nk
    ck = bk // n_m
    assert n_rows % n_parts == 0 and bk % n_m == 0 and ck % SUBLANE_BF16 == 0
    wblk = lambda t: jnp.minimum(t, n_blocks - 1)
    cblk = lambda t: jnp.maximum(t - 1, 0)
    orow = lambda p, t, m: p * n_m + jnp.where(jnp.logical_and(t > 0, cblk(t) % nk == nk - 1), m, 0)
    scratch = [pltpu.VMEM((2, bk, bn), BF16)]
    if nk > 1:
        scratch.append(pltpu.VMEM((n_rows // n_parts, bn), F32))
    return pl.pallas_call(
        functools.partial(_mm_res_kernel, n_blocks=n_blocks, nk=nk, n_m=n_m, bm=bm, cfg=cfg),
        out_shape=jax.ShapeDtypeStruct((n_rows, N), F32),
        grid=(n_parts, n_blocks + 1, n_m),
        in_specs=[
            pl.BlockSpec((bm, bk), lambda p, t, m: (p * n_m + jnp.where(t == 0, 0, m), cblk(t) % nk)),
            pl.BlockSpec((None, ck, bn), lambda p, t, m: (layer, (wblk(t) % nk) * n_m + m, wblk(t) // nk)),
            pl.BlockSpec((bm, bn), lambda p, t, m: (orow(p, t, m), cblk(t) // nk)),
            pl.BlockSpec((MOD_ROWS, bn), lambda p, t, m: (0, gate_chunk * (N // bn) + cblk(t) // nk)),
        ],
        out_specs=pl.BlockSpec((bm, bn), lambda p, t, m: (orow(p, t, m), cblk(t) // nk)),
        scratch_shapes=scratch,
        compiler_params=_cparams(3),
        name="matmul_residual",
    )(a, w_all, res, mods)


def _swiglu_kernel(*refs, gated, n_blocks, nbe):
    if gated:
        x_ref, wg_ref, wu_ref, gates_ref, o_ref, wgb_ref, wub_ref = refs
    else:
        x_ref, wg_ref, wu_ref, o_ref, wgb_ref, wub_ref = refs
    t = pl.program_id(0)
    m = pl.program_id(1)
    _stage_weight(wg_ref, wgb_ref, t, m, n_blocks)
    _stage_weight(wu_ref, wub_ref, t, m, n_blocks)

    rs = x_ref.shape[0] // SWIGLU_ROW_SPLIT

    @pl.when(t > 0)
    def _():
        j = t - 1
        wg = wgb_ref.at[j % 2]
        wu = wub_ref.at[j % 2]
        for r in range(SWIGLU_ROW_SPLIT):
            rows = slice(r * rs, (r + 1) * rs)
            x = x_ref[rows, :]
            g = _dot(x, wg[...])
            u = _dot(x, wu[...])
            hid = g * jax.nn.sigmoid(g) * u
            if gated:
                gates = gates_ref[rows, :]
                lane = lax.broadcasted_iota(jnp.int32, gates.shape, 1)
                hid = hid * jnp.sum(jnp.where(lane == j // nbe, gates, 0.0), axis=-1, keepdims=True)
            o_ref[rows, :] = hid.astype(BF16)


def _swiglu_up(t, w_gate, w_up, layer, gates):
    M, D = t.shape
    _, E, _, F = w_gate.shape
    bn = math.gcd(F, SWIGLU_BLOCK_N)
    nbe = F // bn
    bm = _pick_block(M, 1152, SUBLANE_BF16)
    n_m, n_blocks = M // bm, E * nbe
    ck = D // n_m
    assert D % n_m == 0 and ck % SUBLANE_BF16 == 0
    gated = gates is not None
    row = lambda t, m: jnp.where(t == 0, 0, m)
    col = lambda t: jnp.maximum(t - 1, 0)
    wblk = lambda t: jnp.minimum(t, n_blocks - 1)
    w_spec = pl.BlockSpec((None, None, ck, bn), lambda t, m: (layer, wblk(t) // nbe, m, wblk(t) % nbe))
    in_specs = [pl.BlockSpec((bm, D), lambda t, m: (row(t, m), 0)), w_spec, w_spec]
    args = [t, w_gate, w_up]
    if gated:
        in_specs.append(pl.BlockSpec((bm, LANE), lambda t, m: (row(t, m), 0)))
        args.append(gates)
    return pl.pallas_call(
        functools.partial(_swiglu_kernel, gated=gated, n_blocks=n_blocks, nbe=nbe),
        out_shape=jax.ShapeDtypeStruct((M, E * F), BF16),
        grid=(n_blocks + 1, n_m),
        in_specs=in_specs,
        out_specs=pl.BlockSpec((bm, bn), lambda t, m: (row(t, m), col(t))),
        scratch_shapes=[pltpu.VMEM((2, D, bn), BF16), pltpu.VMEM((2, D, bn), BF16)],
        compiler_params=_cparams(2),
        name="swiglu_up",
    )(*args)


def _router_kernel(t_ref, w_ref, b_ref, o_ref):
    logits = _dot(t_ref[...], w_ref[...].astype(BF16)) + b_ref[...]
    lane = lax.broadcasted_iota(jnp.int32, logits.shape, 1).astype(F32)
    top1 = jnp.max(logits, axis=-1, keepdims=True)
    i1 = jnp.min(jnp.where(logits == top1, lane, float(LANE)), axis=-1, keepdims=True)
    rest = jnp.where(lane == i1, -jnp.inf, logits)
    top2 = jnp.max(rest, axis=-1, keepdims=True)
    i2 = jnp.min(jnp.where(rest == top2, lane, float(LANE)), axis=-1, keepdims=True)
    e2 = jnp.exp(top2 - top1)
    inv = 1.0 / (1.0 + e2)
    o_ref[...] = jnp.where(lane == i1, inv, 0.0) + jnp.where(lane == i2, e2 * inv, 0.0)


def _router(t, router_w, router_b, cfg):
    M, D = t.shape
    E = cfg.n_experts
    assert cfg.top_k == 2 and E <= LANE
    w = jnp.zeros((D, LANE), F32).at[:, :E].set(router_w)
    b = jnp.full((1, LANE), NEG_INF, F32).at[0, :E].set(router_b)
    bm = _pick_block(M, 1152, SUBLANE_BF16)
    return pl.pallas_call(
        _router_kernel,
        out_shape=jax.ShapeDtypeStruct((M, LANE), F32),
        grid=(M // bm,),
        in_specs=[
            pl.BlockSpec((bm, D), lambda m: (m, 0)),
            pl.BlockSpec((D, LANE), lambda m: (0, 0)),
            pl.BlockSpec((1, LANE), lambda m: (0, 0)),
        ],
        out_specs=pl.BlockSpec((bm, LANE), lambda m: (m, 0)),
        compiler_params=_cparams(1),
        name="moe_router",
    )(t, w, b)


def _na_variant_groups(n_groups):
    return (0, 1 if n_groups > 2 else 0, n_groups - 1)


def _na_build_bias(tp_ref, bias_ref, *, rows, W, kh):
    n_groups = rows // kh
    lane = lax.broadcasted_iota(jnp.int32, (W, 2 * W), 1)
    neg = jnp.full((W, 2 * W), NEG_INF, F32)
    for variant, g in enumerate(_na_variant_groups(n_groups)):
        key_row = min(max(g * kh - kh // 2, 0), rows - 2 * kh)
        for ql in range(kh):
            qi = g * kh + ql
            r0 = min(max(qi - kh // 2, 0), rows - kh)
            for j in range(kh):
                ki = key_row + 2 * j
                first_ok = r0 <= ki < r0 + kh
                second_ok = r0 <= ki + 1 < r0 + kh
                if not (first_ok or second_ok):
                    blk = neg
                else:
                    blk = tp_ref[ki - qi + kh]
                    if not second_ok:
                        blk = jnp.where(lane < W, blk, NEG_INF)
                    elif not first_ok:
                        blk = jnp.where(lane >= W, blk, NEG_INF)
                bias_ref[variant, ql * W:(ql + 1) * W, j * 2 * W:(j + 1) * 2 * W] = blk


def _na_kernel(q_ref, k_ref, v_ref, ck_ref, cv_ref, tp_ref, o_ref, bias_ref, *, rows, W, kh, unroll):
    @pl.when(pl.program_id(1) == 0)
    def _():
        _na_build_bias(tp_ref, bias_ref, rows=rows, W=W, kh=kh)

    ck = ck_ref[...]
    cv = cv_ref[...]
    n_groups = rows // kh
    gq, gk = kh * W, 2 * kh * W

    def one_group(g):
        variant = jnp.where(g == 0, 0, jnp.where(g == n_groups - 1, 2, 1))
        key_row = jnp.clip(g * kh - kh // 2, 0, rows - 2 * kh)
        q = q_ref[pl.ds(pl.multiple_of(g * gq, gq), gq), :]
        kstart = pl.multiple_of(key_row * W, W)
        s = _dot_t(q, k_ref[pl.ds(kstart, gk), :]) + bias_ref[variant]
        sc = _dot_t(q, ck)
        m = jnp.maximum(jnp.max(s, axis=-1, keepdims=True), jnp.max(sc, axis=-1, keepdims=True))
        e = jnp.exp2(s - m)
        ec = jnp.exp2(sc - m)
        inv = 1.0 / (jnp.sum(e, axis=-1, keepdims=True) + jnp.sum(ec, axis=-1, keepdims=True))
        o = (_dot(e.astype(BF16), v_ref[pl.ds(kstart, gk), :]) + _dot(ec.astype(BF16), cv)) * inv
        return o.astype(BF16)

    def body(it, carry):
        outs = [one_group(it * unroll + u) for u in range(unroll)]
        start = pl.multiple_of(it * (unroll * gq), unroll * gq)
        o_ref[pl.ds(start, unroll * gq), :] = jnp.concatenate(outs, axis=0)
        return carry

    lax.fori_loop(0, n_groups // unroll, body, 0)


def _na_bias_blocks(rpb, cfg):
    W, kw, wh = cfg.grid_w, cfg.na_win_w, cfg.na_win_h
    H = rpb.shape[0]
    wq = jnp.arange(W)[:, None]
    kc = jnp.arange(W)[None, :]
    c0 = jnp.clip(wq - kw // 2, 0, W - kw)
    col_valid = (kc >= c0) & (kc < c0 + kw)
    cidx = jnp.clip(kc - wq + (kw - 1), 0, 2 * kw - 2)
    toe = jnp.where(col_valid[None, None], rpb[:, :, cidx], NEG_INF).astype(F32)
    pad = jnp.full((H, 1, W, W), NEG_INF, F32)
    toe = jnp.concatenate([pad, toe, pad], axis=1)
    return jnp.concatenate([toe[:, :-1], toe[:, 1:]], axis=-1) * LOG2E


def _na_attention(qkv, rpb, n_out_rows, cfg):
    B, S, L, D, W, Dh = cfg.B, cfg.S, cfg.L, cfg.D, cfg.grid_w, cfg.head_dim
    H = D // Dh
    rows = S // W
    kh = min(cfg.na_win_h, rows)
    n_groups = rows // kh
    assert rows % kh == 0 and n_groups >= 2 and kh == cfg.na_win_h and 2 * W == LANE
    blocks = _na_bias_blocks(rpb, cfg)
    cblk = cfg.MX // L
    return pl.pallas_call(
        functools.partial(_na_kernel, rows=rows, W=W, kh=kh, unroll=math.gcd(rows // kh, NA_GROUP_UNROLL)),
        out_shape=jax.ShapeDtypeStruct((n_out_rows, D), BF16),
        grid=(H, B),
        in_specs=[
            pl.BlockSpec((S, Dh), lambda h, b: (b, h)),
            pl.BlockSpec((S, Dh), lambda h, b: (b, H + h)),
            pl.BlockSpec((S, Dh), lambda h, b: (b, 2 * H + h)),
            pl.BlockSpec((L, Dh), lambda h, b: (cblk + b, H + h)),
            pl.BlockSpec((L, Dh), lambda h, b: (cblk + b, 2 * H + h)),
            pl.BlockSpec((None, 2 * kh, W, 2 * W), lambda h, b: (h, 0, 0, 0)),
        ],
        out_specs=pl.BlockSpec((S, Dh), lambda h, b: (b, h)),
        scratch_shapes=[pltpu.VMEM((3, kh * W, 2 * kh * W), F32)],
        compiler_params=_cparams(2),
        name="neighborhood_attention",
    )(qkv, qkv, qkv, qkv, qkv, blocks)


def _gqa_kernel(*refs, G, has_local, has_sink, has_alias, win, S, Dh):
    refs = list(refs)
    sink_ref = refs.pop(0) if has_sink else None
    q_ref = refs.pop(0)
    k_ref, v_ref, mask_ref = (refs.pop(0), refs.pop(0), refs.pop(0)) if has_local else (None, None, None)
    ck_ref, cv_ref = refs.pop(0), refs.pop(0)
    if has_alias:
        refs.pop(0)
    o_ref = refs.pop(0)
    hk = pl.program_id(1)
    rows = q_ref.shape[0]
    ck = ck_ref[...]
    cv = cv_ref[...]
    if has_local:
        nkw = rows + 2 * win
        start = pl.multiple_of(jnp.clip(pl.program_id(2) * rows - win, 0, S - nkw), win)
        kw = k_ref[pl.ds(start, nkw), :]
        vw = v_ref[pl.ds(start, nkw), :]
    outs = []
    for g in range(G):
        q = q_ref[:, g * Dh:(g + 1) * Dh]
        sc = _dot_t(q, ck)
        m = jnp.max(sc, axis=-1, keepdims=True)
        if has_local:
            s = _dot_t(q, kw) + mask_ref[...]
            m = jnp.maximum(m, jnp.max(s, axis=-1, keepdims=True))
        if has_sink:
            sk = sink_ref[hk, g] * LOG2E
            m = jnp.maximum(m, sk)
        ec = jnp.exp2(sc - m)
        l = jnp.sum(ec, axis=-1, keepdims=True)
        o = _dot(ec.astype(BF16), cv)
        if has_local:
            e = jnp.exp2(s - m)
            l = l + jnp.sum(e, axis=-1, keepdims=True)
            o = o + _dot(e.astype(BF16), vw)
        if has_sink:
            l = l + jnp.exp2(sk - m)
        outs.append((o * (1.0 / l)).astype(BF16))
    o_ref[...] = jnp.concatenate(outs, axis=1)


def _band_mask(rows, win, S):
    nkw = rows + 2 * win
    r = jnp.arange(rows)[:, None]
    c = jnp.arange(nkw)[None, :]
    offsets = (0, -win, -2 * win)
    return jnp.stack([jnp.where(jnp.abs(off + c - r) <= win, 0.0, NEG_INF) for off in offsets]).astype(F32)


def _gqa_attention(qkv, sink, q_heads, kv_heads, local, o_prev, n_out_rows, cfg):
    B, S, L, D, Dh = cfg.B, cfg.S, cfg.L, cfg.D, cfg.head_dim
    G = q_heads // kv_heads
    kcol = q_heads
    vcol = q_heads + kv_heads
    cblk = cfg.MX // L
    has_sink = sink is not None
    win = cfg.swa_window
    rows_step = GQA_QUERY_ROWS if local else L
    nq = S // rows_step if local else 1
    qrow = (lambda b, i: b * nq + i) if local else (lambda b, i: cblk + b)
    in_specs, args = [], []
    if has_sink:
        in_specs.append(pl.BlockSpec(memory_space=pltpu.SMEM))
        args.append(sink.reshape(kv_heads, G).astype(F32))
    in_specs.append(pl.BlockSpec((rows_step, G * Dh), lambda b, h, i: (qrow(b, i), h)))
    args.append(qkv)
    if local:
        assert cfg.swa_block == win and S % rows_step == 0 and nq >= 2 and win % SUBLANE_BF16 == 0
        nkw = rows_step + 2 * win
        variant = lambda i: jnp.where(i == 0, 0, jnp.where(i == nq - 1, 2, 1))
        in_specs += [pl.BlockSpec((S, Dh), lambda b, h, i: (b, kcol + h)),
                     pl.BlockSpec((S, Dh), lambda b, h, i: (b, vcol + h)),
                     pl.BlockSpec((None, rows_step, nkw), lambda b, h, i: (variant(i), 0, 0))]
        args += [qkv, qkv, _band_mask(rows_step, win, S)]
    in_specs += [pl.BlockSpec((L, Dh), lambda b, h, i: (cblk + b, kcol + h)),
                 pl.BlockSpec((L, Dh), lambda b, h, i: (cblk + b, vcol + h))]
    args += [qkv, qkv]
    aliases = {}
    if o_prev is not None:
        in_specs.append(pl.BlockSpec(memory_space=pl.ANY))
        aliases = {len(args): 0}
        args.append(o_prev)
    return pl.pallas_call(
        functools.partial(_gqa_kernel, G=G, has_local=local, has_sink=has_sink,
                          has_alias=o_prev is not None, win=win, S=S, Dh=Dh),
        out_shape=jax.ShapeDtypeStruct((n_out_rows, D), BF16),
        grid=(B, kv_heads, nq),
        in_specs=in_specs,
        out_specs=pl.BlockSpec((rows_step, G * Dh), lambda b, h, i: (qrow(b, i), h)),
        input_output_aliases=aliases,
        compiler_params=_cparams(3),
        name="gqa_window_attention" if local else "gqa_context_attention",
    )(*args)


def _diff_kernel(*refs, has_x, has_alias, chunk, row_split, S, L, lambda_init, Dh):
    refs = list(refs)
    lq1, lk1, lq2, lk2, sub_ref, q_ref = refs[:6]
    refs = refs[6:]
    kx_ref, vx_ref = (refs.pop(0), refs.pop(0)) if has_x else (None, None)
    kc_ref, vc_ref = refs.pop(0), refs.pop(0)
    if has_alias:
        refs.pop(0)
    o_ref, e_ref = refs
    lam = (jnp.exp(jnp.sum(lq1[...] * lk1[...], axis=-1, keepdims=True))
           - jnp.exp(jnp.sum(lq2[...] * lk2[...], axis=-1, keepdims=True)) + lambda_init)
    chunks = []
    if has_x:
        chunks += [(kx_ref, vx_ref, c * chunk, chunk, c * chunk) for c in range(S // chunk)]
    off = S if has_x else 0
    chunks.append((kc_ref, vc_ref, 0, L, off))
    rs = q_ref.shape[0] // row_split
    for r in range(row_split):
        rows = slice(r * rs, (r + 1) * rs)
        coef = []
        for mi in range(2):
            sl = slice(mi * Dh, (mi + 1) * Dh)
            q = q_ref[rows, sl]
            mx = jnp.full((rs, 1), -jnp.inf, F32)
            l = jnp.zeros((rs, 1), F32)
            maxes = []
            for k_ref, _, r0, n, e0 in chunks:
                s = _dot_t(q, k_ref[r0:r0 + n, sl])
                mn = jnp.maximum(mx, jnp.max(s, axis=-1, keepdims=True))
                e = jnp.exp2(s - mn)
                l = l * jnp.exp2(mx - mn) + jnp.sum(e, axis=-1, keepdims=True)
                e_ref[mi, rows, e0:e0 + n] = e
                maxes.append(mn)
                mx = mn
            inv = 1.0 / l
            coef.append([jnp.exp2(mc - mx) * inv for mc in maxes])
        acc = jnp.zeros((rs, 2 * Dh), F32)
        for ci, (_, v_ref, r0, n, e0) in enumerate(chunks):
            p = e_ref[0, rows, e0:e0 + n] * coef[0][ci] - e_ref[1, rows, e0:e0 + n] * (lam * coef[1][ci])
            acc = acc + _dot(p.astype(BF16), v_ref[r0:r0 + n, :])
        ms = jnp.mean(acc * acc, axis=-1, keepdims=True)
        y = acc * lax.rsqrt(ms + NORM_EPS) * sub_ref[...] * (1.0 - lambda_init)
        o_ref[rows, :] = y.astype(BF16)


def _diff_attention(qkv, lams, subln, lambda_init, main, o_prev, n_out_rows, cfg):
    B, S, L, D, Dh = cfg.B, cfg.S, cfg.L, cfg.D, cfg.head_dim
    H = D // (2 * Dh)
    HW = 2 * Dh
    cblk = cfg.MX // L
    bq = DIFF_QUERY_ROWS if main else L
    nq = S // bq if main else 1
    nkeys = S + L if main else L
    qrow = (lambda b, i: b * nq + i) if main else (lambda b, i: cblk + b)
    vec = pl.BlockSpec((1, Dh), lambda b, h, i: (0, 0))
    in_specs = [vec, vec, vec, vec,
                pl.BlockSpec((1, HW), lambda b, h, i: (0, 0)),
                pl.BlockSpec((bq, HW), lambda b, h, i: (qrow(b, i), h))]
    args = [v.reshape(1, Dh).astype(F32) for v in lams] + [subln.reshape(1, HW).astype(F32), qkv]
    if main:
        in_specs += [pl.BlockSpec((S, HW), lambda b, h, i: (b, H + h)),
                     pl.BlockSpec((S, HW), lambda b, h, i: (b, 2 * H + h))]
        args += [qkv, qkv]
    in_specs += [pl.BlockSpec((L, HW), lambda b, h, i: (cblk + b, H + h)),
                 pl.BlockSpec((L, HW), lambda b, h, i: (cblk + b, 2 * H + h))]
    args += [qkv, qkv]
    aliases = {}
    if o_prev is not None:
        in_specs.append(pl.BlockSpec(memory_space=pl.ANY))
        aliases = {len(args): 0}
        args.append(o_prev)
    return pl.pallas_call(
        functools.partial(_diff_kernel, has_x=main, has_alias=o_prev is not None,
                          chunk=math.gcd(S, DIFF_KEY_CHUNK), row_split=min(DIFF_ROW_SPLIT, max(bq // LANE, 1)),
                          S=S, L=L,
                          lambda_init=lambda_init, Dh=Dh),
        out_shape=jax.ShapeDtypeStruct((n_out_rows, D), BF16),
        grid=(B, H, nq),
        in_specs=in_specs,
        out_specs=pl.BlockSpec((bq, HW), lambda b, h, i: (qrow(b, i), h)),
        scratch_shapes=[pltpu.VMEM((2, bq, nkeys), F32)],
        input_output_aliases=aliases,
        compiler_params=_cparams(3),
        name="diff_attention" if main else "diff_context_attention",
    )(*args)


def _rope_tables(cfg):
    Dh = cfg.head_dim
    t = jnp.arange(cfg.S, dtype=jnp.int32)
    row = (t // cfg.grid_w).astype(F32)
    col = (t % cfg.grid_w).astype(F32)
    axis_dim = Dh // 2
    inv_freq = ROPE_BASE ** (-jnp.arange(0, axis_dim, 2, dtype=F32) / axis_dim)
    ang_r = row[:, None] * inv_freq[None, :]
    ang_c = col[:, None] * inv_freq[None, :]
    ang = jnp.concatenate([ang_r, ang_r, ang_c, ang_c], axis=-1)
    cos, sin = jnp.cos(ang), jnp.sin(ang)
    quarter = jnp.arange(Dh) // (Dh // 4)
    sin_a = jnp.where(quarter % 2 == 0, -sin, 0.0)
    sin_b = jnp.where(quarter % 2 == 1, sin, 0.0)
    n_ctx = cfg.B * cfg.L

    def full(tab, fill):
        return jnp.concatenate([jnp.tile(tab, (cfg.B, 1)), jnp.full((n_ctx, Dh), fill, F32)], axis=0)

    return full(cos, 1.0), full(sin_a, 0.0), full(sin_b, 0.0)


def _head_gains(q_gain, k_gain, n_q, n_k, n_v_cols, head_dim):
    q_scale = head_dim ** -0.5 * LOG2E
    return jnp.concatenate([jnp.tile(q_gain.astype(F32) * q_scale, n_q), jnp.tile(k_gain.astype(F32), n_k),
                            jnp.ones((n_v_cols,), F32)]).reshape(1, -1)


def _forward(cfg, x, c, ctx, c_ctx, ada_down, ada_up, ada_bias, norm_mix, norm_ffn,
             na_w_qkv, na_w_o, na_q_norm, na_k_norm, na_rpb,
             diff_w_qkv, diff_w_o, diff_q_norm, diff_k_norm,
             diff_lambda_q1, diff_lambda_k1, diff_lambda_q2, diff_lambda_k2, diff_subln,
             swa_w_qkv, swa_w_o, swa_q_norm, swa_k_norm, swa_sink,
             ffn_w_gate, ffn_w_up, ffn_w_down,
             moe_router, moe_router_bias, moe_w_gate, moe_w_up, moe_w_down):
    B, S, L, D, Dh = cfg.B, cfg.S, cfg.L, cfg.D, cfg.head_dim
    M, MX = cfg.M, cfg.MX
    heads = D // Dh
    h = jnp.concatenate([x.reshape(MX, D), ctx.reshape(B * L, D)], axis=0).astype(F32)
    cond = jnp.zeros((MOD_ROWS, D), F32).at[:B].set(c).at[B].set(c_ctx)
    mods_all = _ada_all(cond, ada_down, ada_up, ada_bias)
    rope_tabs = _rope_tables(cfg)

    for i in range(cfg.depth):
        ctx_out = i < cfg.depth - 1
        n_rows = M if ctx_out else MX
        mods = mods_all[i]
        xn = _modulate(h, norm_mix[i], mods, 0, 1, M, cfg)
        kind, j = i % 3, i // 3
        if kind == 0:
            gains = _head_gains(na_q_norm[j], na_k_norm[j], heads, heads, D, Dh)
            qkv = _qkv_proj(xn, na_w_qkv, j, gains, 2 * D, None, cfg)
            o = _na_attention(qkv, na_rpb[j], n_rows, cfg)
            if ctx_out:
                o = _gqa_attention(qkv, None, heads, heads, False, o, n_rows, cfg)
            w_o = na_w_o
        elif kind == 1:
            gains = _head_gains(diff_q_norm[j], diff_k_norm[j], heads, heads, D, Dh)
            qkv = _qkv_proj(xn, diff_w_qkv, j, gains, 2 * D, rope_tabs, cfg)
            lams = (diff_lambda_q1[j], diff_lambda_k1[j], diff_lambda_q2[j], diff_lambda_k2[j])
            lambda_init = 0.8 - 0.6 * math.exp(-0.3 * i)
            o = _diff_attention(qkv, lams, diff_subln[j], lambda_init, True, None, n_rows, cfg)
            if ctx_out:
                o = _diff_attention(qkv, lams, diff_subln[j], lambda_init, False, o, n_rows, cfg)
            w_o = diff_w_o
        else:
            kvh = heads // cfg.swa_group
            gains = _head_gains(swa_q_norm[j], swa_k_norm[j], heads, kvh, kvh * Dh, Dh)
            qkv = _qkv_proj(xn, swa_w_qkv, j, gains, (heads + kvh) * Dh, rope_tabs, cfg)
            o = _gqa_attention(qkv, swa_sink[j], heads, kvh, True, None, n_rows, cfg)
            if ctx_out:
                o = _gqa_attention(qkv, swa_sink[j], heads, kvh, False, o, n_rows, cfg)
            w_o = swa_w_o
        h = _mm_res(o, w_o, j, h, mods, 2, n_rows, cfg)
        t = _modulate(h, norm_ffn[i], mods, 3, 4, n_rows, cfg)
        fj = i // 2
        if i % 2 == 0:
            n_ffn, _, d_ff = ffn_w_gate.shape
            hid = _swiglu_up(t, ffn_w_gate.reshape(n_ffn, 1, D, d_ff), ffn_w_up.reshape(n_ffn, 1, D, d_ff), fj, None)
            w_down = ffn_w_down
        else:
            gates = _router(t, moe_router[fj], moe_router_bias[fj], cfg)
            hid = _swiglu_up(t, moe_w_gate, moe_w_up, fj, gates)
            w_down = moe_w_down.reshape(moe_w_down.shape[0], -1, D)
        h = _mm_res(hid, w_down, fj, h, mods, 5, n_rows, cfg)
    return h.reshape(B, S, D)


_CFG = Cfg(B=2, S=4096, L=256, D=4096, depth=4, grid_w=64, head_dim=128, na_win_h=8, na_win_w=16,
           swa_window=128, swa_block=128, swa_group=4, n_experts=8, top_k=2)


def kernel(x, c, ctx, c_ctx, ada_down, ada_up, ada_bias, norm_mix, norm_ffn, na_w_qkv, na_w_o, na_q_norm, na_k_norm, na_rpb, diff_w_qkv, diff_w_o, diff_q_norm, diff_k_norm, diff_lambda_q1, diff_lambda_k1, diff_lambda_q2, diff_lambda_k2, diff_subln, swa_w_qkv, swa_w_o, swa_q_norm, swa_k_norm, swa_sink, ffn_w_gate, ffn_w_up, ffn_w_down, moe_router, moe_router_bias, moe_w_gate, moe_w_up, moe_w_down):
    return _forward(_CFG, x, c, ctx, c_ctx, ada_down, ada_up, ada_bias, norm_mix, norm_ffn, na_w_qkv, na_w_o, na_q_norm, na_k_norm, na_rpb, diff_w_qkv, diff_w_o, diff_q_norm, diff_k_norm, diff_lambda_q1, diff_lambda_k1, diff_lambda_q2, diff_lambda_k2, diff_subln, swa_w_qkv, swa_w_o, swa_q_norm, swa_k_norm, swa_sink, ffn_w_gate, ffn_w_up, ffn_w_down, moe_router, moe_router_bias, moe_w_gate, moe_w_up, moe_w_down)
```

```python
import functools
import math
from typing import NamedTuple

import jax
import jax.numpy as jnp
from jax import lax
from jax.experimental import pallas as pl
from jax.experimental.pallas import tpu as pltpu

F32 = jnp.float32
BF16 = jnp.bfloat16

LANE = 128
SUBLANE_BF16 = 16
MOD_ROWS = 8
VMEM_LIMIT = 58 * 1024 * 1024
MAX_FULL_K = 4096
K_TILE = 4096
RES_ROW_PARTS = 2
RES_BLOCK_N = 512
RES_BLOCK_M = 1152
RES_ROW_SPLIT = 2
QKV_BLOCK_N = 1024
QKV_ROW_SPLIT = 4
SWIGLU_BLOCK_N = 512
SWIGLU_ROW_SPLIT = 4

NA_GROUP_UNROLL = 8
MODULATE_ROWS = 512
GQA_QUERY_ROWS = 512
GQA_SUB_ROWS = 256
DIFF_QUERY_ROWS = 1024
DIFF_KEY_CHUNK = 1024
DIFF_ROW_SPLIT = 8

NORM_EPS = 1e-6
NEG_INF = -1e30
ROPE_BASE = 10000.0
LOG2E = 1.4426950408889634


class Cfg(NamedTuple):
    B: int
    S: int
    L: int
    D: int
    depth: int
    grid_w: int
    head_dim: int
    na_win_h: int
    na_win_w: int
    swa_window: int
    swa_block: int
    swa_group: int
    n_experts: int
    top_k: int

    @property
    def M(self):
        return self.B * (self.S + self.L)

    @property
    def MX(self):
        return self.B * self.S


def _cparams(n_axes):
    return pltpu.CompilerParams(dimension_semantics=("arbitrary",) * n_axes,
                                vmem_limit_bytes=VMEM_LIMIT)


def _pick_block(n, cap, mult):
    best = None
    for d in range(mult, min(n, cap) + 1, mult):
        if n % d == 0:
            best = d
    assert best is not None, (n, cap, mult)
    return best


def _pick_row_block(rows, cap, bk):
    best = None
    for d in range(SUBLANE_BF16, min(rows, cap) + 1, SUBLANE_BF16):
        if rows % d == 0 and bk % (rows // d) == 0 and (bk // (rows // d)) % SUBLANE_BF16 == 0:
            best = d
    assert best is not None, (rows, cap, bk)
    return best


def _dot(a, b):
    return jnp.dot(a, b, preferred_element_type=F32)


def _dot_t(a, b):
    return lax.dot_general(a, b, (((1,), (1,)), ((), ())), preferred_element_type=F32)


def _row_select(vecs_ref, row0, bm, cfg):
    rows = row0 + lax.broadcasted_iota(jnp.int32, (bm, 1), 0)
    g = vecs_ref[cfg.B:cfg.B + 1, :]
    for b in reversed(range(cfg.B)):
        g = jnp.where(rows < (b + 1) * cfg.S, vecs_ref[b:b + 1, :], g)
    return g


def _ada_kernel(c_ref, down_ref, up_ref, b_ref, o_ref):
    c = c_ref[...]
    sc = (c * jax.nn.sigmoid(c)).astype(BF16)
    t = _dot(sc, down_ref[...].astype(BF16))
    o_ref[...] = _dot(t.astype(BF16), up_ref[...].astype(BF16)) + b_ref[...]


def _ada_all(cond, ada_down, ada_up, ada_bias):
    depth, D, R = ada_down.shape
    N = ada_up.shape[2]
    bn = _pick_block(N, 3072, LANE)
    return pl.pallas_call(
        _ada_kernel,
        out_shape=jax.ShapeDtypeStruct((depth, MOD_ROWS, N), F32),
        grid=(depth, N // bn),
        in_specs=[
            pl.BlockSpec((MOD_ROWS, D), lambda l, j: (0, 0)),
            pl.BlockSpec((None, D, R), lambda l, j: (l, 0, 0)),
            pl.BlockSpec((None, R, bn), lambda l, j: (l, 0, j)),
            pl.BlockSpec((None, 1, bn), lambda l, j: (l, 0, j)),
        ],
        out_specs=pl.BlockSpec((None, MOD_ROWS, bn), lambda l, j: (l, 0, j)),
        compiler_params=_cparams(2),
        name="ada_modulation",
    )(cond, ada_down, ada_up, ada_bias.reshape(depth, 1, N))


def _modulate_kernel(x_ref, g_ref, sh_ref, sc_ref, o_ref, *, bm, cfg):
    row0 = pl.program_id(0) * bm
    t = jnp.where(row0 < cfg.MX, row0 // cfg.S, cfg.B)
    x = x_ref[...]
    ms = jnp.mean(x * x, axis=-1, keepdims=True)
    y = x * lax.rsqrt(ms + NORM_EPS) * g_ref[...]
    o_ref[...] = (y * (1 + sc_ref[pl.ds(t, 1), :]) + sh_ref[pl.ds(t, 1), :]).astype(BF16)


def _modulate(h, gain, mods, shift_chunk, scale_chunk, n_rows, cfg):
    D = cfg.D
    bm = _pick_block(math.gcd(cfg.S, cfg.B * cfg.L), MODULATE_ROWS, SUBLANE_BF16)
    return pl.pallas_call(
        functools.partial(_modulate_kernel, bm=bm, cfg=cfg),
        out_shape=jax.ShapeDtypeStruct((n_rows, D), BF16),
        grid=(n_rows // bm,),
        in_specs=[
            pl.BlockSpec((bm, D), lambda i: (i, 0)),
            pl.BlockSpec((1, D), lambda i: (0, 0)),
            pl.BlockSpec((MOD_ROWS, D), lambda i: (0, shift_chunk)),
            pl.BlockSpec((MOD_ROWS, D), lambda i: (0, scale_chunk)),
        ],
        out_specs=pl.BlockSpec((bm, D), lambda i: (i, 0)),
        compiler_params=_cparams(1),
        name="modulate",
    )(h, gain.reshape(1, D), mods, mods)


def _stage_weight(w_ref, wb_ref, t, m, n_blocks):
    ck = w_ref.shape[0]

    @pl.when(t < n_blocks)
    def _():
        wb_ref[t % 2, pl.ds(pl.multiple_of(m * ck, ck), ck), :] = w_ref[...].astype(BF16)


def _qkv_kernel(*refs, n_blocks, n_norm_blocks, rope, bn):
    if rope:
        x_ref, w_ref, g_ref, cos_ref, sa_ref, sb_ref, o_ref, wb_ref = refs
    else:
        x_ref, w_ref, g_ref, o_ref, wb_ref = refs
    t = pl.program_id(0)
    _stage_weight(w_ref, wb_ref, t, pl.program_id(1), n_blocks)

    bm = x_ref.shape[0]
    rs = bm // QKV_ROW_SPLIT

    @pl.when(jnp.logical_and(t > 0, t - 1 < n_norm_blocks))
    def _():
        w = wb_ref.at[(t - 1) % 2]
        for r in range(QKV_ROW_SPLIT):
            rows = slice(r * rs, (r + 1) * rs)
            acc = _dot(x_ref[rows, :], w[...])
            outs = []
            for hh in range(bn // LANE):
                sl = slice(hh * LANE, (hh + 1) * LANE)
                xh = acc[:, sl]
                ms = jnp.mean(xh * xh, axis=-1, keepdims=True)
                y = xh * lax.rsqrt(ms + NORM_EPS) * g_ref[:, sl]
                if rope:
                    y = (y * cos_ref[rows, :] + pltpu.roll(y, 3 * LANE // 4, 1) * sa_ref[rows, :]
                         + pltpu.roll(y, LANE // 4, 1) * sb_ref[rows, :])
                outs.append(y.astype(BF16))
            o_ref[rows, :] = jnp.concatenate(outs, axis=1)

    @pl.when(jnp.logical_and(t > 0, t - 1 >= n_norm_blocks))
    def _():
        o_ref[...] = _dot(x_ref[...], wb_ref[(t - 1) % 2]).astype(BF16)


def _qkv_proj(xn, w_all, layer, gains, n_norm_cols, rope_tabs, cfg):
    M, D = xn.shape
    N = w_all.shape[2]
    bn = math.gcd(math.gcd(N, n_norm_cols), QKV_BLOCK_N)
    bm = _pick_block(M, 1152, SUBLANE_BF16)
    n_m, n_blocks = M // bm, N // bn
    ck = D // n_m
    assert bn % LANE == 0 and D % n_m == 0 and ck % SUBLANE_BF16 == 0
    rope = rope_tabs is not None
    row = lambda t, m: jnp.where(t == 0, 0, m)
    col = lambda t: jnp.maximum(t - 1, 0)
    in_specs = [
        pl.BlockSpec((bm, D), lambda t, m: (row(t, m), 0)),
        pl.BlockSpec((None, ck, bn), lambda t, m: (layer, m, jnp.minimum(t, n_blocks - 1))),
        pl.BlockSpec((1, bn), lambda t, m: (0, col(t))),
    ]
    args = [xn, w_all, gains]
    if rope:
        in_specs += [pl.BlockSpec((bm, LANE), lambda t, m: (row(t, m), 0))] * 3
        args += list(rope_tabs)
    return pl.pallas_call(
        functools.partial(_qkv_kernel, n_blocks=n_blocks, n_norm_blocks=n_norm_cols // bn, rope=rope, bn=bn),
        out_shape=jax.ShapeDtypeStruct((M, N), BF16),
        grid=(n_blocks + 1, n_m),
        in_specs=in_specs,
        out_specs=pl.BlockSpec((bm, bn), lambda t, m: (row(t, m), col(t))),
        scratch_shapes=[pltpu.VMEM((2, D, bn), BF16)],
        compiler_params=_cparams(2),
        name="qkv_proj",
    )(*args)


def _mm_res_kernel(a_ref, w_ref, res_ref, gate_ref, o_ref, wb_ref, *acc, n_blocks, nk, n_m, bm, cfg):
    part = pl.program_id(0)
    t = pl.program_id(1)
    m = pl.program_id(2)
    _stage_weight(w_ref, wb_ref, t, m, n_blocks)

    rs = bm // RES_ROW_SPLIT
    subs = [(slice(r * rs, (r + 1) * rs), r * rs) for r in range(RES_ROW_SPLIT)]

    def epilogue(sub, off, total):
        row0 = (part * n_m + m) * bm + off
        o_ref[sub, :] = res_ref[sub, :] + _row_select(gate_ref, row0, rs, cfg) * total

    @pl.when(t > 0)
    def _():
        w = wb_ref.at[(t - 1) % 2]
        if nk == 1:
            for sub, off in subs:
                epilogue(sub, off, _dot(a_ref[sub, :], w[...]))
        else:
            k = (t - 1) % nk
            acc_ref = acc[0]
            acc_rows = lambda off: pl.ds(pl.multiple_of(m * bm + off, 8), rs)

            @pl.when(k == 0)
            def _():
                for sub, off in subs:
                    acc_ref[acc_rows(off), :] = _dot(a_ref[sub, :], w[...])

            @pl.when(jnp.logical_and(k > 0, k < nk - 1))
            def _():
                for sub, off in subs:
                    acc_ref[acc_rows(off), :] += _dot(a_ref[sub, :], w[...])

            @pl.when(k == nk - 1)
            def _():
                for sub, off in subs:
                    epilogue(sub, off, acc_ref[acc_rows(off), :] + _dot(a_ref[sub, :], w[...]))


def _mm_res(a, w_all, layer, res, mods, gate_chunk, n_rows, cfg):
    _, K, N = w_all.shape
    bn = math.gcd(N, RES_BLOCK_N)
    bk = K if K <= MAX_FULL_K else _pick_block(K, K_TILE, LANE)
    nk = K // bk
    n_parts = RES_ROW_PARTS if nk > 1 else 1
    bm = _pick_row_block(n_rows // n_parts, RES_BLOCK_M, bk)
    n_m = n_rows // n_parts // bm
    n_blocks = (N // bn) * nk
    ck = bk // n_m
    assert n_rows % n_parts == 0 and bk % n_m == 0 and ck % SUBLANE_BF16 == 0
    wblk = lambda t: jnp.minimum(t, n_blocks - 1)
    cblk = lambda t: jnp.maximum(t - 1, 0)
    orow = lambda p, t, m: p * n_m + jnp.where(jnp.logical_and(t > 0, cblk(t) % nk == nk - 1), m, 0)
    scratch = [pltpu.VMEM((2, bk, bn), BF16)]
    if nk > 1:
        scratch.append(pltpu.VMEM((n_rows // n_parts, bn), F32))
    return pl.pallas_call(
        functools.partial(_mm_res_kernel, n_blocks=n_blocks, nk=nk, n_m=n_m, bm=bm, cfg=cfg),
        out_shape=jax.ShapeDtypeStruct((n_rows, N), F32),
        grid=(n_parts, n_blocks + 1, n_m),
        in_specs=[
            pl.BlockSpec((bm, bk), lambda p, t, m: (p * n_m + jnp.where(t == 0, 0, m), cblk(t) % nk)),
            pl.BlockSpec((None, ck, bn), lambda p, t, m: (layer, (wblk(t) % nk) * n_m + m, wblk(t) // nk)),
            pl.BlockSpec((bm, bn), lambda p, t, m: (orow(p, t, m), cblk(t) // nk)),
            pl.BlockSpec((MOD_ROWS, bn), lambda p, t, m: (0, gate_chunk * (N // bn) + cblk(t) // nk)),
        ],
        out_specs=pl.BlockSpec((bm, bn), lambda p, t, m: (orow(p, t, m), cblk(t) // nk)),
        scratch_shapes=scratch,
        compiler_params=_cparams(3),
        name="matmul_residual",
    )(a, w_all, res, mods)


def _swiglu_kernel(*refs, gated, n_blocks, nbe):
    if gated:
        x_ref, wg_ref, wu_ref, gates_ref, o_ref, wgb_ref, wub_ref = refs
    else:
        x_ref, wg_ref, wu_ref, o_ref, wgb_ref, wub_ref = refs
    t = pl.program_id(0)
    m = pl.program_id(1)
    _stage_weight(wg_ref, wgb_ref, t, m, n_blocks)
    _stage_weight(wu_ref, wub_ref, t, m, n_blocks)

    rs = x_ref.shape[0] // SWIGLU_ROW_SPLIT

    @pl.when(t > 0)
    def _():
        j = t - 1
        wg = wgb_ref.at[j % 2]
        wu = wub_ref.at[j % 2]
        for r in range(SWIGLU_ROW_SPLIT):
            rows = slice(r * rs, (r + 1) * rs)
            x = x_ref[rows, :]
            g = _dot(x, wg[...])
            u = _dot(x, wu[...])
            hid = g * jax.nn.sigmoid(g) * u
            if gated:
                gates = gates_ref[rows, :]
                lane = lax.broadcasted_iota(jnp.int32, gates.shape, 1)
                hid = hid * jnp.sum(jnp.where(lane == j // nbe, gates, 0.0), axis=-1, keepdims=True)
            o_ref[rows, :] = hid.astype(BF16)


def _swiglu_up(t, w_gate, w_up, layer, gates):
    M, D = t.shape
    _, E, _, F = w_gate.shape
    bn = math.gcd(F, SWIGLU_BLOCK_N)
    nbe = F // bn
    bm = _pick_block(M, 1152, SUBLANE_BF16)
    n_m, n_blocks = M // bm, E * nbe
    ck = D // n_m
    assert D % n_m == 0 and ck % SUBLANE_BF16 == 0
    gated = gates is not None
    row = lambda t, m: jnp.where(t == 0, 0, m)
    col = lambda t: jnp.maximum(t - 1, 0)
    wblk = lambda t: jnp.minimum(t, n_blocks - 1)
    w_spec = pl.BlockSpec((None, None, ck, bn), lambda t, m: (layer, wblk(t) // nbe, m, wblk(t) % nbe))
    in_specs = [pl.BlockSpec((bm, D), lambda t, m: (row(t, m), 0)), w_spec, w_spec]
    args = [t, w_gate, w_up]
    if gated:
        in_specs.append(pl.BlockSpec((bm, LANE), lambda t, m: (row(t, m), 0)))
        args.append(gates)
    return pl.pallas_call(
        functools.partial(_swiglu_kernel, gated=gated, n_blocks=n_blocks, nbe=nbe),
        out_shape=jax.ShapeDtypeStruct((M, E * F), BF16),
        grid=(n_blocks + 1, n_m),
        in_specs=in_specs,
        out_specs=pl.BlockSpec((bm, bn), lambda t, m: (row(t, m), col(t))),
        scratch_shapes=[pltpu.VMEM((2, D, bn), BF16), pltpu.VMEM((2, D, bn), BF16)],
        compiler_params=_cparams(2),
        name="swiglu_up",
    )(*args)


def _router_kernel(t_ref, w_ref, b_ref, o_ref):
    logits = _dot(t_ref[...], w_ref[...].astype(BF16)) + b_ref[...]
    lane = lax.broadcasted_iota(jnp.int32, logits.shape, 1).astype(F32)
    top1 = jnp.max(logits, axis=-1, keepdims=True)
    i1 = jnp.min(jnp.where(logits == top1, lane, float(LANE)), axis=-1, keepdims=True)
    rest = jnp.where(lane == i1, -jnp.inf, logits)
    top2 = jnp.max(rest, axis=-1, keepdims=True)
    i2 = jnp.min(jnp.where(rest == top2, lane, float(LANE)), axis=-1, keepdims=True)
    e2 = jnp.exp(top2 - top1)
    inv = 1.0 / (1.0 + e2)
    o_ref[...] = jnp.where(lane == i1, inv, 0.0) + jnp.where(lane == i2, e2 * inv, 0.0)


def _router(t, router_w, router_b, cfg):
    M, D = t.shape
    E = cfg.n_experts
    assert cfg.top_k == 2 and E <= LANE
    w = jnp.zeros((D, LANE), F32).at[:, :E].set(router_w)
    b = jnp.full((1, LANE), NEG_INF, F32).at[0, :E].set(router_b)
    bm = _pick_block(M, 1152, SUBLANE_BF16)
    return pl.pallas_call(
        _router_kernel,
        out_shape=jax.ShapeDtypeStruct((M, LANE), F32),
        grid=(M // bm,),
        in_specs=[
            pl.BlockSpec((bm, D), lambda m: (m, 0)),
            pl.BlockSpec((D, LANE), lambda m: (0, 0)),
            pl.BlockSpec((1, LANE), lambda m: (0, 0)),
        ],
        out_specs=pl.BlockSpec((bm, LANE), lambda m: (m, 0)),
        compiler_params=_cparams(1),
        name="moe_router",
    )(t, w, b)


def _na_variant_groups(n_groups):
    return (0, 1 if n_groups > 2 else 0, n_groups - 1)


def _na_build_bias(tp_ref, bias_ref, *, rows, W, kh):
    n_groups = rows // kh
    lane = lax.broadcasted_iota(jnp.int32, (W, 2 * W), 1)
    neg = jnp.full((W, 2 * W), NEG_INF, F32)
    for variant, g in enumerate(_na_variant_groups(n_groups)):
        key_row = min(max(g * kh - kh // 2, 0), rows - 2 * kh)
        for ql in range(kh):
            qi = g * kh + ql
            r0 = min(max(qi - kh // 2, 0), rows - kh)
            for j in range(kh):
                ki = key_row + 2 * j
                first_ok = r0 <= ki < r0 + kh
                second_ok = r0 <= ki + 1 < r0 + kh
                if not (first_ok or second_ok):
                    blk = neg
                else:
                    blk = tp_ref[ki - qi + kh]
                    if not second_ok:
                        blk = jnp.where(lane < W, blk, NEG_INF)
                    elif not first_ok:
                        blk = jnp.where(lane >= W, blk, NEG_INF)
                bias_ref[variant, ql * W:(ql + 1) * W, j * 2 * W:(j + 1) * 2 * W] = blk


def _na_kernel(q_ref, k_ref, v_ref, ck_ref, cv_ref, tp_ref, o_ref, bias_ref, *, rows, W, kh, unroll):
    @pl.when(pl.program_id(1) == 0)
    def _():
        _na_build_bias(tp_ref, bias_ref, rows=rows, W=W, kh=kh)

    ck = ck_ref[...]
    cv = cv_ref[...]
    n_groups = rows // kh
    gq, gk = kh * W, 2 * kh * W

    def one_group(g):
        variant = jnp.where(g == 0, 0, jnp.where(g == n_groups - 1, 2, 1))
        key_row = jnp.clip(g * kh - kh // 2, 0, rows - 2 * kh)
        q = q_ref[pl.ds(pl.multiple_of(g * gq, gq), gq), :]
        kstart = pl.multiple_of(key_row * W, W)
        s = _dot_t(q, k_ref[pl.ds(kstart, gk), :]) + bias_ref[variant]
        sc = _dot_t(q, ck)
        m = jnp.maximum(jnp.max(s, axis=-1, keepdims=True), jnp.max(sc, axis=-1, keepdims=True))
        e = jnp.exp2(s - m)
        ec = jnp.exp2(sc - m)
        inv = 1.0 / (jnp.sum(e, axis=-1, keepdims=True) + jnp.sum(ec, axis=-1, keepdims=True))
        o = (_dot(e.astype(BF16), v_ref[pl.ds(kstart, gk), :]) + _dot(ec.astype(BF16), cv)) * inv
        return o.astype(BF16)

    def body(it, carry):
        outs = [one_group(it * unroll + u) for u in range(unroll)]
        start = pl.multiple_of(it * (unroll * gq), unroll * gq)
        o_ref[pl.ds(start, unroll * gq), :] = jnp.concatenate(outs, axis=0)
        return carry

    lax.fori_loop(0, n_groups // unroll, body, 0)


def _na_bias_blocks(rpb, cfg):
    W, kw, wh = cfg.grid_w, cfg.na_win_w, cfg.na_win_h
    H = rpb.shape[0]
    wq = jnp.arange(W)[:, None]
    kc = jnp.arange(W)[None, :]
    c0 = jnp.clip(wq - kw // 2, 0, W - kw)
    col_valid = (kc >= c0) & (kc < c0 + kw)
    cidx = jnp.clip(kc - wq + (kw - 1), 0, 2 * kw - 2)
    toe = jnp.where(col_valid[None, None], rpb[:, :, cidx], NEG_INF).astype(F32)
    pad = jnp.full((H, 1, W, W), NEG_INF, F32)
    toe = jnp.concatenate([pad, toe, pad], axis=1)
    return jnp.concatenate([toe[:, :-1], toe[:, 1:]], axis=-1) * LOG2E


def _na_attention(qkv, rpb, n_out_rows, cfg):
    B, S, L, D, W, Dh = cfg.B, cfg.S, cfg.L, cfg.D, cfg.grid_w, cfg.head_dim
    H = D // Dh
    rows = S // W
    kh = min(cfg.na_win_h, rows)
    n_groups = rows // kh
    assert rows % kh == 0 and n_groups >= 2 and kh == cfg.na_win_h and 2 * W == LANE
    blocks = _na_bias_blocks(rpb, cfg)
    cblk = cfg.MX // L
    return pl.pallas_call(
        functools.partial(_na_kernel, rows=rows, W=W, kh=kh, unroll=math.gcd(rows // kh, NA_GROUP_UNROLL)),
        out_shape=jax.ShapeDtypeStruct((n_out_rows, D), BF16),
        grid=(H, B),
        in_specs=[
            pl.BlockSpec((S, Dh), lambda h, b: (b, h)),
            pl.BlockSpec((S, Dh), lambda h, b: (b, H + h)),
            pl.BlockSpec((S, Dh), lambda h, b: (b, 2 * H + h)),
            pl.BlockSpec((L, Dh), lambda h, b: (cblk + b, H + h)),
            pl.BlockSpec((L, Dh), lambda h, b: (cblk + b, 2 * H + h)),
            pl.BlockSpec((None, 2 * kh, W, 2 * W), lambda h, b: (h, 0, 0, 0)),
        ],
        out_specs=pl.BlockSpec((S, Dh), lambda h, b: (b, h)),
        scratch_shapes=[pltpu.VMEM((3, kh * W, 2 * kh * W), F32)],
        compiler_params=_cparams(2),
        name="neighborhood_attention",
    )(qkv, qkv, qkv, qkv, qkv, blocks)


def _gqa_kernel(*refs, G, sub, has_local, has_sink, has_alias, win, S, Dh):
    refs = list(refs)
    sink_ref = refs.pop(0) if has_sink else None
    q_ref = refs.pop(0)
    k_ref, v_ref, mask_ref = (refs.pop(0), refs.pop(0), refs.pop(0)) if has_local else (None, None, None)
    ck_ref, cv_ref = refs.pop(0), refs.pop(0)
    if has_alias:
        refs.pop(0)
    o_ref = refs.pop(0)
    hk = pl.program_id(1)
    n_sub = q_ref.shape[0] // sub
    ck = ck_ref[...]
    cv = cv_ref[...]
    for sb in range(n_sub):
        rows = slice(sb * sub, (sb + 1) * sub)
        if has_local:
            nkw = sub + 2 * win
            blk = pl.program_id(2) * n_sub + sb
            variant = jnp.where(blk == 0, 0, jnp.where(blk == S // sub - 1, 2, 1))
            start = pl.multiple_of(jnp.clip(blk * sub - win, 0, S - nkw), win)
            kw = k_ref[pl.ds(start, nkw), :]
            vw = v_ref[pl.ds(start, nkw), :]
            mask = mask_ref[variant]
        for g in range(G):
            cols = slice(g * Dh, (g + 1) * Dh)
            q = q_ref[rows, cols]
            sc = _dot_t(q, ck)
            m = jnp.max(sc, axis=-1, keepdims=True)
            if has_local:
                s = _dot_t(q, kw) + mask
                m = jnp.maximum(m, jnp.max(s, axis=-1, keepdims=True))
            if has_sink:
                sk = sink_ref[hk, g] * LOG2E
                m = jnp.maximum(m, sk)
            ec = jnp.exp2(sc - m)
            l = jnp.sum(ec, axis=-1, keepdims=True)
            o = _dot(ec.astype(BF16), cv)
            if has_local:
                e = jnp.exp2(s - m)
                l = l + jnp.sum(e, axis=-1, keepdims=True)
                o = o + _dot(e.astype(BF16), vw)
            if has_sink:
                l = l + jnp.exp2(sk - m)
            o_ref[rows, cols] = (o * (1.0 / l)).astype(BF16)


def _band_mask(rows, win, S):
    nkw = rows + 2 * win
    r = jnp.arange(rows)[:, None]
    c = jnp.arange(nkw)[None, :]
    offsets = (0, -win, -2 * win)
    return jnp.stack([jnp.where(jnp.abs(off + c - r) <= win, 0.0, NEG_INF) for off in offsets]).astype(F32)


def _gqa_attention(qkv, sink, q_heads, kv_heads, local, o_prev, n_out_rows, cfg):
    B, S, L, D, Dh = cfg.B, cfg.S, cfg.L, cfg.D, cfg.head_dim
    G = q_heads // kv_heads
    kcol = q_heads
    vcol = q_heads + kv_heads
    cblk = cfg.MX // L
    has_sink = sink is not None
    win = cfg.swa_window
    rows_step = GQA_QUERY_ROWS if local else L
    nq = S // rows_step if local else 1
    qrow = (lambda b, i: b * nq + i) if local else (lambda b, i: cblk + b)
    in_specs, args = [], []
    if has_sink:
        in_specs.append(pl.BlockSpec(memory_space=pltpu.SMEM))
        args.append(sink.reshape(kv_heads, G).astype(F32))
    in_specs.append(pl.BlockSpec((rows_step, G * Dh), lambda b, h, i: (qrow(b, i), h)))
    args.append(qkv)
    sub = math.gcd(rows_step, GQA_SUB_ROWS) if local else L
    if local:
        assert cfg.swa_block == win and S % rows_step == 0 and S // sub >= 2 and win % SUBLANE_BF16 == 0
        nkw = sub + 2 * win
        in_specs += [pl.BlockSpec((S, Dh), lambda b, h, i: (b, kcol + h)),
                     pl.BlockSpec((S, Dh), lambda b, h, i: (b, vcol + h)),
                     pl.BlockSpec((3, sub, nkw), lambda b, h, i: (0, 0, 0))]
        args += [qkv, qkv, _band_mask(sub, win, S)]
    in_specs += [pl.BlockSpec((L, Dh), lambda b, h, i: (cblk + b, kcol + h)),
                 pl.BlockSpec((L, Dh), lambda b, h, i: (cblk + b, vcol + h))]
    args += [qkv, qkv]
    aliases = {}
    if o_prev is not None:
        in_specs.append(pl.BlockSpec(memory_space=pl.ANY))
        aliases = {len(args): 0}
        args.append(o_prev)
    return pl.pallas_call(
        functools.partial(_gqa_kernel, G=G, sub=sub, has_local=local, has_sink=has_sink,
                          has_alias=o_prev is not None, win=win, S=S, Dh=Dh),
        out_shape=jax.ShapeDtypeStruct((n_out_rows, D), BF16),
        grid=(B, kv_heads, nq),
        in_specs=in_specs,
        out_specs=pl.BlockSpec((rows_step, G * Dh), lambda b, h, i: (qrow(b, i), h)),
        input_output_aliases=aliases,
        compiler_params=_cparams(3),
        name="gqa_window_attention" if local else "gqa_context_attention",
    )(*args)


def _diff_kernel(*refs, has_x, has_alias, chunk, row_split, S, L, lambda_init, Dh):
    refs = list(refs)
    lq1, lk1, lq2, lk2, sub_ref, q_ref = refs[:6]
    refs = refs[6:]
    kx_ref, vx_ref = (refs.pop(0), refs.pop(0)) if has_x else (None, None)
    kc_ref, vc_ref = refs.pop(0), refs.pop(0)
    if has_alias:
        refs.pop(0)
    o_ref, e_ref = refs
    lam = (jnp.exp(jnp.sum(lq1[...] * lk1[...], axis=-1, keepdims=True))
           - jnp.exp(jnp.sum(lq2[...] * lk2[...], axis=-1, keepdims=True)) + lambda_init)
    chunks = []
    if has_x:
        chunks += [(kx_ref, vx_ref, c * chunk, chunk, c * chunk) for c in range(S // chunk)]
    off = S if has_x else 0
    chunks.append((kc_ref, vc_ref, 0, L, off))
    rs = q_ref.shape[0] // row_split
    for r in range(row_split):
        rows = slice(r * rs, (r + 1) * rs)
        coef = []
        for mi in range(2):
            sl = slice(mi * Dh, (mi + 1) * Dh)
            q = q_ref[rows, sl]
            mx = jnp.full((rs, 1), -jnp.inf, F32)
            l = jnp.zeros((rs, 1), F32)
            maxes = []
            for k_ref, _, r0, n, e0 in chunks:
                s = _dot_t(q, k_ref[r0:r0 + n, sl])
                mn = jnp.maximum(mx, jnp.max(s, axis=-1, keepdims=True))
                e = jnp.exp2(s - mn)
                l = l * jnp.exp2(mx - mn) + jnp.sum(e, axis=-1, keepdims=True)
                e_ref[mi, rows, e0:e0 + n] = e
                maxes.append(mn)
                mx = mn
            inv = 1.0 / l
            coef.append([jnp.exp2(mc - mx) * inv for mc in maxes])
        acc = jnp.zeros((rs, 2 * Dh), F32)
        for ci, (_, v_ref, r0, n, e0) in enumerate(chunks):
            p = e_ref[0, rows, e0:e0 + n] * coef[0][ci] - e_ref[1, rows, e0:e0 + n] * (lam * coef[1][ci])
            acc = acc + _dot(p.astype(BF16), v_ref[r0:r0 + n, :])
        ms = jnp.mean(acc * acc, axis=-1, keepdims=True)
        y = acc * lax.rsqrt(ms + NORM_EPS) * sub_ref[...] * (1.0 - lambda_init)
        o_ref[rows, :] = y.astype(BF16)


def _diff_attention(qkv, lams, subln, lambda_init, main, o_prev, n_out_rows, cfg):
    B, S, L, D, Dh = cfg.B, cfg.S, cfg.L, cfg.D, cfg.head_dim
    H = D // (2 * Dh)
    HW = 2 * Dh
    cblk = cfg.MX // L
    bq = DIFF_QUERY_ROWS if main else L
    nq = S // bq if main else 1
    nkeys = S + L if main else L
    qrow = (lambda b, i: b * nq + i) if main else (lambda b, i: cblk + b)
    vec = pl.BlockSpec((1, Dh), lambda b, h, i: (0, 0))
    in_specs = [vec, vec, vec, vec,
                pl.BlockSpec((1, HW), lambda b, h, i: (0, 0)),
                pl.BlockSpec((bq, HW), lambda b, h, i: (qrow(b, i), h))]
    args = [v.reshape(1, Dh).astype(F32) for v in lams] + [subln.reshape(1, HW).astype(F32), qkv]
    if main:
        in_specs += [pl.BlockSpec((S, HW), lambda b, h, i: (b, H + h)),
                     pl.BlockSpec((S, HW), lambda b, h, i: (b, 2 * H + h))]
        args += [qkv, qkv]
    in_specs += [pl.BlockSpec((L, HW), lambda b, h, i: (cblk + b, H + h)),
                 pl.BlockSpec((L, HW), lambda b, h, i: (cblk + b, 2 * H + h))]
    args += [qkv, qkv]
    aliases = {}
    if o_prev is not None:
        in_specs.append(pl.BlockSpec(memory_space=pl.ANY))
        aliases = {len(args): 0}
        args.append(o_prev)
    return pl.pallas_call(
        functools.partial(_diff_kernel, has_x=main, has_alias=o_prev is not None,
                          chunk=math.gcd(S, DIFF_KEY_CHUNK), row_split=min(DIFF_ROW_SPLIT, max(bq // LANE, 1)),
                          S=S, L=L,
                          lambda_init=lambda_init, Dh=Dh),
        out_shape=jax.ShapeDtypeStruct((n_out_rows, D), BF16),
        grid=(B, H, nq),
        in_specs=in_specs,
        out_specs=pl.BlockSpec((bq, HW), lambda b, h, i: (qrow(b, i), h)),
        scratch_shapes=[pltpu.VMEM((2, bq, nkeys), F32)],
        input_output_aliases=aliases,
        compiler_params=_cparams(3),
        name="diff_attention" if main else "diff_context_attention",
    )(*args)


def _rope_tables(cfg):
    Dh = cfg.head_dim
    t = jnp.arange(cfg.S, dtype=jnp.int32)
    row = (t // cfg.grid_w).astype(F32)
    col = (t % cfg.grid_w).astype(F32)
    axis_dim = Dh // 2
    inv_freq = ROPE_BASE ** (-jnp.arange(0, axis_dim, 2, dtype=F32) / axis_dim)
    ang_r = row[:, None] * inv_freq[None, :]
    ang_c = col[:, None] * inv_freq[None, :]
    ang = jnp.concatenate([ang_r, ang_r, ang_c, ang_c], axis=-1)
    cos, sin = jnp.cos(ang), jnp.sin(ang)
    quarter = jnp.arange(Dh) // (Dh // 4)
    sin_a = jnp.where(quarter % 2 == 0, -sin, 0.0)
    sin_b = jnp.where(quarter % 2 == 1, sin, 0.0)
    n_ctx = cfg.B * cfg.L

    def full(tab, fill):
        return jnp.concatenate([jnp.tile(tab, (cfg.B, 1)), jnp.full((n_ctx, Dh), fill, F32)], axis=0)

    return full(cos, 1.0), full(sin_a, 0.0), full(sin_b, 0.0)


def _head_gains(q_gain, k_gain, n_q, n_k, n_v_cols, head_dim):
    q_scale = head_dim ** -0.5 * LOG2E
    return jnp.concatenate([jnp.tile(q_gain.astype(F32) * q_scale, n_q), jnp.tile(k_gain.astype(F32), n_k),
                            jnp.ones((n_v_cols,), F32)]).reshape(1, -1)


def _forward(cfg, x, c, ctx, c_ctx, ada_down, ada_up, ada_bias, norm_mix, norm_ffn,
             na_w_qkv, na_w_o, na_q_norm, na_k_norm, na_rpb,
             diff_w_qkv, diff_w_o, diff_q_norm, diff_k_norm,
             diff_lambda_q1, diff_lambda_k1, diff_lambda_q2, diff_lambda_k2, diff_subln,
             swa_w_qkv, swa_w_o, swa_q_norm, swa_k_norm, swa_sink,
             ffn_w_gate, ffn_w_up, ffn_w_down,
             moe_router, moe_router_bias, moe_w_gate, moe_w_up, moe_w_down):
    B, S, L, D, Dh = cfg.B, cfg.S, cfg.L, cfg.D, cfg.head_dim
    M, MX = cfg.M, cfg.MX
    heads = D // Dh
    h = jnp.concatenate([x.reshape(MX, D), ctx.reshape(B * L, D)], axis=0).astype(F32)
    cond = jnp.zeros((MOD_ROWS, D), F32).at[:B].set(c).at[B].set(c_ctx)
    mods_all = _ada_all(cond, ada_down, ada_up, ada_bias)
    rope_tabs = _rope_tables(cfg)

    for i in range(cfg.depth):
        ctx_out = i < cfg.depth - 1
        n_rows = M if ctx_out else MX
        mods = mods_all[i]
        xn = _modulate(h, norm_mix[i], mods, 0, 1, M, cfg)
        kind, j = i % 3, i // 3
        if kind == 0:
            gains = _head_gains(na_q_norm[j], na_k_norm[j], heads, heads, D, Dh)
            qkv = _qkv_proj(xn, na_w_qkv, j, gains, 2 * D, None, cfg)
            o = _na_attention(qkv, na_rpb[j], n_rows, cfg)
            if ctx_out:
                o = _gqa_attention(qkv, None, heads, heads, False, o, n_rows, cfg)
            w_o = na_w_o
        elif kind == 1:
            gains = _head_gains(diff_q_norm[j], diff_k_norm[j], heads, heads, D, Dh)
            qkv = _qkv_proj(xn, diff_w_qkv, j, gains, 2 * D, rope_tabs, cfg)
            lams = (diff_lambda_q1[j], diff_lambda_k1[j], diff_lambda_q2[j], diff_lambda_k2[j])
            lambda_init = 0.8 - 0.6 * math.exp(-0.3 * i)
            o = _diff_attention(qkv, lams, diff_subln[j], lambda_init, True, None, n_rows, cfg)
            if ctx_out:
                o = _diff_attention(qkv, lams, diff_subln[j], lambda_init, False, o, n_rows, cfg)
            w_o = diff_w_o
        else:
            kvh = heads // cfg.swa_group
            gains = _head_gains(swa_q_norm[j], swa_k_norm[j], heads, kvh, kvh * Dh, Dh)
            qkv = _qkv_proj(xn, swa_w_qkv, j, gains, (heads + kvh) * Dh, rope_tabs, cfg)
            o = _gqa_attention(qkv, swa_sink[j], heads, kvh, True, None, n_rows, cfg)
            if ctx_out:
                o = _gqa_attention(qkv, swa_sink[j], heads, kvh, False, o, n_rows, cfg)
            w_o = swa_w_o
        h = _mm_res(o, w_o, j, h, mods, 2, n_rows, cfg)
        t = _modulate(h, norm_ffn[i], mods, 3, 4, n_rows, cfg)
        fj = i // 2
        if i % 2 == 0:
            n_ffn, _, d_ff = ffn_w_gate.shape
            hid = _swiglu_up(t, ffn_w_gate.reshape(n_ffn, 1, D, d_ff), ffn_w_up.reshape(n_ffn, 1, D, d_ff), fj, None)
            w_down = ffn_w_down
        else:
            gates = _router(t, moe_router[fj], moe_router_bias[fj], cfg)
            hid = _swiglu_up(t, moe_w_gate, moe_w_up, fj, gates)
            w_down = moe_w_down.reshape(moe_w_down.shape[0], -1, D)
        h = _mm_res(hid, w_down, fj, h, mods, 5, n_rows, cfg)
    return h.reshape(B, S, D)


_CFG = Cfg(B=2, S=4096, L=256, D=4096, depth=4, grid_w=64, head_dim=128, na_win_h=8, na_win_w=16,
           swa_window=128, swa_block=128, swa_group=4, n_experts=8, top_k=2)


def kernel(x, c, ctx, c_ctx, ada_down, ada_up, ada_bias, norm_mix, norm_ffn, na_w_qkv, na_w_o, na_q_norm, na_k_norm, na_rpb, diff_w_qkv, diff_w_o, diff_q_norm, diff_k_norm, diff_lambda_q1, diff_lambda_k1, diff_lambda_q2, diff_lambda_k2, diff_subln, swa_w_qkv, swa_w_o, swa_q_norm, swa_k_norm, swa_sink, ffn_w_gate, ffn_w_up, ffn_w_down, moe_router, moe_router_bias, moe_w_gate, moe_w_up, moe_w_down):
    return _forward(_CFG, x, c, ctx, c_ctx, ada_down, ada_up, ada_bias, norm_mix, norm_ffn, na_w_qkv, na_w_o, na_q_norm, na_k_norm, na_rpb, diff_w_qkv, diff_w_o, diff_q_norm, diff_k_norm, diff_lambda_q1, diff_lambda_k1, diff_lambda_q2, diff_lambda_k2, diff_subln, swa_w_qkv, swa_w_o, swa_q_norm, swa_k_norm, swa_sink, ffn_w_gate, ffn_w_up, ffn_w_down, moe_router, moe_router_bias, moe_w_gate, moe_w_up, moe_w_down)
```

```python
import functools
import math
from typing import NamedTuple

import jax
import jax.numpy as jnp
from jax import lax
from jax.experimental import pallas as pl
from jax.experimental.pallas import tpu as pltpu

F32 = jnp.float32
BF16 = jnp.bfloat16

LANE = 128
SUBLANE_BF16 = 16
MOD_ROWS = 8
VMEM_LIMIT = 58 * 1024 * 1024
MAX_FULL_K = 4096
K_TILE = 4096
RES_ROW_PARTS = 2
RES_BLOCK_N = 512
RES_BLOCK_M = 1152
RES_ROW_SPLIT = 2
QKV_BLOCK_N = 1024
QKV_ROW_SPLIT = 4
SWIGLU_BLOCK_N = 512
SWIGLU_ROW_SPLIT = 2

NA_GROUP_UNROLL = 8
MODULATE_ROWS = 512
GQA_QUERY_ROWS = 512
DIFF_QUERY_ROWS = 1024
DIFF_KEY_CHUNK = 1024
DIFF_ROW_SPLIT = 4

NORM_EPS = 1e-6
NEG_INF = -1e30
ROPE_BASE = 10000.0
LOG2E = 1.4426950408889634


class Cfg(NamedTuple):
    B: int
    S: int
    L: int
    D: int
    depth: int
    grid_w: int
    head_dim: int
    na_win_h: int
    na_win_w: int
    swa_window: int
    swa_block: int
    swa_group: int
    n_experts: int
    top_k: int

    @property
    def M(self):
        return self.B * (self.S + self.L)

    @property
    def MX(self):
        return self.B * self.S


def _cparams(n_axes):
    return pltpu.CompilerParams(dimension_semantics=("arbitrary",) * n_axes,
                                vmem_limit_bytes=VMEM_LIMIT)


def _pick_block(n, cap, mult):
    best = None
    for d in range(mult, min(n, cap) + 1, mult):
        if n % d == 0:
            best = d
    assert best is not None, (n, cap, mult)
    return best


def _pick_row_block(rows, cap, bk):
    best = None
    for d in range(SUBLANE_BF16, min(rows, cap) + 1, SUBLANE_BF16):
        if rows % d == 0 and bk % (rows // d) == 0 and (bk // (rows // d)) % SUBLANE_BF16 == 0:
            best = d
    assert best is not None, (rows, cap, bk)
    return best


def _dot(a, b):
    return jnp.dot(a, b, preferred_element_type=F32)


def _dot_t(a, b):
    return lax.dot_general(a, b, (((1,), (1,)), ((), ())), preferred_element_type=F32)


def _row_select(vecs_ref, row0, bm, cfg):
    rows = row0 + lax.broadcasted_iota(jnp.int32, (bm, 1), 0)
    g = vecs_ref[cfg.B:cfg.B + 1, :]
    for b in reversed(range(cfg.B)):
        g = jnp.where(rows < (b + 1) * cfg.S, vecs_ref[b:b + 1, :], g)
    return g


def _ada_kernel(c_ref, down_ref, up_ref, b_ref, o_ref):
    c = c_ref[...]
    sc = (c * jax.nn.sigmoid(c)).astype(BF16)
    t = _dot(sc, down_ref[...].astype(BF16))
    o_ref[...] = _dot(t.astype(BF16), up_ref[...].astype(BF16)) + b_ref[...]


def _ada_all(cond, ada_down, ada_up, ada_bias):
    depth, D, R = ada_down.shape
    N = ada_up.shape[2]
    bn = _pick_block(N, 3072, LANE)
    return pl.pallas_call(
        _ada_kernel,
        out_shape=jax.ShapeDtypeStruct((depth, MOD_ROWS, N), F32),
        grid=(depth, N // bn),
        in_specs=[
            pl.BlockSpec((MOD_ROWS, D), lambda l, j: (0, 0)),
            pl.BlockSpec((None, D, R), lambda l, j: (l, 0, 0)),
            pl.BlockSpec((None, R, bn), lambda l, j: (l, 0, j)),
            pl.BlockSpec((None, 1, bn), lambda l, j: (l, 0, j)),
        ],
        out_specs=pl.BlockSpec((None, MOD_ROWS, bn), lambda l, j: (l, 0, j)),
        compiler_params=_cparams(2),
        name="ada_modulation",
    )(cond, ada_down, ada_up, ada_bias.reshape(depth, 1, N))


def _modulate_kernel(x_ref, g_ref, sh_ref, sc_ref, o_ref, *, bm, cfg):
    row0 = pl.program_id(0) * bm
    t = jnp.where(row0 < cfg.MX, row0 // cfg.S, cfg.B)
    x = x_ref[...]
    ms = jnp.mean(x * x, axis=-1, keepdims=True)
    y = x * lax.rsqrt(ms + NORM_EPS) * g_ref[...]
    o_ref[...] = (y * (1 + sc_ref[pl.ds(t, 1), :]) + sh_ref[pl.ds(t, 1), :]).astype(BF16)


def _modulate(h, gain, mods, shift_chunk, scale_chunk, n_rows, cfg):
    D = cfg.D
    bm = _pick_block(math.gcd(cfg.S, cfg.B * cfg.L), MODULATE_ROWS, SUBLANE_BF16)
    return pl.pallas_call(
        functools.partial(_modulate_kernel, bm=bm, cfg=cfg),
        out_shape=jax.ShapeDtypeStruct((n_rows, D), BF16),
        grid=(n_rows // bm,),
        in_specs=[
            pl.BlockSpec((bm, D), lambda i: (i, 0)),
            pl.BlockSpec((1, D), lambda i: (0, 0)),
            pl.BlockSpec((MOD_ROWS, D), lambda i: (0, shift_chunk)),
            pl.BlockSpec((MOD_ROWS, D), lambda i: (0, scale_chunk)),
        ],
        out_specs=pl.BlockSpec((bm, D), lambda i: (i, 0)),
        compiler_params=_cparams(1),
        name="modulate",
    )(h, gain.reshape(1, D), mods, mods)


def _stage_weight(w_ref, wb_ref, t, m, n_blocks):
    ck = w_ref.shape[0]

    @pl.when(t < n_blocks)
    def _():
        wb_ref[t % 2, pl.ds(pl.multiple_of(m * ck, ck), ck), :] = w_ref[...].astype(BF16)


def _qkv_kernel(*refs, n_blocks, n_norm_blocks, rope, bn):
    if rope:
        x_ref, w_ref, g_ref, cos_ref, sa_ref, sb_ref, o_ref, wb_ref = refs
    else:
        x_ref, w_ref, g_ref, o_ref, wb_ref = refs
    t = pl.program_id(0)
    _stage_weight(w_ref, wb_ref, t, pl.program_id(1), n_blocks)

    bm = x_ref.shape[0]
    rs = bm // QKV_ROW_SPLIT

    @pl.when(jnp.logical_and(t > 0, t - 1 < n_norm_blocks))
    def _():
        w = wb_ref.at[(t - 1) % 2]
        for r in range(QKV_ROW_SPLIT):
            rows = slice(r * rs, (r + 1) * rs)
            acc = _dot(x_ref[rows, :], w[...])
            outs = []
            for hh in range(bn // LANE):
                sl = slice(hh * LANE, (hh + 1) * LANE)
                xh = acc[:, sl]
                ms = jnp.mean(xh * xh, axis=-1, keepdims=True)
                y = xh * lax.rsqrt(ms + NORM_EPS) * g_ref[:, sl]
                if rope:
                    y = (y * cos_ref[rows, :] + pltpu.roll(y, 3 * LANE // 4, 1) * sa_ref[rows, :]
                         + pltpu.roll(y, LANE // 4, 1) * sb_ref[rows, :])
                outs.append(y.astype(BF16))
            o_ref[rows, :] = jnp.concatenate(outs, axis=1)

    @pl.when(jnp.logical_and(t > 0, t - 1 >= n_norm_blocks))
    def _():
        o_ref[...] = _dot(x_ref[...], wb_ref[(t - 1) % 2]).astype(BF16)


def _qkv_proj(xn, w_all, layer, gains, n_norm_cols, rope_tabs, cfg):
    M, D = xn.shape
    N = w_all.shape[2]
    bn = math.gcd(math.gcd(N, n_norm_cols), QKV_BLOCK_N)
    bm = _pick_block(M, 1152, SUBLANE_BF16)
    n_m, n_blocks = M // bm, N // bn
    ck = D // n_m
    assert bn % LANE == 0 and D % n_m == 0 and ck % SUBLANE_BF16 == 0
    rope = rope_tabs is not None
    row = lambda t, m: jnp.where(t == 0, 0, m)
    col = lambda t: jnp.maximum(t - 1, 0)
    in_specs = [
        pl.BlockSpec((bm, D), lambda t, m: (row(t, m), 0)),
        pl.BlockSpec((None, ck, bn), lambda t, m: (layer, m, jnp.minimum(t, n_blocks - 1))),
        pl.BlockSpec((1, bn), lambda t, m: (0, col(t))),
    ]
    args = [xn, w_all, gains]
    if rope:
        in_specs += [pl.BlockSpec((bm, LANE), lambda t, m: (row(t, m), 0))] * 3
        args += list(rope_tabs)
    return pl.pallas_call(
        functools.partial(_qkv_kernel, n_blocks=n_blocks, n_norm_blocks=n_norm_cols // bn, rope=rope, bn=bn),
        out_shape=jax.ShapeDtypeStruct((M, N), BF16),
        grid=(n_blocks + 1, n_m),
        in_specs=in_specs,
        out_specs=pl.BlockSpec((bm, bn), lambda t, m: (row(t, m), col(t))),
        scratch_shapes=[pltpu.VMEM((2, D, bn), BF16)],
        compiler_params=_cparams(2),
        name="qkv_proj",
    )(*args)


def _mm_res_kernel(a_ref, w_ref, res_ref, gate_ref, o_ref, wb_ref, *acc, n_blocks, nk, n_m, bm, cfg):
    part = pl.program_id(0)
    t = pl.program_id(1)
    m = pl.program_id(2)
    _stage_weight(w_ref, wb_ref, t, m, n_blocks)

    rs = bm // RES_ROW_SPLIT
    subs = [(slice(r * rs, (r + 1) * rs), r * rs) for r in range(RES_ROW_SPLIT)]

    def epilogue(sub, off, total):
        row0 = (part * n_m + m) * bm + off
        o_ref[sub, :] = res_ref[sub, :] + _row_select(gate_ref, row0, rs, cfg) * total

    @pl.when(t > 0)
    def _():
        w = wb_ref.at[(t - 1) % 2]
        if nk == 1:
            for sub, off in subs:
                epilogue(sub, off, _dot(a_ref[sub, :], w[...]))
        else:
            k = (t - 1) % nk
            acc_ref = acc[0]
            acc_rows = lambda off: pl.ds(pl.multiple_of(m * bm + off, 8), rs)

            @pl.when(k == 0)
            def _():
                for sub, off in subs:
                    acc_ref[acc_rows(off), :] = _dot(a_ref[sub, :], w[...])

            @pl.when(jnp.logical_and(k > 0, k < nk - 1))
            def _():
                for sub, off in subs:
                    acc_ref[acc_rows(off), :] += _dot(a_ref[sub, :], w[...])

            @pl.when(k == nk - 1)
            def _():
                for sub, off in subs:
                    epilogue(sub, off, acc_ref[acc_rows(off), :] + _dot(a_ref[sub, :], w[...]))


def _mm_res(a, w_all, layer, res, mods, gate_chunk, n_rows, cfg):
    _, K, N = w_all.shape
    bn = math.gcd(N, RES_BLOCK_N)
    bk = K if K <= MAX_FULL_K else _pick_block(K, K_TILE, LANE)
    nk = K // bk
    n_parts = RES_ROW_PARTS if nk > 1 else 1
    bm = _pick_row_block(n_rows // n_parts, RES_BLOCK_M, bk)
    n_m = n_rows // n_parts // bm
    n_blocks = (N // bn) * nk
    ck = bk // n_m
    assert n_rows % n_parts == 0 and bk % n_m == 0 and ck % SUBLANE_BF16 == 0
    wblk = lambda t: jnp.minimum(t, n_blocks - 1)
    cblk = lambda t: jnp.maximum(t - 1, 0)
    orow = lambda p, t, m: p * n_m + jnp.where(jnp.logical_and(t > 0, cblk(t) % nk == nk - 1), m, 0)
    scratch = [pltpu.VMEM((2, bk, bn), BF16)]
    if nk > 1:
        scratch.append(pltpu.VMEM((n_rows // n_parts, bn), F32))
    return pl.pallas_call(
        functools.partial(_mm_res_kernel, n_blocks=n_blocks, nk=nk, n_m=n_m, bm=bm, cfg=cfg),
        out_shape=jax.ShapeDtypeStruct((n_rows, N), F32),
        grid=(n_parts, n_blocks + 1, n_m),
        in_specs=[
            pl.BlockSpec((bm, bk), lambda p, t, m: (p * n_m + jnp.where(t == 0, 0, m), cblk(t) % nk)),
            pl.BlockSpec((None, ck, bn), lambda p, t, m: (layer, (wblk(t) % nk) * n_m + m, wblk(t) // nk)),
            pl.BlockSpec((bm, bn), lambda p, t, m: (orow(p, t, m), cblk(t) // nk)),
            pl.BlockSpec((MOD_ROWS, bn), lambda p, t, m: (0, gate_chunk * (N // bn) + cblk(t) // nk)),
        ],
        out_specs=pl.BlockSpec((bm, bn), lambda p, t, m: (orow(p, t, m), cblk(t) // nk)),
        scratch_shapes=scratch,
        compiler_params=_cparams(3),
        name="matmul_residual",
    )(a, w_all, res, mods)


def _swiglu_kernel(*refs, gated, n_blocks, nbe):
    if gated:
        x_ref, wg_ref, wu_ref, gates_ref, o_ref, wgb_ref, wub_ref = refs
    else:
        x_ref, wg_ref, wu_ref, o_ref, wgb_ref, wub_ref = refs
    t = pl.program_id(0)
    m = pl.program_id(1)
    _stage_weight(wg_ref, wgb_ref, t, m, n_blocks)
    _stage_weight(wu_ref, wub_ref, t, m, n_blocks)

    rs = x_ref.shape[0] // SWIGLU_ROW_SPLIT

    @pl.when(t > 0)
    def _():
        j = t - 1
        wg = wgb_ref.at[j % 2]
        wu = wub_ref.at[j % 2]
        for r in range(SWIGLU_ROW_SPLIT):
            rows = slice(r * rs, (r + 1) * rs)
            x = x_ref[rows, :]
            g = _dot(x, wg[...])
            u = _dot(x, wu[...])
            hid = g * jax.nn.sigmoid(g) * u
            if gated:
                gates = gates_ref[rows, :]
                lane = lax.broadcasted_iota(jnp.int32, gates.shape, 1)
                hid = hid * jnp.sum(jnp.where(lane == j // nbe, gates, 0.0), axis=-1, keepdims=True)
            o_ref[rows, :] = hid.astype(BF16)


def _swiglu_up(t, w_gate, w_up, layer, gates):
    M, D = t.shape
    _, E, _, F = w_gate.shape
    bn = math.gcd(F, SWIGLU_BLOCK_N)
    nbe = F // bn
    bm = _pick_block(M, 1152, SUBLANE_BF16)
    n_m, n_blocks = M // bm, E * nbe
    ck = D // n_m
    assert D % n_m == 0 and ck % SUBLANE_BF16 == 0
    gated = gates is not None
    row = lambda t, m: jnp.where(t == 0, 0, m)
    col = lambda t: jnp.maximum(t - 1, 0)
    wblk = lambda t: jnp.minimum(t, n_blocks - 1)
    w_spec = pl.BlockSpec((None, None, ck, bn), lambda t, m: (layer, wblk(t) // nbe, m, wblk(t) % nbe))
    in_specs = [pl.BlockSpec((bm, D), lambda t, m: (row(t, m), 0)), w_spec, w_spec]
    args = [t, w_gate, w_up]
    if gated:
        in_specs.append(pl.BlockSpec((bm, LANE), lambda t, m: (row(t, m), 0)))
        args.append(gates)
    return pl.pallas_call(
        functools.partial(_swiglu_kernel, gated=gated, n_blocks=n_blocks, nbe=nbe),
        out_shape=jax.ShapeDtypeStruct((M, E * F), BF16),
        grid=(n_blocks + 1, n_m),
        in_specs=in_specs,
        out_specs=pl.BlockSpec((bm, bn), lambda t, m: (row(t, m), col(t))),
        scratch_shapes=[pltpu.VMEM((2, D, bn), BF16), pltpu.VMEM((2, D, bn), BF16)],
        compiler_params=_cparams(2),
        name="swiglu_up",
    )(*args)


def _router_kernel(t_ref, w_ref, b_ref, o_ref):
    logits = _dot(t_ref[...], w_ref[...].astype(BF16)) + b_ref[...]
    lane = lax.broadcasted_iota(jnp.int32, logits.shape, 1).astype(F32)
    top1 = jnp.max(logits, axis=-1, keepdims=True)
    i1 = jnp.min(jnp.where(logits == top1, lane, float(LANE)), axis=-1, keepdims=True)
    rest = jnp.where(lane == i1, -jnp.inf, logits)
    top2 = jnp.max(rest, axis=-1, keepdims=True)
    i2 = jnp.min(jnp.where(rest == top2, lane, float(LANE)), axis=-1, keepdims=True)
    e2 = jnp.exp(top2 - top1)
    inv = 1.0 / (1.0 + e2)
    o_ref[...] = jnp.where(lane == i1, inv, 0.0) + jnp.where(lane == i2, e2 * inv, 0.0)


def _router(t, router_w, router_b, cfg):
    M, D = t.shape
    E = cfg.n_experts
    assert cfg.top_k == 2 and E <= LANE
    w = jnp.zeros((D, LANE), F32).at[:, :E].set(router_w)
    b = jnp.full((1, LANE), NEG_INF, F32).at[0, :E].set(router_b)
    bm = _pick_block(M, 1152, SUBLANE_BF16)
    return pl.pallas_call(
        _router_kernel,
        out_shape=jax.ShapeDtypeStruct((M, LANE), F32),
        grid=(M // bm,),
        in_specs=[
            pl.BlockSpec((bm, D), lambda m: (m, 0)),
            pl.BlockSpec((D, LANE), lambda m: (0, 0)),
            pl.BlockSpec((1, LANE), lambda m: (0, 0)),
        ],
        out_specs=pl.BlockSpec((bm, LANE), lambda m: (m, 0)),
        compiler_params=_cparams(1),
        name="moe_router",
    )(t, w, b)


def _na_variant_groups(n_groups):
    return (0, 1 if n_groups > 2 else 0, n_groups - 1)


def _na_build_bias(tp_ref, bias_ref, *, rows, W, kh):
    n_groups = rows // kh
    lane = lax.broadcasted_iota(jnp.int32, (W, 2 * W), 1)
    neg = jnp.full((W, 2 * W), NEG_INF, F32)
    for variant, g in enumerate(_na_variant_groups(n_groups)):
        key_row = min(max(g * kh - kh // 2, 0), rows - 2 * kh)
        for ql in range(kh):
            qi = g * kh + ql
            r0 = min(max(qi - kh // 2, 0), rows - kh)
            for j in range(kh):
                ki = key_row + 2 * j
                first_ok = r0 <= ki < r0 + kh
                second_ok = r0 <= ki + 1 < r0 + kh
                if not (first_ok or second_ok):
                    blk = neg
                else:
                    blk = tp_ref[ki - qi + kh]
                    if not second_ok:
                        blk = jnp.where(lane < W, blk, NEG_INF)
                    elif not first_ok:
                        blk = jnp.where(lane >= W, blk, NEG_INF)
                bias_ref[variant, ql * W:(ql + 1) * W, j * 2 * W:(j + 1) * 2 * W] = blk


def _na_kernel(q_ref, k_ref, v_ref, ck_ref, cv_ref, tp_ref, o_ref, bias_ref, *, rows, W, kh, unroll):
    @pl.when(pl.program_id(1) == 0)
    def _():
        _na_build_bias(tp_ref, bias_ref, rows=rows, W=W, kh=kh)

    ck = ck_ref[...]
    cv = cv_ref[...]
    n_groups = rows // kh
    gq, gk = kh * W, 2 * kh * W

    def one_group(g):
        variant = jnp.where(g == 0, 0, jnp.where(g == n_groups - 1, 2, 1))
        key_row = jnp.clip(g * kh - kh // 2, 0, rows - 2 * kh)
        q = q_ref[pl.ds(pl.multiple_of(g * gq, gq), gq), :]
        kstart = pl.multiple_of(key_row * W, W)
        s = _dot_t(q, k_ref[pl.ds(kstart, gk), :]) + bias_ref[variant]
        sc = _dot_t(q, ck)
        m = jnp.maximum(jnp.max(s, axis=-1, keepdims=True), jnp.max(sc, axis=-1, keepdims=True))
        e = jnp.exp2(s - m)
        ec = jnp.exp2(sc - m)
        inv = 1.0 / (jnp.sum(e, axis=-1, keepdims=True) + jnp.sum(ec, axis=-1, keepdims=True))
        o = (_dot(e.astype(BF16), v_ref[pl.ds(kstart, gk), :]) + _dot(ec.astype(BF16), cv)) * inv
        return o.astype(BF16)

    def body(it, carry):
        outs = [one_group(it * unroll + u) for u in range(unroll)]
        start = pl.multiple_of(it * (unroll * gq), unroll * gq)
        o_ref[pl.ds(start, unroll * gq), :] = jnp.concatenate(outs, axis=0)
        return carry

    lax.fori_loop(0, n_groups // unroll, body, 0)


def _na_bias_blocks(rpb, cfg):
    W, kw, wh = cfg.grid_w, cfg.na_win_w, cfg.na_win_h
    H = rpb.shape[0]
    wq = jnp.arange(W)[:, None]
    kc = jnp.arange(W)[None, :]
    c0 = jnp.clip(wq - kw // 2, 0, W - kw)
    col_valid = (kc >= c0) & (kc < c0 + kw)
    cidx = jnp.clip(kc - wq + (kw - 1), 0, 2 * kw - 2)
    toe = jnp.where(col_valid[None, None], rpb[:, :, cidx], NEG_INF).astype(F32)
    pad = jnp.full((H, 1, W, W), NEG_INF, F32)
    toe = jnp.concatenate([pad, toe, pad], axis=1)
    return jnp.concatenate([toe[:, :-1], toe[:, 1:]], axis=-1) * LOG2E


def _na_attention(qkv, rpb, n_out_rows, cfg):
    B, S, L, D, W, Dh = cfg.B, cfg.S, cfg.L, cfg.D, cfg.grid_w, cfg.head_dim
    H = D // Dh
    rows = S // W
    kh = min(cfg.na_win_h, rows)
    n_groups = rows // kh
    assert rows % kh == 0 and n_groups >= 2 and kh == cfg.na_win_h and 2 * W == LANE
    blocks = _na_bias_blocks(rpb, cfg)
    cblk = cfg.MX // L
    return pl.pallas_call(
        functools.partial(_na_kernel, rows=rows, W=W, kh=kh, unroll=math.gcd(rows // kh, NA_GROUP_UNROLL)),
        out_shape=jax.ShapeDtypeStruct((n_out_rows, D), BF16),
        grid=(H, B),
        in_specs=[
            pl.BlockSpec((S, Dh), lambda h, b: (b, h)),
            pl.BlockSpec((S, Dh), lambda h, b: (b, H + h)),
            pl.BlockSpec((S, Dh), lambda h, b: (b, 2 * H + h)),
            pl.BlockSpec((L, Dh), lambda h, b: (cblk + b, H + h)),
            pl.BlockSpec((L, Dh), lambda h, b: (cblk + b, 2 * H + h)),
            pl.BlockSpec((None, 2 * kh, W, 2 * W), lambda h, b: (h, 0, 0, 0)),
        ],
        out_specs=pl.BlockSpec((S, Dh), lambda h, b: (b, h)),
        scratch_shapes=[pltpu.VMEM((3, kh * W, 2 * kh * W), F32)],
        compiler_params=_cparams(2),
        name="neighborhood_attention",
    )(qkv, qkv, qkv, qkv, qkv, blocks)


def _gqa_kernel(*refs, G, has_local, has_sink, has_alias, win, S, Dh):
    refs = list(refs)
    sink_ref = refs.pop(0) if has_sink else None
    q_ref = refs.pop(0)
    k_ref, v_ref, mask_ref = (refs.pop(0), refs.pop(0), refs.pop(0)) if has_local else (None, None, None)
    ck_ref, cv_ref = refs.pop(0), refs.pop(0)
    if has_alias:
        refs.pop(0)
    o_ref = refs.pop(0)
    hk = pl.program_id(1)
    rows = q_ref.shape[0]
    ck = ck_ref[...]
    cv = cv_ref[...]
    if has_local:
        nkw = rows + 2 * win
        start = pl.multiple_of(jnp.clip(pl.program_id(2) * rows - win, 0, S - nkw), win)
        kw = k_ref[pl.ds(start, nkw), :]
        vw = v_ref[pl.ds(start, nkw), :]
    outs = []
    for g in range(G):
        q = q_ref[:, g * Dh:(g + 1) * Dh]
        sc = _dot_t(q, ck)
        m = jnp.max(sc, axis=-1, keepdims=True)
        if has_local:
            s = _dot_t(q, kw) + mask_ref[...]
            m = jnp.maximum(m, jnp.max(s, axis=-1, keepdims=True))
        if has_sink:
            sk = sink_ref[hk, g] * LOG2E
            m = jnp.maximum(m, sk)
        ec = jnp.exp2(sc - m)
        l = jnp.sum(ec, axis=-1, keepdims=True)
        o = _dot(ec.astype(BF16), cv)
        if has_local:
            e = jnp.exp2(s - m)
            l = l + jnp.sum(e, axis=-1, keepdims=True)
            o = o + _dot(e.astype(BF16), vw)
        if has_sink:
            l = l + jnp.exp2(sk - m)
        outs.append((o * (1.0 / l)).astype(BF16))
    o_ref[...] = jnp.concatenate(outs, axis=1)


def _band_mask(rows, win, S):
    nkw = rows + 2 * win
    r = jnp.arange(rows)[:, None]
    c = jnp.arange(nkw)[None, :]
    offsets = (0, -win, -2 * win)
    return jnp.stack([jnp.where(jnp.abs(off + c - r) <= win, 0.0, NEG_INF) for off in offsets]).astype(F32)


def _gqa_attention(qkv, sink, q_heads, kv_heads, local, o_prev, n_out_rows, cfg):
    B, S, L, D, Dh = cfg.B, cfg.S, cfg.L, cfg.D, cfg.head_dim
    G = q_heads // kv_heads
    kcol = q_heads
    vcol = q_heads + kv_heads
    cblk = cfg.MX // L
    has_sink = sink is not None
    win = cfg.swa_window
    rows_step = GQA_QUERY_ROWS if local else L
    nq = S // rows_step if local else 1
    qrow = (lambda b, i: b * nq + i) if local else (lambda b, i: cblk + b)
    in_specs, args = [], []
    if has_sink:
        in_specs.append(pl.BlockSpec(memory_space=pltpu.SMEM))
        args.append(sink.reshape(kv_heads, G).astype(F32))
    in_specs.append(pl.BlockSpec((rows_step, G * Dh), lambda b, h, i: (qrow(b, i), h)))
    args.append(qkv)
    if local:
        assert cfg.swa_block == win and S % rows_step == 0 and nq >= 2 and win % SUBLANE_BF16 == 0
        nkw = rows_step + 2 * win
        variant = lambda i: jnp.where(i == 0, 0, jnp.where(i == nq - 1, 2, 1))
        in_specs += [pl.BlockSpec((S, Dh), lambda b, h, i: (b, kcol + h)),
                     pl.BlockSpec((S, Dh), lambda b, h, i: (b, vcol + h)),
                     pl.BlockSpec((None, rows_step, nkw), lambda b, h, i: (variant(i), 0, 0))]
        args += [qkv, qkv, _band_mask(rows_step, win, S)]
    in_specs += [pl.BlockSpec((L, Dh), lambda b, h, i: (cblk + b, kcol + h)),
                 pl.BlockSpec((L, Dh), lambda b, h, i: (cblk + b, vcol + h))]
    args += [qkv, qkv]
    aliases = {}
    if o_prev is not None:
        in_specs.append(pl.BlockSpec(memory_space=pl.ANY))
        aliases = {len(args): 0}
        args.append(o_prev)
    return pl.pallas_call(
        functools.partial(_gqa_kernel, G=G, has_local=local, has_sink=has_sink,
                          has_alias=o_prev is not None, win=win, S=S, Dh=Dh),
        out_shape=jax.ShapeDtypeStruct((n_out_rows, D), BF16),
        grid=(B, kv_heads, nq),
        in_specs=in_specs,
        out_specs=pl.BlockSpec((rows_step, G * Dh), lambda b, h, i: (qrow(b, i), h)),
        input_output_aliases=aliases,
        compiler_params=_cparams(3),
        name="gqa_window_attention" if local else "gqa_context_attention",
    )(*args)


def _diff_kernel(*refs, has_x, has_alias, chunk, row_split, S, L, lambda_init, Dh):
    refs = list(refs)
    lq1, lk1, lq2, lk2, sub_ref, q_ref = refs[:6]
    refs = refs[6:]
    kx_ref, vx_ref = (refs.pop(0), refs.pop(0)) if has_x else (None, None)
    kc_ref, vc_ref = refs.pop(0), refs.pop(0)
    if has_alias:
        refs.pop(0)
    o_ref, e_ref = refs
    lam = (jnp.exp(jnp.sum(lq1[...] * lk1[...], axis=-1, keepdims=True))
           - jnp.exp(jnp.sum(lq2[...] * lk2[...], axis=-1, keepdims=True)) + lambda_init)
    chunks = []
    if has_x:
        chunks += [(kx_ref, vx_ref, c * chunk, chunk, c * chunk) for c in range(S // chunk)]
    off = S if has_x else 0
    chunks.append((kc_ref, vc_ref, 0, L, off))
    rs = q_ref.shape[0] // row_split
    for r in range(row_split):
        rows = slice(r * rs, (r + 1) * rs)
        coef = []
        for mi in range(2):
            sl = slice(mi * Dh, (mi + 1) * Dh)
            q = q_ref[rows, sl]
            mx = jnp.full((rs, 1), -jnp.inf, F32)
            l = jnp.zeros((rs, 1), F32)
            maxes = []
            for k_ref, _, r0, n, e0 in chunks:
                s = _dot_t(q, k_ref[r0:r0 + n, sl])
                mn = jnp.maximum(mx, jnp.max(s, axis=-1, keepdims=True))
                e = jnp.exp2(s - mn)
                l = l * jnp.exp2(mx - mn) + jnp.sum(e, axis=-1, keepdims=True)
                e_ref[mi, rows, e0:e0 + n] = e
                maxes.append(mn)
                mx = mn
            inv = 1.0 / l
            coef.append([jnp.exp2(mc - mx) * inv for mc in maxes])
        acc = jnp.zeros((rs, 2 * Dh), F32)
        for ci, (_, v_ref, r0, n, e0) in enumerate(chunks):
            p = e_ref[0, rows, e0:e0 + n] * coef[0][ci] - e_ref[1, rows, e0:e0 + n] * (lam * coef[1][ci])
            acc = acc + _dot(p.astype(BF16), v_ref[r0:r0 + n, :])
        ms = jnp.mean(acc * acc, axis=-1, keepdims=True)
        y = acc * lax.rsqrt(ms + NORM_EPS) * sub_ref[...] * (1.0 - lambda_init)
        o_ref[rows, :] = y.astype(BF16)


def _diff_attention(qkv, lams, subln, lambda_init, main, o_prev, n_out_rows, cfg):
    B, S, L, D, Dh = cfg.B, cfg.S, cfg.L, cfg.D, cfg.head_dim
    H = D // (2 * Dh)
    HW = 2 * Dh
    cblk = cfg.MX // L
    bq = DIFF_QUERY_ROWS if main else L
    nq = S // bq if main else 1
    nkeys = S + L if main else L
    qrow = (lambda b, i: b * nq + i) if main else (lambda b, i: cblk + b)
    vec = pl.BlockSpec((1, Dh), lambda b, h, i: (0, 0))
    in_specs = [vec, vec, vec, vec,
                pl.BlockSpec((1, HW), lambda b, h, i: (0, 0)),
                pl.BlockSpec((bq, HW), lambda b, h, i: (qrow(b, i), h))]
    args = [v.reshape(1, Dh).astype(F32) for v in lams] + [subln.reshape(1, HW).astype(F32), qkv]
    if main:
        in_specs += [pl.BlockSpec((S, HW), lambda b, h, i: (b, H + h)),
                     pl.BlockSpec((S, HW), lambda b, h, i: (b, 2 * H + h))]
        args += [qkv, qkv]
    in_specs += [pl.BlockSpec((L, HW), lambda b, h, i: (cblk + b, H + h)),
                 pl.BlockSpec((L, HW), lambda b, h, i: (cblk + b, 2 * H + h))]
    args += [qkv, qkv]
    aliases = {}
    if o_prev is not None:
        in_specs.append(pl.BlockSpec(memory_space=pl.ANY))
        aliases = {len(args): 0}
        args.append(o_prev)
    return pl.pallas_call(
        functools.partial(_diff_kernel, has_x=main, has_alias=o_prev is not None,
                          chunk=math.gcd(S, DIFF_KEY_CHUNK), row_split=min(DIFF_ROW_SPLIT, max(bq // LANE, 1)),
                          S=S, L=L,
                          lambda_init=lambda_init, Dh=Dh),
        out_shape=jax.ShapeDtypeStruct((n_out_rows, D), BF16),
        grid=(B, H, nq),
        in_specs=in_specs,
        out_specs=pl.BlockSpec((bq, HW), lambda b, h, i: (qrow(b, i), h)),
        scratch_shapes=[pltpu.VMEM((2, bq, nkeys), F32)],
        input_output_aliases=aliases,
        compiler_params=_cparams(3),
        name="diff_attention" if main else "diff_context_attention",
    )(*args)


def _rope_tables(cfg):
    Dh = cfg.head_dim
    t = jnp.arange(cfg.S, dtype=jnp.int32)
    row = (t // cfg.grid_w).astype(F32)
    col = (t % cfg.grid_w).astype(F32)
    axis_dim = Dh // 2
    inv_freq = ROPE_BASE ** (-jnp.arange(0, axis_dim, 2, dtype=F32) / axis_dim)
    ang_r = row[:, None] * inv_freq[None, :]
    ang_c = col[:, None] * inv_freq[None, :]
    ang = jnp.concatenate([ang_r, ang_r, ang_c, ang_c], axis=-1)
    cos, sin = jnp.cos(ang), jnp.sin(ang)
    quarter = jnp.arange(Dh) // (Dh // 4)
    sin_a = jnp.where(quarter % 2 == 0, -sin, 0.0)
    sin_b = jnp.where(quarter % 2 == 1, sin, 0.0)
    n_ctx = cfg.B * cfg.L

    def full(tab, fill):
        return jnp.concatenate([jnp.tile(tab, (cfg.B, 1)), jnp.full((n_ctx, Dh), fill, F32)], axis=0)

    return full(cos, 1.0), full(sin_a, 0.0), full(sin_b, 0.0)


def _head_gains(q_gain, k_gain, n_q, n_k, n_v_cols, head_dim):
    q_scale = head_dim ** -0.5 * LOG2E
    return jnp.concatenate([jnp.tile(q_gain.astype(F32) * q_scale, n_q), jnp.tile(k_gain.astype(F32), n_k),
                            jnp.ones((n_v_cols,), F32)]).reshape(1, -1)


def _forward(cfg, x, c, ctx, c_ctx, ada_down, ada_up, ada_bias, norm_mix, norm_ffn,
             na_w_qkv, na_w_o, na_q_norm, na_k_norm, na_rpb,
             diff_w_qkv, diff_w_o, diff_q_norm, diff_k_norm,
             diff_lambda_q1, diff_lambda_k1, diff_lambda_q2, diff_lambda_k2, diff_subln,
             swa_w_qkv, swa_w_o, swa_q_norm, swa_k_norm, swa_sink,
             ffn_w_gate, ffn_w_up, ffn_w_down,
             moe_router, moe_router_bias, moe_w_gate, moe_w_up, moe_w_down):
    B, S, L, D, Dh = cfg.B, cfg.S, cfg.L, cfg.D, cfg.head_dim
    M, MX = cfg.M, cfg.MX
    heads = D // Dh
    h = jnp.concatenate([x.reshape(MX, D), ctx.reshape(B * L, D)], axis=0).astype(F32)
    cond = jnp.zeros((MOD_ROWS, D), F32).at[:B].set(c).at[B].set(c_ctx)
    mods_all = _ada_all(cond, ada_down, ada_up, ada_bias)
    rope_tabs = _rope_tables(cfg)

    for i in range(cfg.depth):
        ctx_out = i < cfg.depth - 1
        n_rows = M if ctx_out else MX
        mods = mods_all[i]
        xn = _modulate(h, norm_mix[i], mods, 0, 1, M, cfg)
        kind, j = i % 3, i // 3
        if kind == 0:
            gains = _head_gains(na_q_norm[j], na_k_norm[j], heads, heads, D, Dh)
            qkv = _qkv_proj(xn, na_w_qkv, j, gains, 2 * D, None, cfg)
            o = _na_attention(qkv, na_rpb[j], n_rows, cfg)
            if ctx_out:
                o = _gqa_attention(qkv, None, heads, heads, False, o, n_rows, cfg)
            w_o = na_w_o
        elif kind == 1:
            gains = _head_gains(diff_q_norm[j], diff_k_norm[j], heads, heads, D, Dh)
            qkv = _qkv_proj(xn, diff_w_qkv, j, gains, 2 * D, rope_tabs, cfg)
            lams = (diff_lambda_q1[j], diff_lambda_k1[j], diff_lambda_q2[j], diff_lambda_k2[j])
            lambda_init = 0.8 - 0.6 * math.exp(-0.3 * i)
            o = _diff_attention(qkv, lams, diff_subln[j], lambda_init, True, None, n_rows, cfg)
            if ctx_out:
                o = _diff_attention(qkv, lams, diff_subln[j], lambda_init, False, o, n_rows, cfg)
            w_o = diff_w_o
        else:
            kvh = heads // cfg.swa_group
            gains = _head_gains(swa_q_norm[j], swa_k_norm[j], heads, kvh, kvh * Dh, Dh)
            qkv = _qkv_proj(xn, swa_w_qkv, j, gains, (heads + kvh) * Dh, rope_tabs, cfg)
            o = _gqa_attention(qkv, swa_sink[j], heads, kvh, True, None, n_rows, cfg)
            if ctx_out:
                o = _gqa_attention(qkv, swa_sink[j], heads, kvh, False, o, n_rows, cfg)
            w_o = swa_w_o
        h = _mm_res(o, w_o, j, h, mods, 2, n_rows, cfg)
        t = _modulate(h, norm_ffn[i], mods, 3, 4, n_rows, cfg)
        fj = i // 2
        if i % 2 == 0:
            n_ffn, _, d_ff = ffn_w_gate.shape
            hid = _swiglu_up(t, ffn_w_gate.reshape(n_ffn, 1, D, d_ff), ffn_w_up.reshape(n_ffn, 1, D, d_ff), fj, None)
            w_down = ffn_w_down
        else:
            gates = _router(t, moe_router[fj], moe_router_bias[fj], cfg)
            hid = _swiglu_up(t, moe_w_gate, moe_w_up, fj, gates)
            w_down = moe_w_down.reshape(moe_w_down.shape[0], -1, D)
        h = _mm_res(hid, w_down, fj, h, mods, 5, n_rows, cfg)
    return h.reshape(B, S, D)


_CFG = Cfg(B=2, S=4096, L=256, D=4096, depth=4, grid_w=64, head_dim=128, na_win_h=8, na_win_w=16,
           swa_window=128, swa_block=128, swa_group=4, n_experts=8, top_k=2)


def kernel(x, c, ctx, c_ctx, ada_down, ada_up, ada_bias, norm_mix, norm_ffn, na_w_qkv, na_w_o, na_q_norm, na_k_norm, na_rpb, diff_w_qkv, diff_w_o, diff_q_norm, diff_k_norm, diff_lambda_q1, diff_lambda_k1, diff_lambda_q2, diff_lambda_k2, diff_subln, swa_w_qkv, swa_w_o, swa_q_norm, swa_k_norm, swa_sink, ffn_w_gate, ffn_w_up, ffn_w_down, moe_router, moe_router_bias, moe_w_gate, moe_w_up, moe_w_down):
    return _forward(_CFG, x, c, ctx, c_ctx, ada_down, ada_up, ada_bias, norm_mix, norm_ffn, na_w_qkv, na_w_o, na_q_norm, na_k_norm, na_rpb, diff_w_qkv, diff_w_o, diff_q_norm, diff_k_norm, diff_lambda_q1, diff_lambda_k1, diff_lambda_q2, diff_lambda_k2, diff_subln, swa_w_qkv, swa_w_o, swa_q_norm, swa_k_norm, swa_sink, ffn_w_gate, ffn_w_up, ffn_w_down, moe_router, moe_router_bias, moe_w_gate, moe_w_up, moe_w_down)
```

```python
import functools
import math
from typing import NamedTuple

import jax
import jax.numpy as jnp
from jax import lax
from jax.experimental import pallas as pl
from jax.experimental.pallas import tpu as pltpu

F32 = jnp.float32
BF16 = jnp.bfloat16

LANE = 128
SUBLANE_BF16 = 16
MOD_ROWS = 8
VMEM_LIMIT = 58 * 1024 * 1024
MAX_FULL_K = 4096
K_TILE = 4096
RES_ROW_PARTS = 2
RES_BLOCK_N = 512
RES_BLOCK_M = 1152
RES_ROW_SPLIT = 2
QKV_BLOCK_N = 1024
QKV_ROW_SPLIT = 4
SWIGLU_BLOCK_N = 512
SWIGLU_ROW_SPLIT = 2

NA_GROUP_UNROLL = 8
MODULATE_ROWS = 512
GQA_QUERY_ROWS = 512
DIFF_QUERY_ROWS = 1024
DIFF_KEY_CHUNK = 1024
DIFF_ROW_SPLIT = 4

NORM_EPS = 1e-6
NEG_INF = -1e30
ROPE_BASE = 10000.0
LOG2E = 1.4426950408889634


class Cfg(NamedTuple):
    B: int
    S: int
    L: int
    D: int
    depth: int
    grid_w: int
    head_dim: int
    na_win_h: int
    na_win_w: int
    swa_window: int
    swa_block: int
    swa_group: int
    n_experts: int
    top_k: int

    @property
    def M(self):
        return self.B * (self.S + self.L)

    @property
    def MX(self):
        return self.B * self.S


def _cparams(n_axes):
    return pltpu.CompilerParams(dimension_semantics=("arbitrary",) * n_axes,
                                vmem_limit_bytes=VMEM_LIMIT)


def _pick_block(n, cap, mult):
    best = None
    for d in range(mult, min(n, cap) + 1, mult):
        if n % d == 0:
            best = d
    assert best is not None, (n, cap, mult)
    return best


def _pick_row_block(rows, cap, bk):
    best = None
    for d in range(SUBLANE_BF16, min(rows, cap) + 1, SUBLANE_BF16):
        if rows % d == 0 and bk % (rows // d) == 0 and (bk // (rows // d)) % SUBLANE_BF16 == 0:
            best = d
    assert best is not None, (rows, cap, bk)
    return best


def _dot(a, b):
    return jnp.dot(a, b, preferred_element_type=F32)


def _dot_t(a, b):
    return lax.dot_general(a, b, (((1,), (1,)), ((), ())), preferred_element_type=F32)


def _row_select(vecs_ref, row0, bm, cfg):
    rows = row0 + lax.broadcasted_iota(jnp.int32, (bm, 1), 0)
    g = vecs_ref[cfg.B:cfg.B + 1, :]
    for b in reversed(range(cfg.B)):
        g = jnp.where(rows < (b + 1) * cfg.S, vecs_ref[b:b + 1, :], g)
    return g


def _ada_kernel(c_ref, down_ref, up_ref, b_ref, o_ref, t_ref):
    @pl.when(pl.program_id(1) == 0)
    def _():
        c = c_ref[...]
        sc = (c * jax.nn.sigmoid(c)).astype(BF16)
        t_ref[...] = _dot(sc, down_ref[...].astype(BF16))

    o_ref[...] = _dot(t_ref[...].astype(BF16), up_ref[...].astype(BF16)) + b_ref[...]


def _ada_all(cond, ada_down, ada_up, ada_bias):
    depth, D, R = ada_down.shape
    N = ada_up.shape[2]
    bn = _pick_block(N, 3072, LANE)
    return pl.pallas_call(
        _ada_kernel,
        out_shape=jax.ShapeDtypeStruct((depth, MOD_ROWS, N), F32),
        grid=(depth, N // bn),
        in_specs=[
            pl.BlockSpec((MOD_ROWS, D), lambda l, j: (0, 0)),
            pl.BlockSpec((None, D, R), lambda l, j: (l, 0, 0)),
            pl.BlockSpec((None, R, bn), lambda l, j: (l, 0, j)),
            pl.BlockSpec((None, 1, bn), lambda l, j: (l, 0, j)),
        ],
        out_specs=pl.BlockSpec((None, MOD_ROWS, bn), lambda l, j: (l, 0, j)),
        scratch_shapes=[pltpu.VMEM((MOD_ROWS, R), F32)],
        compiler_params=_cparams(2),
        name="ada_modulation",
    )(cond, ada_down, ada_up, ada_bias.reshape(depth, 1, N))


def _modulate_kernel(x_ref, g_ref, sh_ref, sc_ref, o_ref, *, bm, cfg):
    row0 = pl.program_id(0) * bm
    t = jnp.where(row0 < cfg.MX, row0 // cfg.S, cfg.B)
    x = x_ref[...]
    ms = jnp.mean(x * x, axis=-1, keepdims=True)
    y = x * lax.rsqrt(ms + NORM_EPS) * g_ref[...]
    o_ref[...] = (y * (1 + sc_ref[pl.ds(t, 1), :]) + sh_ref[pl.ds(t, 1), :]).astype(BF16)


def _modulate(h, gain, mods, shift_chunk, scale_chunk, n_rows, cfg):
    D = cfg.D
    bm = _pick_block(math.gcd(cfg.S, cfg.B * cfg.L), MODULATE_ROWS, SUBLANE_BF16)
    return pl.pallas_call(
        functools.partial(_modulate_kernel, bm=bm, cfg=cfg),
        out_shape=jax.ShapeDtypeStruct((n_rows, D), BF16),
        grid=(n_rows // bm,),
        in_specs=[
            pl.BlockSpec((bm, D), lambda i: (i, 0)),
            pl.BlockSpec((1, D), lambda i: (0, 0)),
            pl.BlockSpec((MOD_ROWS, D), lambda i: (0, shift_chunk)),
            pl.BlockSpec((MOD_ROWS, D), lambda i: (0, scale_chunk)),
        ],
        out_specs=pl.BlockSpec((bm, D), lambda i: (i, 0)),
        compiler_params=_cparams(1),
        name="modulate",
    )(h, gain.reshape(1, D), mods, mods)


def _stage_weight(w_ref, wb_ref, t, m, n_blocks):
    ck = w_ref.shape[0]

    @pl.when(t < n_blocks)
    def _():
        wb_ref[t % 2, pl.ds(pl.multiple_of(m * ck, ck), ck), :] = w_ref[...].astype(BF16)


def _qkv_kernel(*refs, n_blocks, n_norm_blocks, rope, bn):
    if rope:
        x_ref, w_ref, g_ref, cos_ref, sa_ref, sb_ref, o_ref, wb_ref = refs
    else:
        x_ref, w_ref, g_ref, o_ref, wb_ref = refs
    t = pl.program_id(0)
    _stage_weight(w_ref, wb_ref, t, pl.program_id(1), n_blocks)

    bm = x_ref.shape[0]
    rs = bm // QKV_ROW_SPLIT

    @pl.when(jnp.logical_and(t > 0, t - 1 < n_norm_blocks))
    def _():
        w = wb_ref.at[(t - 1) % 2]
        for r in range(QKV_ROW_SPLIT):
            rows = slice(r * rs, (r + 1) * rs)
            acc = _dot(x_ref[rows, :], w[...])
            outs = []
            for hh in range(bn // LANE):
                sl = slice(hh * LANE, (hh + 1) * LANE)
                xh = acc[:, sl]
                ms = jnp.mean(xh * xh, axis=-1, keepdims=True)
                y = xh * lax.rsqrt(ms + NORM_EPS) * g_ref[:, sl]
                if rope:
                    y = (y * cos_ref[rows, :] + pltpu.roll(y, 3 * LANE // 4, 1) * sa_ref[rows, :]
                         + pltpu.roll(y, LANE // 4, 1) * sb_ref[rows, :])
                outs.append(y.astype(BF16))
            o_ref[rows, :] = jnp.concatenate(outs, axis=1)

    @pl.when(jnp.logical_and(t > 0, t - 1 >= n_norm_blocks))
    def _():
        o_ref[...] = _dot(x_ref[...], wb_ref[(t - 1) % 2]).astype(BF16)


def _qkv_proj(xn, w_all, layer, gains, n_norm_cols, rope_tabs, cfg):
    M, D = xn.shape
    N = w_all.shape[2]
    bn = math.gcd(math.gcd(N, n_norm_cols), QKV_BLOCK_N)
    bm = _pick_block(M, 1152, SUBLANE_BF16)
    n_m, n_blocks = M // bm, N // bn
    ck = D // n_m
    assert bn % LANE == 0 and D % n_m == 0 and ck % SUBLANE_BF16 == 0
    rope = rope_tabs is not None
    row = lambda t, m: jnp.where(t == 0, 0, m)
    col = lambda t: jnp.maximum(t - 1, 0)
    in_specs = [
        pl.BlockSpec((bm, D), lambda t, m: (row(t, m), 0)),
        pl.BlockSpec((None, ck, bn), lambda t, m: (layer, m, jnp.minimum(t, n_blocks - 1))),
        pl.BlockSpec((1, bn), lambda t, m: (0, col(t))),
    ]
    args = [xn, w_all, gains]
    if rope:
        in_specs += [pl.BlockSpec((bm, LANE), lambda t, m: (row(t, m), 0))] * 3
        args += list(rope_tabs)
    return pl.pallas_call(
        functools.partial(_qkv_kernel, n_blocks=n_blocks, n_norm_blocks=n_norm_cols // bn, rope=rope, bn=bn),
        out_shape=jax.ShapeDtypeStruct((M, N), BF16),
        grid=(n_blocks + 1, n_m),
        in_specs=in_specs,
        out_specs=pl.BlockSpec((bm, bn), lambda t, m: (row(t, m), col(t))),
        scratch_shapes=[pltpu.VMEM((2, D, bn), BF16)],
        compiler_params=_cparams(2),
        name="qkv_proj",
    )(*args)


def _mm_res_kernel(a_ref, w_ref, res_ref, gate_ref, o_ref, wb_ref, *acc, n_blocks, nk, n_m, bm, cfg):
    part = pl.program_id(0)
    t = pl.program_id(1)
    m = pl.program_id(2)
    _stage_weight(w_ref, wb_ref, t, m, n_blocks)

    rs = bm // RES_ROW_SPLIT
    subs = [(slice(r * rs, (r + 1) * rs), r * rs) for r in range(RES_ROW_SPLIT)]

    def epilogue(sub, off, total):
        row0 = (part * n_m + m) * bm + off
        o_ref[sub, :] = res_ref[sub, :] + _row_select(gate_ref, row0, rs, cfg) * total

    @pl.when(t > 0)
    def _():
        w = wb_ref.at[(t - 1) % 2]
        if nk == 1:
            for sub, off in subs:
                epilogue(sub, off, _dot(a_ref[sub, :], w[...]))
        else:
            k = (t - 1) % nk
            acc_ref = acc[0]
            acc_rows = lambda off: pl.ds(pl.multiple_of(m * bm + off, 8), rs)

            @pl.when(k == 0)
            def _():
                for sub, off in subs:
                    acc_ref[acc_rows(off), :] = _dot(a_ref[sub, :], w[...])

            @pl.when(jnp.logical_and(k > 0, k < nk - 1))
            def _():
                for sub, off in subs:
                    acc_ref[acc_rows(off), :] += _dot(a_ref[sub, :], w[...])

            @pl.when(k == nk - 1)
            def _():
                for sub, off in subs:
                    epilogue(sub, off, acc_ref[acc_rows(off), :] + _dot(a_ref[sub, :], w[...]))


def _mm_res(a, w_all, layer, res, mods, gate_chunk, n_rows, cfg):
    _, K, N = w_all.shape
    bn = math.gcd(N, RES_BLOCK_N)
    bk = K if K <= MAX_FULL_K else _pick_block(K, K_TILE, LANE)
    nk = K // bk
    n_parts = RES_ROW_PARTS if nk > 1 else 1
    bm = _pick_row_block(n_rows // n_parts, RES_BLOCK_M, bk)
    n_m = n_rows // n_parts // bm
    n_blocks = (N // bn) * nk
    ck = bk // n_m
    assert n_rows % n_parts == 0 and bk % n_m == 0 and ck % SUBLANE_BF16 == 0
    wblk = lambda t: jnp.minimum(t, n_blocks - 1)
    cblk = lambda t: jnp.maximum(t - 1, 0)
    orow = lambda p, t, m: p * n_m + jnp.where(jnp.logical_and(t > 0, cblk(t) % nk == nk - 1), m, 0)
    scratch = [pltpu.VMEM((2, bk, bn), BF16)]
    if nk > 1:
        scratch.append(pltpu.VMEM((n_rows // n_parts, bn), F32))
    return pl.pallas_call(
        functools.partial(_mm_res_kernel, n_blocks=n_blocks, nk=nk, n_m=n_m, bm=bm, cfg=cfg),
        out_shape=jax.ShapeDtypeStruct((n_rows, N), F32),
        grid=(n_parts, n_blocks + 1, n_m),
        in_specs=[
            pl.BlockSpec((bm, bk), lambda p, t, m: (p * n_m + jnp.where(t == 0, 0, m), cblk(t) % nk)),
            pl.BlockSpec((None, ck, bn), lambda p, t, m: (layer, (wblk(t) % nk) * n_m + m, wblk(t) // nk)),
            pl.BlockSpec((bm, bn), lambda p, t, m: (orow(p, t, m), cblk(t) // nk)),
            pl.BlockSpec((MOD_ROWS, bn), lambda p, t, m: (0, gate_chunk * (N // bn) + cblk(t) // nk)),
        ],
        out_specs=pl.BlockSpec((bm, bn), lambda p, t, m: (orow(p, t, m), cblk(t) // nk)),
        scratch_shapes=scratch,
        compiler_params=_cparams(3),
        name="matmul_residual",
    )(a, w_all, res, mods)


def _swiglu_kernel(*refs, gated, n_blocks, nbe):
    if gated:
        x_ref, wg_ref, wu_ref, gates_ref, o_ref, wgb_ref, wub_ref = refs
    else:
        x_ref, wg_ref, wu_ref, o_ref, wgb_ref, wub_ref = refs
    t = pl.program_id(0)
    m = pl.program_id(1)
    _stage_weight(wg_ref, wgb_ref, t, m, n_blocks)
    _stage_weight(wu_ref, wub_ref, t, m, n_blocks)

    rs = x_ref.shape[0] // SWIGLU_ROW_SPLIT

    @pl.when(t > 0)
    def _():
        j = t - 1
        wg = wgb_ref.at[j % 2]
        wu = wub_ref.at[j % 2]
        for r in range(SWIGLU_ROW_SPLIT):
            rows = slice(r * rs, (r + 1) * rs)
            x = x_ref[rows, :]
            g = _dot(x, wg[...])
            u = _dot(x, wu[...])
            hid = g * jax.nn.sigmoid(g) * u
            if gated:
                gates = gates_ref[rows, :]
                lane = lax.broadcasted_iota(jnp.int32, gates.shape, 1)
                hid = hid * jnp.sum(jnp.where(lane == j // nbe, gates, 0.0), axis=-1, keepdims=True)
            o_ref[rows, :] = hid.astype(BF16)


def _swiglu_up(t, w_gate, w_up, layer, gates):
    M, D = t.shape
    _, E, _, F = w_gate.shape
    bn = math.gcd(F, SWIGLU_BLOCK_N)
    nbe = F // bn
    bm = _pick_block(M, 1152, SUBLANE_BF16)
    n_m, n_blocks = M // bm, E * nbe
    ck = D // n_m
    assert D % n_m == 0 and ck % SUBLANE_BF16 == 0
    gated = gates is not None
    row = lambda t, m: jnp.where(t == 0, 0, m)
    col = lambda t: jnp.maximum(t - 1, 0)
    wblk = lambda t: jnp.minimum(t, n_blocks - 1)
    w_spec = pl.BlockSpec((None, None, ck, bn), lambda t, m: (layer, wblk(t) // nbe, m, wblk(t) % nbe))
    in_specs = [pl.BlockSpec((bm, D), lambda t, m: (row(t, m), 0)), w_spec, w_spec]
    args = [t, w_gate, w_up]
    if gated:
        in_specs.append(pl.BlockSpec((bm, LANE), lambda t, m: (row(t, m), 0)))
        args.append(gates)
    return pl.pallas_call(
        functools.partial(_swiglu_kernel, gated=gated, n_blocks=n_blocks, nbe=nbe),
        out_shape=jax.ShapeDtypeStruct((M, E * F), BF16),
        grid=(n_blocks + 1, n_m),
        in_specs=in_specs,
        out_specs=pl.BlockSpec((bm, bn), lambda t, m: (row(t, m), col(t))),
        scratch_shapes=[pltpu.VMEM((2, D, bn), BF16), pltpu.VMEM((2, D, bn), BF16)],
        compiler_params=_cparams(2),
        name="swiglu_up",
    )(*args)


def _router_kernel(t_ref, w_ref, b_ref, o_ref):
    logits = _dot(t_ref[...], w_ref[...].astype(BF16)) + b_ref[...]
    lane = lax.broadcasted_iota(jnp.int32, logits.shape, 1).astype(F32)
    top1 = jnp.max(logits, axis=-1, keepdims=True)
    i1 = jnp.min(jnp.where(logits == top1, lane, float(LANE)), axis=-1, keepdims=True)
    rest = jnp.where(lane == i1, -jnp.inf, logits)
    top2 = jnp.max(rest, axis=-1, keepdims=True)
    i2 = jnp.min(jnp.where(rest == top2, lane, float(LANE)), axis=-1, keepdims=True)
    e2 = jnp.exp(top2 - top1)
    inv = 1.0 / (1.0 + e2)
    o_ref[...] = jnp.where(lane == i1, inv, 0.0) + jnp.where(lane == i2, e2 * inv, 0.0)


def _router(t, router_w, router_b, cfg):
    M, D = t.shape
    E = cfg.n_experts
    assert cfg.top_k == 2 and E <= LANE
    w = jnp.zeros((D, LANE), F32).at[:, :E].set(router_w)
    b = jnp.full((1, LANE), NEG_INF, F32).at[0, :E].set(router_b)
    bm = _pick_block(M, 1152, SUBLANE_BF16)
    return pl.pallas_call(
        _router_kernel,
        out_shape=jax.ShapeDtypeStruct((M, LANE), F32),
        grid=(M // bm,),
        in_specs=[
            pl.BlockSpec((bm, D), lambda m: (m, 0)),
            pl.BlockSpec((D, LANE), lambda m: (0, 0)),
            pl.BlockSpec((1, LANE), lambda m: (0, 0)),
        ],
        out_specs=pl.BlockSpec((bm, LANE), lambda m: (m, 0)),
        compiler_params=_cparams(1),
        name="moe_router",
    )(t, w, b)


def _na_variant_groups(n_groups):
    return (0, 1 if n_groups > 2 else 0, n_groups - 1)


def _na_build_bias(tp_ref, bias_ref, *, rows, W, kh):
    n_groups = rows // kh
    lane = lax.broadcasted_iota(jnp.int32, (W, 2 * W), 1)
    neg = jnp.full((W, 2 * W), NEG_INF, F32)
    for variant, g in enumerate(_na_variant_groups(n_groups)):
        key_row = min(max(g * kh - kh // 2, 0), rows - 2 * kh)
        for ql in range(kh):
            qi = g * kh + ql
            r0 = min(max(qi - kh // 2, 0), rows - kh)
            for j in range(kh):
                ki = key_row + 2 * j
                first_ok = r0 <= ki < r0 + kh
                second_ok = r0 <= ki + 1 < r0 + kh
                if not (first_ok or second_ok):
                    blk = neg
                else:
                    blk = tp_ref[ki - qi + kh]
                    if not second_ok:
                        blk = jnp.where(lane < W, blk, NEG_INF)
                    elif not first_ok:
                        blk = jnp.where(lane >= W, blk, NEG_INF)
                bias_ref[variant, ql * W:(ql + 1) * W, j * 2 * W:(j + 1) * 2 * W] = blk


def _na_kernel(q_ref, k_ref, v_ref, ck_ref, cv_ref, tp_ref, o_ref, bias_ref, *, rows, W, kh, unroll):
    @pl.when(pl.program_id(1) == 0)
    def _():
        _na_build_bias(tp_ref, bias_ref, rows=rows, W=W, kh=kh)

    ck = ck_ref[...]
    cv = cv_ref[...]
    n_groups = rows // kh
    gq, gk = kh * W, 2 * kh * W

    def one_group(g):
        variant = jnp.where(g == 0, 0, jnp.where(g == n_groups - 1, 2, 1))
        key_row = jnp.clip(g * kh - kh // 2, 0, rows - 2 * kh)
        q = q_ref[pl.ds(pl.multiple_of(g * gq, gq), gq), :]
        kstart = pl.multiple_of(key_row * W, W)
        s = _dot_t(q, k_ref[pl.ds(kstart, gk), :]) + bias_ref[variant]
        sc = _dot_t(q, ck)
        m = jnp.maximum(jnp.max(s, axis=-1, keepdims=True), jnp.max(sc, axis=-1, keepdims=True))
        e = jnp.exp2(s - m)
        ec = jnp.exp2(sc - m)
        inv = 1.0 / (jnp.sum(e, axis=-1, keepdims=True) + jnp.sum(ec, axis=-1, keepdims=True))
        o = (_dot(e.astype(BF16), v_ref[pl.ds(kstart, gk), :]) + _dot(ec.astype(BF16), cv)) * inv
        return o.astype(BF16)

    def body(it, carry):
        outs = [one_group(it * unroll + u) for u in range(unroll)]
        start = pl.multiple_of(it * (unroll * gq), unroll * gq)
        o_ref[pl.ds(start, unroll * gq), :] = jnp.concatenate(outs, axis=0)
        return carry

    lax.fori_loop(0, n_groups // unroll, body, 0)


def _na_bias_blocks(rpb, cfg):
    W, kw, wh = cfg.grid_w, cfg.na_win_w, cfg.na_win_h
    H = rpb.shape[0]
    wq = jnp.arange(W)[:, None]
    kc = jnp.arange(W)[None, :]
    c0 = jnp.clip(wq - kw // 2, 0, W - kw)
    col_valid = (kc >= c0) & (kc < c0 + kw)
    cidx = jnp.clip(kc - wq + (kw - 1), 0, 2 * kw - 2)
    toe = jnp.where(col_valid[None, None], rpb[:, :, cidx], NEG_INF).astype(F32)
    pad = jnp.full((H, 1, W, W), NEG_INF, F32)
    toe = jnp.concatenate([pad, toe, pad], axis=1)
    return jnp.concatenate([toe[:, :-1], toe[:, 1:]], axis=-1) * LOG2E


def _na_attention(qkv, rpb, n_out_rows, cfg):
    B, S, L, D, W, Dh = cfg.B, cfg.S, cfg.L, cfg.D, cfg.grid_w, cfg.head_dim
    H = D // Dh
    rows = S // W
    kh = min(cfg.na_win_h, rows)
    n_groups = rows // kh
    assert rows % kh == 0 and n_groups >= 2 and kh == cfg.na_win_h and 2 * W == LANE
    blocks = _na_bias_blocks(rpb, cfg)
    cblk = cfg.MX // L
    return pl.pallas_call(
        functools.partial(_na_kernel, rows=rows, W=W, kh=kh, unroll=math.gcd(rows // kh, NA_GROUP_UNROLL)),
        out_shape=jax.ShapeDtypeStruct((n_out_rows, D), BF16),
        grid=(H, B),
        in_specs=[
            pl.BlockSpec((S, Dh), lambda h, b: (b, h)),
            pl.BlockSpec((S, Dh), lambda h, b: (b, H + h)),
            pl.BlockSpec((S, Dh), lambda h, b: (b, 2 * H + h)),
            pl.BlockSpec((L, Dh), lambda h, b: (cblk + b, H + h)),
            pl.BlockSpec((L, Dh), lambda h, b: (cblk + b, 2 * H + h)),
            pl.BlockSpec((None, 2 * kh, W, 2 * W), lambda h, b: (h, 0, 0, 0)),
        ],
        out_specs=pl.BlockSpec((S, Dh), lambda h, b: (b, h)),
        scratch_shapes=[pltpu.VMEM((3, kh * W, 2 * kh * W), F32)],
        compiler_params=_cparams(2),
        name="neighborhood_attention",
    )(qkv, qkv, qkv, qkv, qkv, blocks)


def _gqa_kernel(*refs, G, has_local, has_sink, has_alias, win, S, Dh):
    refs = list(refs)
    sink_ref = refs.pop(0) if has_sink else None
    q_ref = refs.pop(0)
    k_ref, v_ref, mask_ref = (refs.pop(0), refs.pop(0), refs.pop(0)) if has_local else (None, None, None)
    ck_ref, cv_ref = refs.pop(0), refs.pop(0)
    if has_alias:
        refs.pop(0)
    o_ref = refs.pop(0)
    hk = pl.program_id(1)
    rows = q_ref.shape[0]
    ck = ck_ref[...]
    cv = cv_ref[...]
    if has_local:
        nkw = rows + 2 * win
        start = pl.multiple_of(jnp.clip(pl.program_id(2) * rows - win, 0, S - nkw), win)
        kw = k_ref[pl.ds(start, nkw), :]
        vw = v_ref[pl.ds(start, nkw), :]
    outs = []
    for g in range(G):
        q = q_ref[:, g * Dh:(g + 1) * Dh]
        sc = _dot_t(q, ck)
        m = jnp.max(sc, axis=-1, keepdims=True)
        if has_local:
            s = _dot_t(q, kw) + mask_ref[...]
            m = jnp.maximum(m, jnp.max(s, axis=-1, keepdims=True))
        if has_sink:
            sk = sink_ref[hk, g] * LOG2E
            m = jnp.maximum(m, sk)
        ec = jnp.exp2(sc - m)
        l = jnp.sum(ec, axis=-1, keepdims=True)
        o = _dot(ec.astype(BF16), cv)
        if has_local:
            e = jnp.exp2(s - m)
            l = l + jnp.sum(e, axis=-1, keepdims=True)
            o = o + _dot(e.astype(BF16), vw)
        if has_sink:
            l = l + jnp.exp2(sk - m)
        outs.append((o * (1.0 / l)).astype(BF16))
    o_ref[...] = jnp.concatenate(outs, axis=1)


def _band_mask(rows, win, S):
    nkw = rows + 2 * win
    r = jnp.arange(rows)[:, None]
    c = jnp.arange(nkw)[None, :]
    offsets = (0, -win, -2 * win)
    return jnp.stack([jnp.where(jnp.abs(off + c - r) <= win, 0.0, NEG_INF) for off in offsets]).astype(F32)


def _gqa_attention(qkv, sink, q_heads, kv_heads, local, o_prev, n_out_rows, cfg):
    B, S, L, D, Dh = cfg.B, cfg.S, cfg.L, cfg.D, cfg.head_dim
    G = q_heads // kv_heads
    kcol = q_heads
    vcol = q_heads + kv_heads
    cblk = cfg.MX // L
    has_sink = sink is not None
    win = cfg.swa_window
    rows_step = GQA_QUERY_ROWS if local else L
    nq = S // rows_step if local else 1
    qrow = (lambda b, i: b * nq + i) if local else (lambda b, i: cblk + b)
    in_specs, args = [], []
    if has_sink:
        in_specs.append(pl.BlockSpec(memory_space=pltpu.SMEM))
        args.append(sink.reshape(kv_heads, G).astype(F32))
    in_specs.append(pl.BlockSpec((rows_step, G * Dh), lambda b, h, i: (qrow(b, i), h)))
    args.append(qkv)
    if local:
        assert cfg.swa_block == win and S % rows_step == 0 and nq >= 2 and win % SUBLANE_BF16 == 0
        nkw = rows_step + 2 * win
        variant = lambda i: jnp.where(i == 0, 0, jnp.where(i == nq - 1, 2, 1))
        in_specs += [pl.BlockSpec((S, Dh), lambda b, h, i: (b, kcol + h)),
                     pl.BlockSpec((S, Dh), lambda b, h, i: (b, vcol + h)),
                     pl.BlockSpec((None, rows_step, nkw), lambda b, h, i: (variant(i), 0, 0))]
        args += [qkv, qkv, _band_mask(rows_step, win, S)]
    in_specs += [pl.BlockSpec((L, Dh), lambda b, h, i: (cblk + b, kcol + h)),
                 pl.BlockSpec((L, Dh), lambda b, h, i: (cblk + b, vcol + h))]
    args += [qkv, qkv]
    aliases = {}
    if o_prev is not None:
        in_specs.append(pl.BlockSpec(memory_space=pl.ANY))
        aliases = {len(args): 0}
        args.append(o_prev)
    return pl.pallas_call(
        functools.partial(_gqa_kernel, G=G, has_local=local, has_sink=has_sink,
                          has_alias=o_prev is not None, win=win, S=S, Dh=Dh),
        out_shape=jax.ShapeDtypeStruct((n_out_rows, D), BF16),
        grid=(B, kv_heads, nq),
        in_specs=in_specs,
        out_specs=pl.BlockSpec((rows_step, G * Dh), lambda b, h, i: (qrow(b, i), h)),
        input_output_aliases=aliases,
        compiler_params=_cparams(3),
        name="gqa_window_attention" if local else "gqa_context_attention",
    )(*args)


def _diff_kernel(*refs, has_x, has_alias, chunk, row_split, S, L, lambda_init, Dh):
    refs = list(refs)
    lq1, lk1, lq2, lk2, sub_ref, q_ref = refs[:6]
    refs = refs[6:]
    kx_ref, vx_ref = (refs.pop(0), refs.pop(0)) if has_x else (None, None)
    kc_ref, vc_ref = refs.pop(0), refs.pop(0)
    if has_alias:
        refs.pop(0)
    o_ref, e_ref = refs
    lam = (jnp.exp(jnp.sum(lq1[...] * lk1[...], axis=-1, keepdims=True))
           - jnp.exp(jnp.sum(lq2[...] * lk2[...], axis=-1, keepdims=True)) + lambda_init)
    chunks = []
    if has_x:
        chunks += [(kx_ref, vx_ref, c * chunk, chunk, c * chunk) for c in range(S // chunk)]
    off = S if has_x else 0
    chunks.append((kc_ref, vc_ref, 0, L, off))
    rs = q_ref.shape[0] // row_split
    for r in range(row_split):
        rows = slice(r * rs, (r + 1) * rs)
        coef = []
        for mi in range(2):
            sl = slice(mi * Dh, (mi + 1) * Dh)
            q = q_ref[rows, sl]
            mx = jnp.full((rs, 1), -jnp.inf, F32)
            l = jnp.zeros((rs, 1), F32)
            maxes = []
            for k_ref, _, r0, n, e0 in chunks:
                s = _dot_t(q, k_ref[r0:r0 + n, sl])
                mn = jnp.maximum(mx, jnp.max(s, axis=-1, keepdims=True))
                e = jnp.exp2(s - mn)
                l = l * jnp.exp2(mx - mn) + jnp.sum(e, axis=-1, keepdims=True)
                e_ref[mi, rows, e0:e0 + n] = e
                maxes.append(mn)
                mx = mn
            inv = 1.0 / l
            coef.append([jnp.exp2(mc - mx) * inv for mc in maxes])
        acc = jnp.zeros((rs, 2 * Dh), F32)
        for ci, (_, v_ref, r0, n, e0) in enumerate(chunks):
            p = e_ref[0, rows, e0:e0 + n] * coef[0][ci] - e_ref[1, rows, e0:e0 + n] * (lam * coef[1][ci])
            acc = acc + _dot(p.astype(BF16), v_ref[r0:r0 + n, :])
        ms = jnp.mean(acc * acc, axis=-1, keepdims=True)
        y = acc * lax.rsqrt(ms + NORM_EPS) * sub_ref[...] * (1.0 - lambda_init)
        o_ref[rows, :] = y.astype(BF16)


def _diff_attention(qkv, lams, subln, lambda_init, main, o_prev, n_out_rows, cfg):
    B, S, L, D, Dh = cfg.B, cfg.S, cfg.L, cfg.D, cfg.head_dim
    H = D // (2 * Dh)
    HW = 2 * Dh
    cblk = cfg.MX // L
    bq = DIFF_QUERY_ROWS if main else L
    nq = S // bq if main else 1
    nkeys = S + L if main else L
    qrow = (lambda b, i: b * nq + i) if main else (lambda b, i: cblk + b)
    vec = pl.BlockSpec((1, Dh), lambda b, h, i: (0, 0))
    in_specs = [vec, vec, vec, vec,
                pl.BlockSpec((1, HW), lambda b, h, i: (0, 0)),
                pl.BlockSpec((bq, HW), lambda b, h, i: (qrow(b, i), h))]
    args = [v.reshape(1, Dh).astype(F32) for v in lams] + [subln.reshape(1, HW).astype(F32), qkv]
    if main:
        in_specs += [pl.BlockSpec((S, HW), lambda b, h, i: (b, H + h)),
                     pl.BlockSpec((S, HW), lambda b, h, i: (b, 2 * H + h))]
        args += [qkv, qkv]
    in_specs += [pl.BlockSpec((L, HW), lambda b, h, i: (cblk + b, H + h)),
                 pl.BlockSpec((L, HW), lambda b, h, i: (cblk + b, 2 * H + h))]
    args += [qkv, qkv]
    aliases = {}
    if o_prev is not None:
        in_specs.append(pl.BlockSpec(memory_space=pl.ANY))
        aliases = {len(args): 0}
        args.append(o_prev)
    return pl.pallas_call(
        functools.partial(_diff_kernel, has_x=main, has_alias=o_prev is not None,
                          chunk=math.gcd(S, DIFF_KEY_CHUNK), row_split=min(DIFF_ROW_SPLIT, max(bq // LANE, 1)),
                          S=S, L=L,
                          lambda_init=lambda_init, Dh=Dh),
        out_shape=jax.ShapeDtypeStruct((n_out_rows, D), BF16),
        grid=(B, H, nq),
        in_specs=in_specs,
        out_specs=pl.BlockSpec((bq, HW), lambda b, h, i: (qrow(b, i), h)),
        scratch_shapes=[pltpu.VMEM((2, bq, nkeys), F32)],
        input_output_aliases=aliases,
        compiler_params=_cparams(3),
        name="diff_attention" if main else "diff_context_attention",
    )(*args)


def _rope_tables(cfg):
    Dh = cfg.head_dim
    t = jnp.arange(cfg.S, dtype=jnp.int32)
    row = (t // cfg.grid_w).astype(F32)
    col = (t % cfg.grid_w).astype(F32)
    axis_dim = Dh // 2
    inv_freq = ROPE_BASE ** (-jnp.arange(0, axis_dim, 2, dtype=F32) / axis_dim)
    ang_r = row[:, None] * inv_freq[None, :]
    ang_c = col[:, None] * inv_freq[None, :]
    ang = jnp.concatenate([ang_r, ang_r, ang_c, ang_c], axis=-1)
    cos, sin = jnp.cos(ang), jnp.sin(ang)
    quarter = jnp.arange(Dh) // (Dh // 4)
    sin_a = jnp.where(quarter % 2 == 0, -sin, 0.0)
    sin_b = jnp.where(quarter % 2 == 1, sin, 0.0)
    n_ctx = cfg.B * cfg.L

    def full(tab, fill):
        return jnp.concatenate([jnp.tile(tab, (cfg.B, 1)), jnp.full((n_ctx, Dh), fill, F32)], axis=0)

    return full(cos, 1.0), full(sin_a, 0.0), full(sin_b, 0.0)


def _head_gains(q_gain, k_gain, n_q, n_k, n_v_cols, head_dim):
    q_scale = head_dim ** -0.5 * LOG2E
    return jnp.concatenate([jnp.tile(q_gain.astype(F32) * q_scale, n_q), jnp.tile(k_gain.astype(F32), n_k),
                            jnp.ones((n_v_cols,), F32)]).reshape(1, -1)


def _forward(cfg, x, c, ctx, c_ctx, ada_down, ada_up, ada_bias, norm_mix, norm_ffn,
             na_w_qkv, na_w_o, na_q_norm, na_k_norm, na_rpb,
             diff_w_qkv, diff_w_o, diff_q_norm, diff_k_norm,
             diff_lambda_q1, diff_lambda_k1, diff_lambda_q2, diff_lambda_k2, diff_subln,
             swa_w_qkv, swa_w_o, swa_q_norm, swa_k_norm, swa_sink,
             ffn_w_gate, ffn_w_up, ffn_w_down,
             moe_router, moe_router_bias, moe_w_gate, moe_w_up, moe_w_down):
    B, S, L, D, Dh = cfg.B, cfg.S, cfg.L, cfg.D, cfg.head_dim
    M, MX = cfg.M, cfg.MX
    heads = D // Dh
    h = jnp.concatenate([x.reshape(MX, D), ctx.reshape(B * L, D)], axis=0).astype(F32)
    cond = jnp.zeros((MOD_ROWS, D), F32).at[:B].set(c).at[B].set(c_ctx)
    mods_all = _ada_all(cond, ada_down, ada_up, ada_bias)
    rope_tabs = _rope_tables(cfg)

    for i in range(cfg.depth):
        ctx_out = i < cfg.depth - 1
        n_rows = M if ctx_out else MX
        mods = mods_all[i]
        xn = _modulate(h, norm_mix[i], mods, 0, 1, M, cfg)
        kind, j = i % 3, i // 3
        if kind == 0:
            gains = _head_gains(na_q_norm[j], na_k_norm[j], heads, heads, D, Dh)
            qkv = _qkv_proj(xn, na_w_qkv, j, gains, 2 * D, None, cfg)
            o = _na_attention(qkv, na_rpb[j], n_rows, cfg)
            if ctx_out:
                o = _gqa_attention(qkv, None, heads, heads, False, o, n_rows, cfg)
            w_o = na_w_o
        elif kind == 1:
            gains = _head_gains(diff_q_norm[j], diff_k_norm[j], heads, heads, D, Dh)
            qkv = _qkv_proj(xn, diff_w_qkv, j, gains, 2 * D, rope_tabs, cfg)
            lams = (diff_lambda_q1[j], diff_lambda_k1[j], diff_lambda_q2[j], diff_lambda_k2[j])
            lambda_init = 0.8 - 0.6 * math.exp(-0.3 * i)
            o = _diff_attention(qkv, lams, diff_subln[j], lambda_init, True, None, n_rows, cfg)
            if ctx_out:
                o = _diff_attention(qkv, lams, diff_subln[j], lambda_init, False, o, n_rows, cfg)
            w_o = diff_w_o
        else:
            kvh = heads // cfg.swa_group
            gains = _head_gains(swa_q_norm[j], swa_k_norm[j], heads, kvh, kvh * Dh, Dh)
            qkv = _qkv_proj(xn, swa_w_qkv, j, gains, (heads + kvh) * Dh, rope_tabs, cfg)
            o = _gqa_attention(qkv, swa_sink[j], heads, kvh, True, None, n_rows, cfg)
            if ctx_out:
                o = _gqa_attention(qkv, swa_sink[j], heads, kvh, False, o, n_rows, cfg)
            w_o = swa_w_o
        h = _mm_res(o, w_o, j, h, mods, 2, n_rows, cfg)
        t = _modulate(h, norm_ffn[i], mods, 3, 4, n_rows, cfg)
        fj = i // 2
        if i % 2 == 0:
            n_ffn, _, d_ff = ffn_w_gate.shape
            hid = _swiglu_up(t, ffn_w_gate.reshape(n_ffn, 1, D, d_ff), ffn_w_up.reshape(n_ffn, 1, D, d_ff), fj, None)
            w_down = ffn_w_down
        else:
            gates = _router(t, moe_router[fj], moe_router_bias[fj], cfg)
            hid = _swiglu_up(t, moe_w_gate, moe_w_up, fj, gates)
            w_down = moe_w_down.reshape(moe_w_down.shape[0], -1, D)
        h = _mm_res(hid, w_down, fj, h, mods, 5, n_rows, cfg)
    return h.reshape(B, S, D)


_CFG = Cfg(B=2, S=4096, L=256, D=4096, depth=4, grid_w=64, head_dim=128, na_win_h=8, na_win_w=16,
           swa_window=128, swa_block=128, swa_group=4, n_experts=8, top_k=2)


def kernel(x, c, ctx, c_ctx, ada_down, ada_up, ada_bias, norm_mix, norm_ffn, na_w_qkv, na_w_o, na_q_norm, na_k_norm, na_rpb, diff_w_qkv, diff_w_o, diff_q_norm, diff_k_norm, diff_lambda_q1, diff_lambda_k1, diff_lambda_q2, diff_lambda_k2, diff_subln, swa_w_qkv, swa_w_o, swa_q_norm, swa_k_norm, swa_sink, ffn_w_gate, ffn_w_up, ffn_w_down, moe_router, moe_router_bias, moe_w_gate, moe_w_up, moe_w_down):
    return _forward(_CFG, x, c, ctx, c_ctx, ada_down, ada_up, ada_bias, norm_mix, norm_ffn, na_w_qkv, na_w_o, na_q_norm, na_k_norm, na_rpb, diff_w_qkv, diff_w_o, diff_q_norm, diff_k_norm, diff_lambda_q1, diff_lambda_k1, diff_lambda_q2, diff_lambda_k2, diff_subln, swa_w_qkv, swa_w_o, swa_q_norm, swa_k_norm, swa_sink, ffn_w_gate, ffn_w_up, ffn_w_down, moe_router, moe_router_bias, moe_w_gate, moe_w_up, moe_w_down)
```

```python
import functools
import math
from typing import NamedTuple

import jax
import jax.numpy as jnp
from jax import lax
from jax.experimental import pallas as pl
from jax.experimental.pallas import tpu as pltpu

F32 = jnp.float32
BF16 = jnp.bfloat16

LANE = 128
SUBLANE_BF16 = 16
MOD_ROWS = 8
VMEM_LIMIT = 58 * 1024 * 1024
MAX_FULL_K = 4096
K_TILE = 4096
RES_ROW_PARTS = 2
RES_BLOCK_N = 512
RES_BLOCK_M = 1152
RES_ROW_SPLIT = 2
QKV_BLOCK_N = 1024
QKV_ROW_SPLIT = 4
SWIGLU_BLOCK_N = 512
SWIGLU_ROW_SPLIT = 2

NA_GROUP_UNROLL = 8
MODULATE_ROWS = 512
GQA_QUERY_ROWS = 512
DIFF_QUERY_ROWS = 1024
DIFF_KEY_CHUNK = 1024
DIFF_ROW_SPLIT = 4

NORM_EPS = 1e-6
NEG_INF = -1e30
ROPE_BASE = 10000.0
LOG2E = 1.4426950408889634


class Cfg(NamedTuple):
    B: int
    S: int
    L: int
    D: int
    depth: int
    grid_w: int
    head_dim: int
    na_win_h: int
    na_win_w: int
    swa_window: int
    swa_block: int
    swa_group: int
    n_experts: int
    top_k: int

    @property
    def M(self):
        return self.B * (self.S + self.L)

    @property
    def MX(self):
        return self.B * self.S


def _cparams(n_axes):
    return pltpu.CompilerParams(dimension_semantics=("arbitrary",) * n_axes,
                                vmem_limit_bytes=VMEM_LIMIT)


def _pick_block(n, cap, mult):
    best = None
    for d in range(mult, min(n, cap) + 1, mult):
        if n % d == 0:
            best = d
    assert best is not None, (n, cap, mult)
    return best


def _pick_row_block(rows, cap, bk):
    best = None
    for d in range(SUBLANE_BF16, min(rows, cap) + 1, SUBLANE_BF16):
        if rows % d == 0 and bk % (rows // d) == 0 and (bk // (rows // d)) % SUBLANE_BF16 == 0:
            best = d
    assert best is not None, (rows, cap, bk)
    return best


def _dot(a, b):
    return jnp.dot(a, b, preferred_element_type=F32)


def _dot_t(a, b):
    return lax.dot_general(a, b, (((1,), (1,)), ((), ())), preferred_element_type=F32)


def _row_select(vecs_ref, row0, bm, cfg):
    rows = row0 + lax.broadcasted_iota(jnp.int32, (bm, 1), 0)
    g = vecs_ref[cfg.B:cfg.B + 1, :]
    for b in reversed(range(cfg.B)):
        g = jnp.where(rows < (b + 1) * cfg.S, vecs_ref[b:b + 1, :], g)
    return g


def _ada_kernel(c_ref, down_ref, up_ref, b_ref, o_ref, t_ref):
    @pl.when(pl.program_id(1) == 0)
    def _():
        c = c_ref[...]
        sc = (c * jax.nn.sigmoid(c)).astype(BF16)
        t_ref[...] = _dot(sc, down_ref[...].astype(BF16))

    o_ref[...] = _dot(t_ref[...].astype(BF16), up_ref[...].astype(BF16)) + b_ref[...]


def _ada_all(cond, ada_down, ada_up, ada_bias):
    depth, D, R = ada_down.shape
    N = ada_up.shape[2]
    bn = _pick_block(N, 3072, LANE)
    return pl.pallas_call(
        _ada_kernel,
        out_shape=jax.ShapeDtypeStruct((depth, MOD_ROWS, N), F32),
        grid=(depth, N // bn),
        in_specs=[
            pl.BlockSpec((MOD_ROWS, D), lambda l, j: (0, 0)),
            pl.BlockSpec((None, D, R), lambda l, j: (l, 0, 0)),
            pl.BlockSpec((None, R, bn), lambda l, j: (l, 0, j)),
            pl.BlockSpec((None, 1, bn), lambda l, j: (l, 0, j)),
        ],
        out_specs=pl.BlockSpec((None, MOD_ROWS, bn), lambda l, j: (l, 0, j)),
        scratch_shapes=[pltpu.VMEM((MOD_ROWS, R), F32)],
        compiler_params=_cparams(2),
        name="ada_modulation",
    )(cond, ada_down, ada_up, ada_bias.reshape(depth, 1, N))


def _modulate_kernel(x_ref, g_ref, sh_ref, sc_ref, o_ref, *, bm, cfg):
    row0 = pl.program_id(0) * bm
    t = jnp.where(row0 < cfg.MX, row0 // cfg.S, cfg.B)
    x = x_ref[...]
    ms = jnp.mean(x * x, axis=-1, keepdims=True)
    y = x * lax.rsqrt(ms + NORM_EPS) * g_ref[...]
    o_ref[...] = (y * (1 + sc_ref[pl.ds(t, 1), :]) + sh_ref[pl.ds(t, 1), :]).astype(BF16)


def _modulate(h, gain, mods, shift_chunk, scale_chunk, n_rows, cfg):
    D = cfg.D
    bm = _pick_block(math.gcd(cfg.S, cfg.B * cfg.L), MODULATE_ROWS, SUBLANE_BF16)
    return pl.pallas_call(
        functools.partial(_modulate_kernel, bm=bm, cfg=cfg),
        out_shape=jax.ShapeDtypeStruct((n_rows, D), BF16),
        grid=(n_rows // bm,),
        in_specs=[
            pl.BlockSpec((bm, D), lambda i: (i, 0)),
            pl.BlockSpec((1, D), lambda i: (0, 0)),
            pl.BlockSpec((MOD_ROWS, D), lambda i: (0, shift_chunk)),
            pl.BlockSpec((MOD_ROWS, D), lambda i: (0, scale_chunk)),
        ],
        out_specs=pl.BlockSpec((bm, D), lambda i: (i, 0)),
        compiler_params=_cparams(1),
        name="modulate",
    )(h, gain.reshape(1, D), mods, mods)


def _stage_weight(w_ref, wb_ref, t, m, n_blocks):
    ck = w_ref.shape[0]

    @pl.when(t < n_blocks)
    def _():
        wb_ref[t % 2, pl.ds(pl.multiple_of(m * ck, ck), ck), :] = w_ref[...].astype(BF16)


def _qkv_kernel(*refs, n_blocks, n_norm_blocks, rope, bn):
    if rope:
        x_ref, w_ref, g_ref, cos_ref, sa_ref, sb_ref, o_ref, wb_ref = refs
    else:
        x_ref, w_ref, g_ref, o_ref, wb_ref = refs
    t = pl.program_id(0)
    _stage_weight(w_ref, wb_ref, t, pl.program_id(1), n_blocks)

    bm = x_ref.shape[0]
    rs = bm // QKV_ROW_SPLIT

    @pl.when(jnp.logical_and(t > 0, t - 1 < n_norm_blocks))
    def _():
        w = wb_ref.at[(t - 1) % 2]
        for r in range(QKV_ROW_SPLIT):
            rows = slice(r * rs, (r + 1) * rs)
            acc = _dot(x_ref[rows, :], w[...])
            outs = []
            for hh in range(bn // LANE):
                sl = slice(hh * LANE, (hh + 1) * LANE)
                xh = acc[:, sl]
                ms = jnp.mean(xh * xh, axis=-1, keepdims=True)
                y = xh * lax.rsqrt(ms + NORM_EPS) * g_ref[:, sl]
                if rope:
                    y = (y * cos_ref[rows, :] + pltpu.roll(y, 3 * LANE // 4, 1) * sa_ref[rows, :]
                         + pltpu.roll(y, LANE // 4, 1) * sb_ref[rows, :])
                outs.append(y.astype(BF16))
            o_ref[rows, :] = jnp.concatenate(outs, axis=1)

    @pl.when(jnp.logical_and(t > 0, t - 1 >= n_norm_blocks))
    def _():
        o_ref[...] = _dot(x_ref[...], wb_ref[(t - 1) % 2]).astype(BF16)


def _qkv_proj(xn, w_all, layer, gains, n_norm_cols, rope_tabs, cfg):
    M, D = xn.shape
    N = w_all.shape[2]
    bn = math.gcd(math.gcd(N, n_norm_cols), QKV_BLOCK_N)
    bm = _pick_block(M, 1152, SUBLANE_BF16)
    n_m, n_blocks = M // bm, N // bn
    ck = D // n_m
    assert bn % LANE == 0 and D % n_m == 0 and ck % SUBLANE_BF16 == 0
    rope = rope_tabs is not None
    row = lambda t, m: jnp.where(t == 0, 0, m)
    col = lambda t: jnp.maximum(t - 1, 0)
    in_specs = [
        pl.BlockSpec((bm, D), lambda t, m: (row(t, m), 0)),
        pl.BlockSpec((None, ck, bn), lambda t, m: (layer, m, jnp.minimum(t, n_blocks - 1))),
        pl.BlockSpec((1, bn), lambda t, m: (0, col(t))),
    ]
    args = [xn, w_all, gains]
    if rope:
        in_specs += [pl.BlockSpec((bm, LANE), lambda t, m: (row(t, m), 0))] * 3
        args += list(rope_tabs)
    return pl.pallas_call(
        functools.partial(_qkv_kernel, n_blocks=n_blocks, n_norm_blocks=n_norm_cols // bn, rope=rope, bn=bn),
        out_shape=jax.ShapeDtypeStruct((M, N), BF16),
        grid=(n_blocks + 1, n_m),
        in_specs=in_specs,
        out_specs=pl.BlockSpec((bm, bn), lambda t, m: (row(t, m), col(t))),
        scratch_shapes=[pltpu.VMEM((2, D, bn), BF16)],
        compiler_params=_cparams(2),
        name="qkv_proj",
    )(*args)


def _mm_res_kernel(a_ref, w_ref, res_ref, gate_ref, o_ref, wb_ref, *acc, n_blocks, nk, n_m, bm, cfg):
    part = pl.program_id(0)
    t = pl.program_id(1)
    m = pl.program_id(2)
    _stage_weight(w_ref, wb_ref, t, m, n_blocks)

    rs = bm // RES_ROW_SPLIT
    subs = [(slice(r * rs, (r + 1) * rs), r * rs) for r in range(RES_ROW_SPLIT)]

    def epilogue(sub, off, total):
        row0 = (part * n_m + m) * bm + off
        o_ref[sub, :] = res_ref[sub, :] + _row_select(gate_ref, row0, rs, cfg) * total

    @pl.when(t > 0)
    def _():
        w = wb_ref.at[(t - 1) % 2]
        if nk == 1:
            for sub, off in subs:
                epilogue(sub, off, _dot(a_ref[sub, :], w[...]))
        else:
            k = (t - 1) % nk
            acc_ref = acc[0]
            acc_rows = lambda off: pl.ds(pl.multiple_of(m * bm + off, 8), rs)

            @pl.when(k == 0)
            def _():
                for sub, off in subs:
                    acc_ref[acc_rows(off), :] = _dot(a_ref[sub, :], w[...])

            @pl.when(jnp.logical_and(k > 0, k < nk - 1))
            def _():
                for sub, off in subs:
                    acc_ref[acc_rows(off), :] += _dot(a_ref[sub, :], w[...])

            @pl.when(k == nk - 1)
            def _():
                for sub, off in subs:
                    epilogue(sub, off, acc_ref[acc_rows(off), :] + _dot(a_ref[sub, :], w[...]))


def _mm_res(a, w_all, layer, res, mods, gate_chunk, n_rows, cfg):
    _, K, N = w_all.shape
    bn = math.gcd(N, RES_BLOCK_N)
    bk = K if K <= MAX_FULL_K else _pick_block(K, K_TILE, LANE)
    nk = K // bk
    n_parts = RES_ROW_PARTS if nk > 1 else 1
    bm = _pick_row_block(n_rows // n_parts, RES_BLOCK_M, bk)
    n_m = n_rows // n_parts // bm
    n_blocks = (N // bn) * nk
    ck = bk // n_m
    assert n_rows % n_parts == 0 and bk % n_m == 0 and ck % SUBLANE_BF16 == 0
    wblk = lambda t: jnp.minimum(t, n_blocks - 1)
    cblk = lambda t: jnp.maximum(t - 1, 0)
    orow = lambda p, t, m: p * n_m + jnp.where(jnp.logical_and(t > 0, cblk(t) % nk == nk - 1), m, 0)
    scratch = [pltpu.VMEM((2, bk, bn), BF16)]
    if nk > 1:
        scratch.append(pltpu.VMEM((n_rows // n_parts, bn), F32))
    return pl.pallas_call(
        functools.partial(_mm_res_kernel, n_blocks=n_blocks, nk=nk, n_m=n_m, bm=bm, cfg=cfg),
        out_shape=jax.ShapeDtypeStruct((n_rows, N), F32),
        grid=(n_parts, n_blocks + 1, n_m),
        in_specs=[
            pl.BlockSpec((bm, bk), lambda p, t, m: (p * n_m + jnp.where(t == 0, 0, m), cblk(t) % nk)),
            pl.BlockSpec((None, ck, bn), lambda p, t, m: (layer, (wblk(t) % nk) * n_m + m, wblk(t) // nk)),
            pl.BlockSpec((bm, bn), lambda p, t, m: (orow(p, t, m), cblk(t) // nk)),
            pl.BlockSpec((MOD_ROWS, bn), lambda p, t, m: (0, gate_chunk * (N // bn) + cblk(t) // nk)),
        ],
        out_specs=pl.BlockSpec((bm, bn), lambda p, t, m: (orow(p, t, m), cblk(t) // nk)),
        scratch_shapes=scratch,
        compiler_params=_cparams(3),
        name="matmul_residual",
    )(a, w_all, res, mods)


def _swiglu_kernel(*refs, gated, n_blocks, nbe):
    if gated:
        x_ref, wg_ref, wu_ref, gates_ref, o_ref, wb_ref = refs
    else:
        x_ref, wg_ref, wu_ref, o_ref, wb_ref = refs
    t = pl.program_id(0)
    m = pl.program_id(1)
    ck, bn = wg_ref.shape

    @pl.when(t < n_blocks)
    def _():
        chunk = pl.ds(pl.multiple_of(m * ck, ck), ck)
        wb_ref[t % 2, chunk, :bn] = wg_ref[...].astype(BF16)
        wb_ref[t % 2, chunk, bn:] = wu_ref[...].astype(BF16)

    rs = x_ref.shape[0] // SWIGLU_ROW_SPLIT

    @pl.when(t > 0)
    def _():
        j = t - 1
        w = wb_ref.at[j % 2]
        for r in range(SWIGLU_ROW_SPLIT):
            rows = slice(r * rs, (r + 1) * rs)
            gu = _dot(x_ref[rows, :], w[...])
            g, u = gu[:, :bn], gu[:, bn:]
            hid = g * jax.nn.sigmoid(g) * u
            if gated:
                gates = gates_ref[rows, :]
                lane = lax.broadcasted_iota(jnp.int32, gates.shape, 1)
                hid = hid * jnp.sum(jnp.where(lane == j // nbe, gates, 0.0), axis=-1, keepdims=True)
            o_ref[rows, :] = hid.astype(BF16)


def _swiglu_up(t, w_gate, w_up, layer, gates):
    M, D = t.shape
    _, E, _, F = w_gate.shape
    bn = math.gcd(F, SWIGLU_BLOCK_N)
    nbe = F // bn
    bm = _pick_block(M, 1152, SUBLANE_BF16)
    n_m, n_blocks = M // bm, E * nbe
    ck = D // n_m
    assert D % n_m == 0 and ck % SUBLANE_BF16 == 0
    gated = gates is not None
    row = lambda t, m: jnp.where(t == 0, 0, m)
    col = lambda t: jnp.maximum(t - 1, 0)
    wblk = lambda t: jnp.minimum(t, n_blocks - 1)
    w_spec = pl.BlockSpec((None, None, ck, bn), lambda t, m: (layer, wblk(t) // nbe, m, wblk(t) % nbe))
    in_specs = [pl.BlockSpec((bm, D), lambda t, m: (row(t, m), 0)), w_spec, w_spec]
    args = [t, w_gate, w_up]
    if gated:
        in_specs.append(pl.BlockSpec((bm, LANE), lambda t, m: (row(t, m), 0)))
        args.append(gates)
    return pl.pallas_call(
        functools.partial(_swiglu_kernel, gated=gated, n_blocks=n_blocks, nbe=nbe),
        out_shape=jax.ShapeDtypeStruct((M, E * F), BF16),
        grid=(n_blocks + 1, n_m),
        in_specs=in_specs,
        out_specs=pl.BlockSpec((bm, bn), lambda t, m: (row(t, m), col(t))),
        scratch_shapes=[pltpu.VMEM((2, D, 2 * bn), BF16)],
        compiler_params=_cparams(2),
        name="swiglu_up",
    )(*args)


def _router_kernel(t_ref, w_ref, b_ref, o_ref):
    logits = _dot(t_ref[...], w_ref[...].astype(BF16)) + b_ref[...]
    lane = lax.broadcasted_iota(jnp.int32, logits.shape, 1).astype(F32)
    top1 = jnp.max(logits, axis=-1, keepdims=True)
    i1 = jnp.min(jnp.where(logits == top1, lane, float(LANE)), axis=-1, keepdims=True)
    rest = jnp.where(lane == i1, -jnp.inf, logits)
    top2 = jnp.max(rest, axis=-1, keepdims=True)
    i2 = jnp.min(jnp.where(rest == top2, lane, float(LANE)), axis=-1, keepdims=True)
    e2 = jnp.exp(top2 - top1)
    inv = 1.0 / (1.0 + e2)
    o_ref[...] = jnp.where(lane == i1, inv, 0.0) + jnp.where(lane == i2, e2 * inv, 0.0)


def _router(t, router_w, router_b, cfg):
    M, D = t.shape
    E = cfg.n_experts
    assert cfg.top_k == 2 and E <= LANE
    w = jnp.zeros((D, LANE), F32).at[:, :E].set(router_w)
    b = jnp.full((1, LANE), NEG_INF, F32).at[0, :E].set(router_b)
    bm = _pick_block(M, 1152, SUBLANE_BF16)
    return pl.pallas_call(
        _router_kernel,
        out_shape=jax.ShapeDtypeStruct((M, LANE), F32),
        grid=(M // bm,),
        in_specs=[
            pl.BlockSpec((bm, D), lambda m: (m, 0)),
            pl.BlockSpec((D, LANE), lambda m: (0, 0)),
            pl.BlockSpec((1, LANE), lambda m: (0, 0)),
        ],
        out_specs=pl.BlockSpec((bm, LANE), lambda m: (m, 0)),
        compiler_params=_cparams(1),
        name="moe_router",
    )(t, w, b)


def _na_variant_groups(n_groups):
    return (0, 1 if n_groups > 2 else 0, n_groups - 1)


def _na_build_bias(tp_ref, bias_ref, *, rows, W, kh):
    n_groups = rows // kh
    lane = lax.broadcasted_iota(jnp.int32, (W, 2 * W), 1)
    neg = jnp.full((W, 2 * W), NEG_INF, F32)
    for variant, g in enumerate(_na_variant_groups(n_groups)):
        key_row = min(max(g * kh - kh // 2, 0), rows - 2 * kh)
        for ql in range(kh):
            qi = g * kh + ql
            r0 = min(max(qi - kh // 2, 0), rows - kh)
            for j in range(kh):
                ki = key_row + 2 * j
                first_ok = r0 <= ki < r0 + kh
                second_ok = r0 <= ki + 1 < r0 + kh
                if not (first_ok or second_ok):
                    blk = neg
                else:
                    blk = tp_ref[ki - qi + kh]
                    if not second_ok:
                        blk = jnp.where(lane < W, blk, NEG_INF)
                    elif not first_ok:
                        blk = jnp.where(lane >= W, blk, NEG_INF)
                bias_ref[variant, ql * W:(ql + 1) * W, j * 2 * W:(j + 1) * 2 * W] = blk


def _na_kernel(q_ref, k_ref, v_ref, ck_ref, cv_ref, tp_ref, o_ref, bias_ref, *, rows, W, kh, unroll):
    @pl.when(pl.program_id(1) == 0)
    def _():
        _na_build_bias(tp_ref, bias_ref, rows=rows, W=W, kh=kh)

    ck = ck_ref[...]
    cv = cv_ref[...]
    n_groups = rows // kh
    gq, gk = kh * W, 2 * kh * W

    def one_group(g):
        variant = jnp.where(g == 0, 0, jnp.where(g == n_groups - 1, 2, 1))
        key_row = jnp.clip(g * kh - kh // 2, 0, rows - 2 * kh)
        q = q_ref[pl.ds(pl.multiple_of(g * gq, gq), gq), :]
        kstart = pl.multiple_of(key_row * W, W)
        s = _dot_t(q, k_ref[pl.ds(kstart, gk), :]) + bias_ref[variant]
        sc = _dot_t(q, ck)
        m = jnp.maximum(jnp.max(s, axis=-1, keepdims=True), jnp.max(sc, axis=-1, keepdims=True))
        e = jnp.exp2(s - m)
        ec = jnp.exp2(sc - m)
        inv = 1.0 / (jnp.sum(e, axis=-1, keepdims=True) + jnp.sum(ec, axis=-1, keepdims=True))
        o = (_dot(e.astype(BF16), v_ref[pl.ds(kstart, gk), :]) + _dot(ec.astype(BF16), cv)) * inv
        return o.astype(BF16)

    def body(it, carry):
        outs = [one_group(it * unroll + u) for u in range(unroll)]
        start = pl.multiple_of(it * (unroll * gq), unroll * gq)
        o_ref[pl.ds(start, unroll * gq), :] = jnp.concatenate(outs, axis=0)
        return carry

    lax.fori_loop(0, n_groups // unroll, body, 0)


def _na_bias_blocks(rpb, cfg):
    W, kw, wh = cfg.grid_w, cfg.na_win_w, cfg.na_win_h
    H = rpb.shape[0]
    wq = jnp.arange(W)[:, None]
    kc = jnp.arange(W)[None, :]
    c0 = jnp.clip(wq - kw // 2, 0, W - kw)
    col_valid = (kc >= c0) & (kc < c0 + kw)
    cidx = jnp.clip(kc - wq + (kw - 1), 0, 2 * kw - 2)
    toe = jnp.where(col_valid[None, None], rpb[:, :, cidx], NEG_INF).astype(F32)
    pad = jnp.full((H, 1, W, W), NEG_INF, F32)
    toe = jnp.concatenate([pad, toe, pad], axis=1)
    return jnp.concatenate([toe[:, :-1], toe[:, 1:]], axis=-1) * LOG2E


def _na_attention(qkv, rpb, n_out_rows, cfg):
    B, S, L, D, W, Dh = cfg.B, cfg.S, cfg.L, cfg.D, cfg.grid_w, cfg.head_dim
    H = D // Dh
    rows = S // W
    kh = min(cfg.na_win_h, rows)
    n_groups = rows // kh
    assert rows % kh == 0 and n_groups >= 2 and kh == cfg.na_win_h and 2 * W == LANE
    blocks = _na_bias_blocks(rpb, cfg)
    cblk = cfg.MX // L
    return pl.pallas_call(
        functools.partial(_na_kernel, rows=rows, W=W, kh=kh, unroll=math.gcd(rows // kh, NA_GROUP_UNROLL)),
        out_shape=jax.ShapeDtypeStruct((n_out_rows, D), BF16),
        grid=(H, B),
        in_specs=[
            pl.BlockSpec((S, Dh), lambda h, b: (b, h)),
            pl.BlockSpec((S, Dh), lambda h, b: (b, H + h)),
            pl.BlockSpec((S, Dh), lambda h, b: (b, 2 * H + h)),
            pl.BlockSpec((L, Dh), lambda h, b: (cblk + b, H + h)),
            pl.BlockSpec((L, Dh), lambda h, b: (cblk + b, 2 * H + h)),
            pl.BlockSpec((None, 2 * kh, W, 2 * W), lambda h, b: (h, 0, 0, 0)),
        ],
        out_specs=pl.BlockSpec((S, Dh), lambda h, b: (b, h)),
        scratch_shapes=[pltpu.VMEM((3, kh * W, 2 * kh * W), F32)],
        compiler_params=_cparams(2),
        name="neighborhood_attention",
    )(qkv, qkv, qkv, qkv, qkv, blocks)


def _gqa_kernel(*refs, G, has_local, has_sink, has_alias, win, S, Dh):
    refs = list(refs)
    sink_ref = refs.pop(0) if has_sink else None
    q_ref = refs.pop(0)
    k_ref, v_ref, mask_ref = (refs.pop(0), refs.pop(0), refs.pop(0)) if has_local else (None, None, None)
    ck_ref, cv_ref = refs.pop(0), refs.pop(0)
    if has_alias:
        refs.pop(0)
    o_ref = refs.pop(0)
    hk = pl.program_id(1)
    rows = q_ref.shape[0]
    ck = ck_ref[...]
    cv = cv_ref[...]
    if has_local:
        nkw = rows + 2 * win
        start = pl.multiple_of(jnp.clip(pl.program_id(2) * rows - win, 0, S - nkw), win)
        kw = k_ref[pl.ds(start, nkw), :]
        vw = v_ref[pl.ds(start, nkw), :]
    outs = []
    for g in range(G):
        q = q_ref[:, g * Dh:(g + 1) * Dh]
        sc = _dot_t(q, ck)
        m = jnp.max(sc, axis=-1, keepdims=True)
        if has_local:
            s = _dot_t(q, kw) + mask_ref[...]
            m = jnp.maximum(m, jnp.max(s, axis=-1, keepdims=True))
        if has_sink:
            sk = sink_ref[hk, g] * LOG2E
            m = jnp.maximum(m, sk)
        ec = jnp.exp2(sc - m)
        l = jnp.sum(ec, axis=-1, keepdims=True)
        o = _dot(ec.astype(BF16), cv)
        if has_local:
            e = jnp.exp2(s - m)
            l = l + jnp.sum(e, axis=-1, keepdims=True)
            o = o + _dot(e.astype(BF16), vw)
        if has_sink:
            l = l + jnp.exp2(sk - m)
        outs.append((o * (1.0 / l)).astype(BF16))
    o_ref[...] = jnp.concatenate(outs, axis=1)


def _band_mask(rows, win, S):
    nkw = rows + 2 * win
    r = jnp.arange(rows)[:, None]
    c = jnp.arange(nkw)[None, :]
    offsets = (0, -win, -2 * win)
    return jnp.stack([jnp.where(jnp.abs(off + c - r) <= win, 0.0, NEG_INF) for off in offsets]).astype(F32)


def _gqa_attention(qkv, sink, q_heads, kv_heads, local, o_prev, n_out_rows, cfg):
    B, S, L, D, Dh = cfg.B, cfg.S, cfg.L, cfg.D, cfg.head_dim
    G = q_heads // kv_heads
    kcol = q_heads
    vcol = q_heads + kv_heads
    cblk = cfg.MX // L
    has_sink = sink is not None
    win = cfg.swa_window
    rows_step = GQA_QUERY_ROWS if local else L
    nq = S // rows_step if local else 1
    qrow = (lambda b, i: b * nq + i) if local else (lambda b, i: cblk + b)
    in_specs, args = [], []
    if has_sink:
        in_specs.append(pl.BlockSpec(memory_space=pltpu.SMEM))
        args.append(sink.reshape(kv_heads, G).astype(F32))
    in_specs.append(pl.BlockSpec((rows_step, G * Dh), lambda b, h, i: (qrow(b, i), h)))
    args.append(qkv)
    if local:
        assert cfg.swa_block == win and S % rows_step == 0 and nq >= 2 and win % SUBLANE_BF16 == 0
        nkw = rows_step + 2 * win
        variant = lambda i: jnp.where(i == 0, 0, jnp.where(i == nq - 1, 2, 1))
        in_specs += [pl.BlockSpec((S, Dh), lambda b, h, i: (b, kcol + h)),
                     pl.BlockSpec((S, Dh), lambda b, h, i: (b, vcol + h)),
                     pl.BlockSpec((None, rows_step, nkw), lambda b, h, i: (variant(i), 0, 0))]
        args += [qkv, qkv, _band_mask(rows_step, win, S)]
    in_specs += [pl.BlockSpec((L, Dh), lambda b, h, i: (cblk + b, kcol + h)),
                 pl.BlockSpec((L, Dh), lambda b, h, i: (cblk + b, vcol + h))]
    args += [qkv, qkv]
    aliases = {}
    if o_prev is not None:
        in_specs.append(pl.BlockSpec(memory_space=pl.ANY))
        aliases = {len(args): 0}
        args.append(o_prev)
    return pl.pallas_call(
        functools.partial(_gqa_kernel, G=G, has_local=local, has_sink=has_sink,
                          has_alias=o_prev is not None, win=win, S=S, Dh=Dh),
        out_shape=jax.ShapeDtypeStruct((n_out_rows, D), BF16),
        grid=(B, kv_heads, nq),
        in_specs=in_specs,
        out_specs=pl.BlockSpec((rows_step, G * Dh), lambda b, h, i: (qrow(b, i), h)),
        input_output_aliases=aliases,
        compiler_params=_cparams(3),
        name="gqa_window_attention" if local else "gqa_context_attention",
    )(*args)


def _diff_kernel(*refs, has_x, has_alias, chunk, row_split, S, L, lambda_init, Dh):
    refs = list(refs)
    lq1, lk1, lq2, lk2, sub_ref, q_ref = refs[:6]
    refs = refs[6:]
    kx_ref, vx_ref = (refs.pop(0), refs.pop(0)) if has_x else (None, None)
    kc_ref, vc_ref = refs.pop(0), refs.pop(0)
    if has_alias:
        refs.pop(0)
    o_ref, e_ref = refs
    lam = (jnp.exp(jnp.sum(lq1[...] * lk1[...], axis=-1, keepdims=True))
           - jnp.exp(jnp.sum(lq2[...] * lk2[...], axis=-1, keepdims=True)) + lambda_init)
    chunks = []
    if has_x:
        chunks += [(kx_ref, vx_ref, c * chunk, chunk, c * chunk) for c in range(S // chunk)]
    off = S if has_x else 0
    chunks.append((kc_ref, vc_ref, 0, L, off))
    rs = q_ref.shape[0] // row_split
    for r in range(row_split):
        rows = slice(r * rs, (r + 1) * rs)
        coef = []
        for mi in range(2):
            sl = slice(mi * Dh, (mi + 1) * Dh)
            q = q_ref[rows, sl]
            mx = jnp.full((rs, 1), -jnp.inf, F32)
            l = jnp.zeros((rs, 1), F32)
            maxes = []
            for k_ref, _, r0, n, e0 in chunks:
                s = _dot_t(q, k_ref[r0:r0 + n, sl])
                mn = jnp.maximum(mx, jnp.max(s, axis=-1, keepdims=True))
                e = jnp.exp2(s - mn)
                l = l * jnp.exp2(mx - mn) + jnp.sum(e, axis=-1, keepdims=True)
                e_ref[mi, rows, e0:e0 + n] = e
                maxes.append(mn)
                mx = mn
            inv = 1.0 / l
            coef.append([jnp.exp2(mc - mx) * inv for mc in maxes])
        acc = jnp.zeros((rs, 2 * Dh), F32)
        for ci, (_, v_ref, r0, n, e0) in enumerate(chunks):
            p = e_ref[0, rows, e0:e0 + n] * coef[0][ci] - e_ref[1, rows, e0:e0 + n] * (lam * coef[1][ci])
            acc = acc + _dot(p.astype(BF16), v_ref[r0:r0 + n, :])
        ms = jnp.mean(acc * acc, axis=-1, keepdims=True)
        y = acc * lax.rsqrt(ms + NORM_EPS) * sub_ref[...] * (1.0 - lambda_init)
        o_ref[rows, :] = y.astype(BF16)


def _diff_attention(qkv, lams, subln, lambda_init, main, o_prev, n_out_rows, cfg):
    B, S, L, D, Dh = cfg.B, cfg.S, cfg.L, cfg.D, cfg.head_dim
    H = D // (2 * Dh)
    HW = 2 * Dh
    cblk = cfg.MX // L
    bq = DIFF_QUERY_ROWS if main else L
    nq = S // bq if main else 1
    nkeys = S + L if main else L
    qrow = (lambda b, i: b * nq + i) if main else (lambda b, i: cblk + b)
    vec = pl.BlockSpec((1, Dh), lambda b, h, i: (0, 0))
    in_specs = [vec, vec, vec, vec,
                pl.BlockSpec((1, HW), lambda b, h, i: (0, 0)),
                pl.BlockSpec((bq, HW), lambda b, h, i: (qrow(b, i), h))]
    args = [v.reshape(1, Dh).astype(F32) for v in lams] + [subln.reshape(1, HW).astype(F32), qkv]
    if main:
        in_specs += [pl.BlockSpec((S, HW), lambda b, h, i: (b, H + h)),
                     pl.BlockSpec((S, HW), lambda b, h, i: (b, 2 * H + h))]
        args += [qkv, qkv]
    in_specs += [pl.BlockSpec((L, HW), lambda b, h, i: (cblk + b, H + h)),
                 pl.BlockSpec((L, HW), lambda b, h, i: (cblk + b, 2 * H + h))]
    args += [qkv, qkv]
    aliases = {}
    if o_prev is not None:
        in_specs.append(pl.BlockSpec(memory_space=pl.ANY))
        aliases = {len(args): 0}
        args.append(o_prev)
    return pl.pallas_call(
        functools.partial(_diff_kernel, has_x=main, has_alias=o_prev is not None,
                          chunk=math.gcd(S, DIFF_KEY_CHUNK), row_split=min(DIFF_ROW_SPLIT, max(bq // LANE, 1)),
                          S=S, L=L,
                          lambda_init=lambda_init, Dh=Dh),
        out_shape=jax.ShapeDtypeStruct((n_out_rows, D), BF16),
        grid=(B, H, nq),
        in_specs=in_specs,
        out_specs=pl.BlockSpec((bq, HW), lambda b, h, i: (qrow(b, i), h)),
        scratch_shapes=[pltpu.VMEM((2, bq, nkeys), F32)],
        input_output_aliases=aliases,
        compiler_params=_cparams(3),
        name="diff_attention" if main else "diff_context_attention",
    )(*args)


def _rope_tables(cfg):
    Dh = cfg.head_dim
    t = jnp.arange(cfg.S, dtype=jnp.int32)
    row = (t // cfg.grid_w).astype(F32)
    col = (t % cfg.grid_w).astype(F32)
    axis_dim = Dh // 2
    inv_freq = ROPE_BASE ** (-jnp.arange(0, axis_dim, 2, dtype=F32) / axis_dim)
    ang_r = row[:, None] * inv_freq[None, :]
    ang_c = col[:, None] * inv_freq[None, :]
    ang = jnp.concatenate([ang_r, ang_r, ang_c, ang_c], axis=-1)
    cos, sin = jnp.cos(ang), jnp.sin(ang)
    quarter = jnp.arange(Dh) // (Dh // 4)
    sin_a = jnp.where(quarter % 2 == 0, -sin, 0.0)
    sin_b = jnp.where(quarter % 2 == 1, sin, 0.0)
    n_ctx = cfg.B * cfg.L

    def full(tab, fill):
        return jnp.concatenate([jnp.tile(tab, (cfg.B, 1)), jnp.full((n_ctx, Dh), fill, F32)], axis=0)

    return full(cos, 1.0), full(sin_a, 0.0), full(sin_b, 0.0)


def _head_gains(q_gain, k_gain, n_q, n_k, n_v_cols, head_dim):
    q_scale = head_dim ** -0.5 * LOG2E
    return jnp.concatenate([jnp.tile(q_gain.astype(F32) * q_scale, n_q), jnp.tile(k_gain.astype(F32), n_k),
                            jnp.ones((n_v_cols,), F32)]).reshape(1, -1)


def _forward(cfg, x, c, ctx, c_ctx, ada_down, ada_up, ada_bias, norm_mix, norm_ffn,
             na_w_qkv, na_w_o, na_q_norm, na_k_norm, na_rpb,
             diff_w_qkv, diff_w_o, diff_q_norm, diff_k_norm,
             diff_lambda_q1, diff_lambda_k1, diff_lambda_q2, diff_lambda_k2, diff_subln,
             swa_w_qkv, swa_w_o, swa_q_norm, swa_k_norm, swa_sink,
             ffn_w_gate, ffn_w_up, ffn_w_down,
             moe_router, moe_router_bias, moe_w_gate, moe_w_up, moe_w_down):
    B, S, L, D, Dh = cfg.B, cfg.S, cfg.L, cfg.D, cfg.head_dim
    M, MX = cfg.M, cfg.MX
    heads = D // Dh
    h = jnp.concatenate([x.reshape(MX, D), ctx.reshape(B * L, D)], axis=0).astype(F32)
    cond = jnp.zeros((MOD_ROWS, D), F32).at[:B].set(c).at[B].set(c_ctx)
    mods_all = _ada_all(cond, ada_down, ada_up, ada_bias)
    rope_tabs = _rope_tables(cfg)

    for i in range(cfg.depth):
        ctx_out = i < cfg.depth - 1
        n_rows = M if ctx_out else MX
        mods = mods_all[i]
        xn = _modulate(h, norm_mix[i], mods, 0, 1, M, cfg)
        kind, j = i % 3, i // 3
        if kind == 0:
            gains = _head_gains(na_q_norm[j], na_k_norm[j], heads, heads, D, Dh)
            qkv = _qkv_proj(xn, na_w_qkv, j, gains, 2 * D, None, cfg)
            o = _na_attention(qkv, na_rpb[j], n_rows, cfg)
            if ctx_out:
                o = _gqa_attention(qkv, None, heads, heads, False, o, n_rows, cfg)
            w_o = na_w_o
        elif kind == 1:
            gains = _head_gains(diff_q_norm[j], diff_k_norm[j], heads, heads, D, Dh)
            qkv = _qkv_proj(xn, diff_w_qkv, j, gains, 2 * D, rope_tabs, cfg)
            lams = (diff_lambda_q1[j], diff_lambda_k1[j], diff_lambda_q2[j], diff_lambda_k2[j])
            lambda_init = 0.8 - 0.6 * math.exp(-0.3 * i)
            o = _diff_attention(qkv, lams, diff_subln[j], lambda_init, True, None, n_rows, cfg)
            if ctx_out:
                o = _diff_attention(qkv, lams, diff_subln[j], lambda_init, False, o, n_rows, cfg)
            w_o = diff_w_o
        else:
            kvh = heads // cfg.swa_group
            gains = _head_gains(swa_q_norm[j], swa_k_norm[j], heads, kvh, kvh * Dh, Dh)
            qkv = _qkv_proj(xn, swa_w_qkv, j, gains, (heads + kvh) * Dh, rope_tabs, cfg)
            o = _gqa_attention(qkv, swa_sink[j], heads, kvh, True, None, n_rows, cfg)
            if ctx_out:
                o = _gqa_attention(qkv, swa_sink[j], heads, kvh, False, o, n_rows, cfg)
            w_o = swa_w_o
        h = _mm_res(o, w_o, j, h, mods, 2, n_rows, cfg)
        t = _modulate(h, norm_ffn[i], mods, 3, 4, n_rows, cfg)
        fj = i // 2
        if i % 2 == 0:
            n_ffn, _, d_ff = ffn_w_gate.shape
            hid = _swiglu_up(t, ffn_w_gate.reshape(n_ffn, 1, D, d_ff), ffn_w_up.reshape(n_ffn, 1, D, d_ff), fj, None)
            w_down = ffn_w_down
        else:
            gates = _router(t, moe_router[fj], moe_router_bias[fj], cfg)
            hid = _swiglu_up(t, moe_w_gate, moe_w_up, fj, gates)
            w_down = moe_w_down.reshape(moe_w_down.shape[0], -1, D)
        h = _mm_res(hid, w_down, fj, h, mods, 5, n_rows, cfg)
    return h.reshape(B, S, D)


_CFG = Cfg(B=2, S=4096, L=256, D=4096, depth=4, grid_w=64, head_dim=128, na_win_h=8, na_win_w=16,
           swa_window=128, swa_block=128, swa_group=4, n_experts=8, top_k=2)


def kernel(x, c, ctx, c_ctx, ada_down, ada_up, ada_bias, norm_mix, norm_ffn, na_w_qkv, na_w_o, na_q_norm, na_k_norm, na_rpb, diff_w_qkv, diff_w_o, diff_q_norm, diff_k_norm, diff_lambda_q1, diff_lambda_k1, diff_lambda_q2, diff_lambda_k2, diff_subln, swa_w_qkv, swa_w_o, swa_q_norm, swa_k_norm, swa_sink, ffn_w_gate, ffn_w_up, ffn_w_down, moe_router, moe_router_bias, moe_w_gate, moe_w_up, moe_w_down):
    return _forward(_CFG, x, c, ctx, c_ctx, ada_down, ada_up, ada_bias, norm_mix, norm_ffn, na_w_qkv, na_w_o, na_q_norm, na_k_norm, na_rpb, diff_w_qkv, diff_w_o, diff_q_norm, diff_k_norm, diff_lambda_q1, diff_lambda_k1, diff_lambda_q2, diff_lambda_k2, diff_subln, swa_w_qkv, swa_w_o, swa_q_norm, swa_k_norm, swa_sink, ffn_w_gate, ffn_w_up, ffn_w_down, moe_router, moe_router_bias, moe_w_gate, moe_w_up, moe_w_down)
```
